```python
import math
import jax
import jax.numpy as jnp
from jax import lax
import numpy as np

D_MODEL = 1024
BATCH = 8
SEQ = 4096
DEPTH = 4

GDN_HEADS = 4
GDN_DK = 128
GDN_DV = 128
GDN_CONV = 4
GDN_CHUNK = 64
DSA_HEADS = 4
DSA_HEAD_DIM = 64
DSA_Q_RANK = 256
DSA_KV_RANK = 128
IDX_HEADS = 8
IDX_DIM = 32
IDX_TOPK = 256
DSA_QBLOCK = 128
MLSTM_HEADS = 4
MLSTM_DQK = 64
MLSTM_DV = 64
MLSTM_CHUNK = 64
FFN_HIDDEN = (8 * D_MODEL + 3 * 256 - 1) // (3 * 256) * 256
NORM_EPS = 1e-6

IN_WIDTHS = (
    GDN_HEADS * GDN_DK, GDN_HEADS * GDN_DK, GDN_HEADS * GDN_DV, GDN_HEADS * GDN_DV,
    GDN_HEADS, GDN_HEADS,
    DSA_Q_RANK, DSA_KV_RANK, IDX_DIM, IDX_HEADS,
    MLSTM_HEADS * MLSTM_DQK, MLSTM_HEADS * MLSTM_DQK, MLSTM_HEADS * MLSTM_DV,
    MLSTM_HEADS * MLSTM_DV, MLSTM_HEADS, MLSTM_HEADS,
)
IN_DIM = sum(IN_WIDTHS)
MIX_WIDTH = GDN_HEADS * GDN_DV + DSA_HEADS * DSA_HEAD_DIM + MLSTM_HEADS * MLSTM_DV

kernel_name = "hybrid_gdn_dsa_mlstm_trunk"


def rms_norm(x, g):
    xf = x.astype(jnp.float32)
    y = xf * lax.rsqrt(jnp.mean(xf * xf, axis=-1, keepdims=True) + NORM_EPS)
    return (y * g.astype(jnp.float32)).astype(x.dtype)


def l2_normalize(x):
    return x * lax.rsqrt(jnp.sum(x * x, axis=-1, keepdims=True) + NORM_EPS)


def to_heads(x, n_heads):
    b, t, _ = x.shape
    return x.reshape(b, t, n_heads, -1).transpose(0, 2, 1, 3).astype(jnp.float32)


def causal_depthwise_conv(x, w):
    width, t = w.shape[0], x.shape[1]
    xp = jnp.pad(x, ((0, 0), (width - 1, 0), (0, 0)))
    return sum(xp[:, j:j + t] * w[j] for j in range(width))


def split_points(widths):
    pts, acc = [], 0
    for w in widths[:-1]:
        acc += w
        pts.append(acc)
    return pts


def chunk_gated_delta(q, k, v, g, beta):
    b, h, t, dk = q.shape
    dv = v.shape[-1]
    c = GDN_CHUNK
    n = t // c
    q = q * dk ** -0.5
    rs = lambda a: a.reshape(b, h, n, c, *a.shape[3:])
    q, k, v, g, beta = rs(q), rs(k), rs(v), rs(g), rs(beta)
    gc = jnp.cumsum(g, axis=-1)
    incl = jnp.tril(jnp.ones((c, c), dtype=bool))
    strict = jnp.tril(jnp.ones((c, c), dtype=bool), -1)
    decay = jnp.exp(jnp.where(incl, gc[..., :, None] - gc[..., None, :], -jnp.inf))
    kb = k * beta[..., None]
    a_mat = jnp.where(strict, jnp.einsum("bhnid,bhnjd->bhnij", kb, k) * decay, 0.0)
    eye = jnp.eye(c, dtype=jnp.float32)
    t_mat = lax.linalg.triangular_solve(eye + a_mat, jnp.broadcast_to(eye, a_mat.shape),
                                        left_side=True, lower=True)
    u = t_mat @ (v * beta[..., None])
    w = t_mat @ (kb * jnp.exp(gc)[..., None])
    qk = jnp.einsum("bhnid,bhnjd->bhnij", q, k) * decay
    q_dec = q * jnp.exp(gc)[..., None]
    g_last = gc[..., -1]
    k_dec = k * jnp.exp(g_last[..., None] - gc)[..., None]

    def step(s, xs):
        u_c, w_c, qk_c, qd_c, kd_c, gl_c = xs
        v_new = u_c - jnp.einsum("bhck,bhkv->bhcv", w_c, s)
        o = jnp.einsum("bhck,bhkv->bhcv", qd_c, s) + jnp.einsum("bhij,bhjv->bhiv", qk_c, v_new)
        s = s * jnp.exp(gl_c)[..., None, None] + jnp.einsum("bhck,bhcv->bhkv", kd_c, v_new)
        return s, o

    xs = tuple(jnp.moveaxis(a, 2, 0) for a in (u, w, qk, q_dec, k_dec, g_last))
    s0 = jnp.zeros((b, h, dk, dv), jnp.float32)
    _, o = lax.scan(step, s0, xs)
    return jnp.moveaxis(o, 0, 2).reshape(b, h, t, dv)


def gated_deltanet(q, k, v, z, a, bt, conv_w, a_log, dt_bias, out_g):
    bsz, t, _ = q.shape
    qkv = jax.nn.silu(causal_depthwise_conv(jnp.concatenate([q, k, v], axis=-1), conv_w))
    q, k, v = jnp.split(qkv, [GDN_HEADS * GDN_DK, 2 * GDN_HEADS * GDN_DK], axis=-1)
    q = l2_normalize(to_heads(q, GDN_HEADS))
    k = l2_normalize(to_heads(k, GDN_HEADS))
    v = to_heads(v, GDN_HEADS)
    g = -jnp.exp(a_log.astype(jnp.float32)) * jax.nn.softplus(a.astype(jnp.float32) + dt_bias.astype(jnp.float32))
    beta = jax.nn.sigmoid(bt.astype(jnp.float32))
    o = chunk_gated_delta(q, k, v, g.transpose(0, 2, 1), beta.transpose(0, 2, 1))
    o = o.transpose(0, 2, 1, 3)
    zf = z.reshape(bsz, t, GDN_HEADS, GDN_DV).astype(jnp.float32)
    o = rms_norm(o, out_g) * jax.nn.silu(zf)
    return o.reshape(bsz, t, -1)


def dsa_sparse_attention(cq, ckv, k_idx, w_idx, q_norm, kv_norm, w_uq, w_qidx, w_uk, w_uv):
    f32 = jnp.float32
    b, t, _ = cq.shape
    cq = rms_norm(cq, q_norm)
    c_kv = rms_norm(ckv, kv_norm).astype(f32)
    q = (cq @ w_uq).reshape(b, t, DSA_HEADS, DSA_HEAD_DIM).astype(f32)
    q_lat = jnp.einsum("bthd,hcd->bthc", q, w_uk.astype(f32)) * DSA_HEAD_DIM ** -0.5
    q_idx = (cq @ w_qidx).reshape(b, t, IDX_HEADS, IDX_DIM).astype(f32)
    k_i = k_idx.astype(f32)
    w_i = w_idx.astype(f32) * (IDX_HEADS ** -0.5 * IDX_DIM ** -0.5)
    n_sel = min(IDX_TOPK, t // 4)
    key_pos = jnp.arange(t)

    def query_block(blk):
        start = blk * DSA_QBLOCK
        sl = lambda arr: lax.dynamic_slice_in_dim(arr, start, DSA_QBLOCK, axis=1)
        qi, wi, ql = sl(q_idx), sl(w_i), sl(q_lat)
        q_pos = start + jnp.arange(DSA_QBLOCK)
        score = jax.nn.relu(jnp.einsum("bqhd,bsd->bqsh", qi, k_i))
        score = jnp.einsum("bqsh,bqh->bqs", score, wi)
        score = jnp.where(key_pos[None, None, :] <= q_pos[None, :, None], score, -jnp.inf)
        _, sel = lax.top_k(score, n_sel)
        c_sel = jax.vmap(lambda c, ix: c[ix])(c_kv, sel)
        valid = sel <= q_pos[None, :, None]
        logits = jnp.einsum("bqhc,bqkc->bqhk", ql, c_sel)
        logits = jnp.where(valid[:, :, None, :], logits, -jnp.inf)
        p = jax.nn.softmax(logits, axis=-1)
        return jnp.einsum("bqhk,bqkc->bqhc", p, c_sel)

    o_lat = lax.map(query_block, jnp.arange(t // DSA_QBLOCK))
    o_lat = jnp.moveaxis(o_lat, 0, 1).reshape(b, t, DSA_HEADS, DSA_KV_RANK)
    o = jnp.einsum("bthc,hcd->bthd", o_lat, w_uv.astype(f32))
    return o.reshape(b, t, -1)


def chunk_mlstm(q, k, v, li, lf):
    b, h, t, dk = q.shape
    dv = v.shape[-1]
    L = MLSTM_CHUNK
    n = t // L
    ch = lambda a: jnp.moveaxis(a.reshape(b, h, n, L, *a.shape[3:]), 2, 0)
    qc, kc, vc, lic = ch(q), ch(k), ch(v), ch(li)
    bc = jnp.cumsum(ch(lf), axis=-1)
    incl = jnp.tril(jnp.ones((L, L), dtype=bool))

    def step(carry, xs):
        c_st, n_st, m_st = carry
        qx, kx, vx, bx, ix = xs
        d_log = jnp.where(incl, bx[..., :, None] - bx[..., None, :] + ix[..., None, :], -jnp.inf)
        a_log = bx + m_st[..., None]
        m_t = jnp.maximum(a_log, jnp.max(d_log, axis=-1))
        dw = jnp.exp(d_log - m_t[..., None])
        aw = jnp.exp(a_log - m_t)
        s = jnp.einsum("bhid,bhjd->bhij", qx, kx) * dw
        num = aw[..., None] * jnp.einsum("bhid,bhdv->bhiv", qx, c_st) + s @ vx
        den = aw * jnp.einsum("bhid,bhd->bhi", qx, n_st) + jnp.sum(s, axis=-1)
        h_out = num / jnp.maximum(jnp.abs(den), jnp.exp(-m_t))[..., None]
        m_new = m_t[..., -1]
        kw = jnp.exp(bx[..., -1:] - bx + ix - m_new[..., None])
        dec = jnp.exp(bx[..., -1] + m_st - m_new)
        kx_w = kx * kw[..., None]
        c_st = dec[..., None, None] * c_st + jnp.einsum("bhld,bhlv->bhdv", kx_w, vx)
        n_st = dec[..., None] * n_st + jnp.sum(kx_w, axis=2)
        return (c_st, n_st, m_new), h_out

    init = (jnp.zeros((b, h, dk, dv), jnp.float32), jnp.zeros((b, h, dk), jnp.float32),
            jnp.zeros((b, h), jnp.float32))
    _, hs = lax.scan(step, init, (qc, kc, vc, bc, lic))
    return jnp.moveaxis(hs, 0, 2).reshape(b, h, t, dv)


def mlstm(q, k, v, o_pre, i_pre, f_pre, i_bias, f_bias, out_g):
    bsz, t, _ = q.shape
    q = to_heads(q, MLSTM_HEADS)
    k = to_heads(k, MLSTM_HEADS) * MLSTM_DQK ** -0.5
    v = to_heads(v, MLSTM_HEADS)
    li = (i_pre.astype(jnp.float32) + i_bias.astype(jnp.float32)).transpose(0, 2, 1)
    lf = jax.nn.log_sigmoid(f_pre.astype(jnp.float32) + f_bias.astype(jnp.float32)).transpose(0, 2, 1)
    hh = chunk_mlstm(q, k, v, li, lf).transpose(0, 2, 1, 3)
    o_gate = jax.nn.sigmoid(o_pre.reshape(bsz, t, MLSTM_HEADS, MLSTM_DV).astype(jnp.float32))
    return (o_gate * rms_norm(hh, out_g)).reshape(bsz, t, -1)


def setup_inputs(seed: int = 0) -> dict:
    key = jax.random.key(seed)
    ks = jax.random.split(key, 32)
    f32 = jnp.float32
    nrm = lambda k, shape, fan_in: jax.random.normal(k, shape, f32) * fan_in ** -0.5
    gain = lambda k, shape: 1.0 + 0.02 * jax.random.normal(k, shape, f32)
    res_scale = (2 * DEPTH) ** -0.5
    dt = jnp.exp(jax.random.uniform(ks[5], (DEPTH, GDN_HEADS), f32, math.log(1e-3), math.log(1e-1)))
    return {
        "x": jax.random.normal(ks[0], (BATCH, SEQ, D_MODEL), f32),
        "attn_norm": gain(ks[1], (DEPTH, D_MODEL)),
        "w_in": nrm(ks[2], (DEPTH, D_MODEL, IN_DIM), D_MODEL),
        "gdn_conv": nrm(ks[3], (DEPTH, GDN_CONV, 2 * GDN_HEADS * GDN_DK + GDN_HEADS * GDN_DV), GDN_CONV),
        "gdn_a_log": jnp.log(jax.random.uniform(ks[4], (DEPTH, GDN_HEADS), f32, 1.0, 16.0)),
        "gdn_dt_bias": dt + jnp.log(-jnp.expm1(-dt)),
        "gdn_out_norm": gain(ks[6], (DEPTH, GDN_DV)),
        "dsa_q_norm": gain(ks[7], (DEPTH, DSA_Q_RANK)),
        "dsa_kv_norm": gain(ks[8], (DEPTH, DSA_KV_RANK)),
        "dsa_w_uq": nrm(ks[9], (DEPTH, DSA_Q_RANK, DSA_HEADS * DSA_HEAD_DIM), DSA_Q_RANK),
        "dsa_w_qidx": nrm(ks[10], (DEPTH, DSA_Q_RANK, IDX_HEADS * IDX_DIM), DSA_Q_RANK),
        "dsa_w_uk": nrm(ks[11], (DEPTH, DSA_HEADS, DSA_KV_RANK, DSA_HEAD_DIM), DSA_HEAD_DIM),
        "dsa_w_uv": nrm(ks[12], (DEPTH, DSA_HEADS, DSA_KV_RANK, DSA_HEAD_DIM), DSA_KV_RANK),
        "mlstm_i_bias": 0.1 * jax.random.normal(ks[13], (DEPTH, MLSTM_HEADS), f32),
        "mlstm_f_bias": jax.random.uniform(ks[14], (DEPTH, MLSTM_HEADS), f32, 3.0, 6.0),
        "mlstm_out_norm": gain(ks[15], (DEPTH, MLSTM_DV)),
        "w_out": nrm(ks[16], (DEPTH, MIX_WIDTH, D_MODEL), MIX_WIDTH) * res_scale,
        "ffn_norm": gain(ks[17], (DEPTH, D_MODEL)),
        "w_gate": nrm(ks[18], (DEPTH, D_MODEL, FFN_HIDDEN), D_MODEL),
        "w_up": nrm(ks[19], (DEPTH, D_MODEL, FFN_HIDDEN), D_MODEL),
        "w_down": nrm(ks[20], (DEPTH, FFN_HIDDEN, D_MODEL), FFN_HIDDEN) * res_scale,
        "final_norm": gain(ks[21], (D_MODEL,)),
    }


def reference(x, attn_norm, w_in, gdn_conv, gdn_a_log, gdn_dt_bias, gdn_out_norm,
              dsa_q_norm, dsa_kv_norm, dsa_w_uq, dsa_w_qidx, dsa_w_uk, dsa_w_uv,
              mlstm_i_bias, mlstm_f_bias, mlstm_out_norm, w_out, ffn_norm,
              w_gate, w_up, w_down, final_norm):
    pts = split_points(IN_WIDTHS)
    for l in range(DEPTH):
        h = rms_norm(x, attn_norm[l])
        proj = h @ w_in[l]
        (gq, gk, gv, gz, ga, gb, cq, ckv, ik, iw,
         mq, mk, mv, mo, mi, mf) = jnp.split(proj, pts, axis=-1)
        y_a = gated_deltanet(gq, gk, gv, gz, ga, gb, gdn_conv[l], gdn_a_log[l],
                             gdn_dt_bias[l], gdn_out_norm[l])
        y_b = dsa_sparse_attention(cq, ckv, ik, iw, dsa_q_norm[l], dsa_kv_norm[l], dsa_w_uq[l],
                                   dsa_w_qidx[l], dsa_w_uk[l], dsa_w_uv[l])
        y_c = mlstm(mq, mk, mv, mo, mi, mf, mlstm_i_bias[l], mlstm_f_bias[l], mlstm_out_norm[l])
        mix = jnp.concatenate([y_a, y_b, y_c], axis=-1).astype(x.dtype)
        x = x + mix @ w_out[l]
        h = rms_norm(x, ffn_norm[l])
        x = x + (jax.nn.silu(h @ w_gate[l]) * (h @ w_up[l])) @ w_down[l]
    return rms_norm(x, final_norm)
```

```python
import functools
import math

import jax
import jax.numpy as jnp
from jax import lax
from jax.experimental import pallas as pl
from jax.experimental.pallas import tpu as pltpu

F32 = jnp.float32
BF16 = jnp.bfloat16
HIGHEST = lax.Precision.HIGHEST

D_MODEL = 1024
GDN_HEADS = 4
GDN_DK = 128
GDN_DV = 128
GDN_CONV = 4
DSA_HEADS = 4
DSA_HEAD_DIM = 64
DSA_Q_RANK = 256
DSA_KV_RANK = 128
IDX_HEADS = 8
IDX_DIM = 32
IDX_TOPK = 256
MLSTM_HEADS = 4
MLSTM_DQK = 64
MLSTM_DV = 64
FFN_HIDDEN = 2816
NORM_EPS = 1e-6

IN_WIDTHS = (512, 512, 512, 512, 4, 4, 256, 128, 32, 8, 256, 256, 256, 256, 4, 4)

C_GQ, C_GK, C_GV, C_GZ = 0, 512, 1024, 1536
C_CQ, C_CKV, C_IDX = 2048, 2304, 2432
C_MQ, C_MK, C_MV, C_MO = 2560, 2816, 3072, 3328
C_GATE = 3584
NP = 3712
LANES = 128

VMEM_LIMIT = 56 * 1024 * 1024

INT_MIN = -2 ** 31


def _mm(a, b):
    return jnp.dot(a.astype(BF16), b.astype(BF16), preferred_element_type=F32)


def _mm_nt(a, b):
    return lax.dot_general(a.astype(BF16), b.astype(BF16), (((1,), (1,)), ((), ())),
                           preferred_element_type=F32)


def _mm_hi(a, b):
    return jnp.dot(a, b, precision=HIGHEST, preferred_element_type=F32)


def _sigmoid(x):
    return 1.0 / (1.0 + jnp.exp(-x))


def _softplus(x):
    return jnp.maximum(x, 0.0) + jnp.log1p(jnp.exp(-jnp.abs(x)))


def _tri(n, kind):
    r = lax.broadcasted_iota(jnp.int32, (n, n), 0)
    c = lax.broadcasted_iota(jnp.int32, (n, n), 1)
    if kind == "incl":
        return r >= c
    if kind == "strict":
        return r > c
    if kind == "upper_incl":
        return r <= c
    raise ValueError(kind)


def _inproj_kernel(x_ref, g_ref, w_ref, o_ref):
    x = x_ref[...]
    ms = jnp.mean(x * x, axis=-1, keepdims=True)
    h = (x * lax.rsqrt(ms + NORM_EPS)) * g_ref[...]
    o_ref[...] = jnp.dot(h.astype(BF16), w_ref[...], preferred_element_type=F32)


def _inproj(x2, g, w):
    m = x2.shape[0]
    tm = 256
    return pl.pallas_call(
        _inproj_kernel,
        grid=(m // tm,),
        in_specs=[
            pl.BlockSpec((tm, D_MODEL), lambda i: (i, 0)),
            pl.BlockSpec((1, D_MODEL), lambda i: (0, 0)),
            pl.BlockSpec((D_MODEL, NP), lambda i: (0, 0)),
        ],
        out_specs=pl.BlockSpec((tm, NP), lambda i: (i, 0)),
        out_shape=jax.ShapeDtypeStruct((m, NP), F32),
        compiler_params=pltpu.CompilerParams(
            dimension_semantics=("arbitrary",), vmem_limit_bytes=VMEM_LIMIT),
        name="inproj",
    )(x2, g, w)


def _ffn_kernel(x_ref, ya_ref, yb_ref, yc_ref, wo_ref, g_ref, wg_ref, wu_ref, wd_ref, fg_ref,
                o_ref, acc_ref, h_ref, *, n_hidden_steps, final_norm):
    j = pl.program_id(1)

    @pl.when(j == 0)
    def _():
        xn = x_ref[...]
        xn = xn + jnp.dot(ya_ref[...], wo_ref[0:512, :], preferred_element_type=F32)
        xn = xn + jnp.dot(yb_ref[...], wo_ref[512:768, :], preferred_element_type=F32)
        xn = xn + jnp.dot(yc_ref[...], wo_ref[768:1024, :], preferred_element_type=F32)
        acc_ref[...] = xn
        ms = jnp.mean(xn * xn, axis=-1, keepdims=True)
        h_ref[...] = ((xn * lax.rsqrt(ms + NORM_EPS)) * g_ref[...]).astype(BF16)

    h = h_ref[...]
    gate = jnp.dot(h, wg_ref[...], preferred_element_type=F32)
    up = jnp.dot(h, wu_ref[...], preferred_element_type=F32)
    act = (gate * _sigmoid(gate)) * up
    acc_ref[...] += jnp.dot(act.astype(BF16), wd_ref[...], preferred_element_type=F32)

    @pl.when(j == n_hidden_steps - 1)
    def _():
        y = acc_ref[...]
        if final_norm:
            ms = jnp.mean(y * y, axis=-1, keepdims=True)
            y = (y * lax.rsqrt(ms + NORM_EPS)) * fg_ref[...]
        o_ref[...] = y


def _out_ffn(x2, ya, yb, yc, wo, g, wg, wu, wd, fg, final_norm):
    m = x2.shape[0]
    tm = 512
    th = 1408
    nh = FFN_HIDDEN // th
    kern = functools.partial(_ffn_kernel, n_hidden_steps=nh, final_norm=final_norm)
    return pl.pallas_call(
        kern,
        grid=(m // tm, nh),
        in_specs=[
            pl.BlockSpec((tm, D_MODEL), lambda i, j: (i, 0)),
            pl.BlockSpec((tm, 512), lambda i, j: (i, 0)),
            pl.BlockSpec((tm, 256), lambda i, j: (i, 0)),
            pl.BlockSpec((tm, 256), lambda i, j: (i, 0)),
            pl.BlockSpec((D_MODEL, D_MODEL), lambda i, j: (0, 0)),
            pl.BlockSpec((1, D_MODEL), lambda i, j: (0, 0)),
            pl.BlockSpec((D_MODEL, th), lambda i, j: (0, j)),
            pl.BlockSpec((D_MODEL, th), lambda i, j: (0, j)),
            pl.BlockSpec((th, D_MODEL), lambda i, j: (j, 0)),
            pl.BlockSpec((1, D_MODEL), lambda i, j: (0, 0)),
        ],
        out_specs=pl.BlockSpec((tm, D_MODEL), lambda i, j: (i, 0)),
        out_shape=jax.ShapeDtypeStruct((m, D_MODEL), F32),
        scratch_shapes=[pltpu.VMEM((tm, D_MODEL), F32), pltpu.VMEM((tm, D_MODEL), BF16)],
        compiler_params=pltpu.CompilerParams(
            dimension_semantics=("arbitrary", "arbitrary"), vmem_limit_bytes=VMEM_LIMIT),
        name="out_ffn",
    )(x2, ya, yb, yc, wo, g, wg, wu, wd, fg)


def _mlstm_kernel(q_ref, k_ref, v_ref, o_ref, gc_ref, gr_ref, bc_ref, br_ref, og_ref,
                  y_ref, c_s, m_s, *, chunk, n_chunks):
    L = chunk
    t = pl.program_id(1)

    @pl.when(t == 0)
    def _():
        c_s[...] = jnp.zeros_like(c_s)
        m_s[...] = jnp.zeros_like(m_s)

    incl = _tri(L, "incl")
    tril_f = incl.astype(F32)
    triu_f = _tri(L, "upper_incl").astype(F32)
    lane = lax.broadcasted_iota(jnp.int32, (L, MLSTM_DV), 1)
    e0 = (lane == 0).astype(F32)
    scale = MLSTM_DQK ** -0.5

    def body(c, carry):
        r0 = pl.multiple_of(c * L, L)
        x_col = gc_ref[pl.ds(r0, L), :] + bc_ref[...]
        lf_col = -_softplus(-x_col)
        b_col = _mm_hi(tril_f, lf_col)
        x_row = gr_ref[c] + br_ref[...]
        lf_row = -_softplus(-x_row)
        b_row = _mm_hi(lf_row, triu_f)
        outs = []
        for h in range(MLSTM_HEADS):
            sl = slice(h * MLSTM_DQK, (h + 1) * MLSTM_DQK)
            q = q_ref[pl.ds(r0, L), sl]
            k = k_ref[pl.ds(r0, L), sl] * scale
            v = v_ref[pl.ds(r0, L), sl]
            bx_c = b_col[:, 12 + h:13 + h]
            ix_c = x_col[:, 8 + h:9 + h]
            bx_r = b_row[12 + h:13 + h, :]
            ix_r = x_row[8 + h:9 + h, :]
            m_prev = m_s[h:h + 1, 0:1]
            d_log = jnp.where(incl, bx_c - bx_r + ix_r, -jnp.inf)
            a_log = bx_c + m_prev
            m_t = jnp.maximum(a_log, jnp.max(d_log, axis=-1, keepdims=True))
            dw = jnp.exp(d_log - m_t)
            aw = jnp.exp(a_log - m_t)
            s = _mm_nt(q, k) * dw
            v_aug = jnp.concatenate([v, e0], axis=1)
            c_aug = c_s[h]
            num = aw * _mm(q, c_aug) + _mm(s, v_aug)
            den = num[:, MLSTM_DV:MLSTM_DV + 1]
            hh = num[:, :MLSTM_DV] / jnp.maximum(jnp.abs(den), jnp.exp(-m_t))
            m_new = m_t[L - 1:L, :]
            bx_last = bx_c[L - 1:L, :]
            kw = jnp.exp(bx_last - bx_c + ix_c - m_new)
            dec = jnp.exp(bx_last + m_prev - m_new)
            kxw = k * kw
            c_s[h] = dec * c_aug + _mm(kxw.T, v_aug)
            m_s[h:h + 1, :] = jnp.broadcast_to(m_new, (1, LANES))
            ms = jnp.mean(hh * hh, axis=-1, keepdims=True)
            outs.append(hh * lax.rsqrt(ms + NORM_EPS))
        y = jnp.concatenate(outs, axis=1) * og_ref[...]
        y = y * _sigmoid(o_ref[pl.ds(r0, L), :])
        y_ref[pl.ds(r0, L), :] = y.astype(y_ref.dtype)
        return carry

    lax.fori_loop(0, n_chunks, body, 0)


def _mlstm(proj3, gates_r, bias_c, bias_r, og, *, chunk, tb):
    b, t, _ = proj3.shape
    nc = tb // chunk
    wq = MLSTM_HEADS * MLSTM_DQK
    kern = functools.partial(_mlstm_kernel, chunk=chunk, n_chunks=nc)
    col = lambda c0: pl.BlockSpec((None, tb, wq), lambda i, j, c0=c0: (i, j, c0 // wq))
    return pl.pallas_call(
        kern,
        grid=(b, t // tb),
        in_specs=[
            col(C_MQ), col(C_MK), col(C_MV), col(C_MO),
            pl.BlockSpec((None, tb, LANES), lambda i, j: (i, j, C_GATE // LANES)),
            pl.BlockSpec((None, nc, 16, chunk), lambda i, j: (i, j, 0, 0)),
            pl.BlockSpec((1, LANES), lambda i, j: (0, 0)),
            pl.BlockSpec((16, 1), lambda i, j: (0, 0)),
            pl.BlockSpec((1, wq), lambda i, j: (0, 0)),
        ],
        out_specs=pl.BlockSpec((None, tb, wq), lambda i, j: (i, j, 0)),
        out_shape=jax.ShapeDtypeStruct((b, t, wq), BF16),
        scratch_shapes=[pltpu.VMEM((MLSTM_HEADS, MLSTM_DQK, LANES), F32),
                        pltpu.VMEM((8, LANES), F32)],
        compiler_params=pltpu.CompilerParams(
            dimension_semantics=("arbitrary", "arbitrary"), vmem_limit_bytes=VMEM_LIMIT),
        name="mlstm",
    )(proj3, proj3, proj3, proj3, proj3, gates_r, bias_c, bias_r, og)


def _unit_lower_inverse(a_strict, n):
    r = lax.broadcasted_iota(jnp.int32, (n, n), 0)
    c = lax.broadcasted_iota(jnp.int32, (n, n), 1)
    t_mat = (r == c).astype(F32)
    lvl = 0
    while (1 << lvl) < n:
        rb = r >> lvl
        cb = c >> lvl
        off = ((rb & 1) == 1) & (cb == rb - 1)
        x = jnp.where(off, a_strict, 0.0)
        t_mat = t_mat - _mm_hi(_mm_hi(t_mat, x), t_mat)
        lvl += 1
    return t_mat


def _gdn_kernel(q_ref, k_ref, v_ref, z_ref, qh_ref, kh_ref, vh_ref, gc_ref, gr_ref, cw_ref,
                pc_ref, pr_ref, og_ref, y_ref, qs, ks, vs, s_s, *, chunk, n_chunks, tb):
    C = chunk
    t = pl.program_id(1)
    width = GDN_HEADS * GDN_DK

    @pl.when(t == 0)
    def _():
        s_s[...] = jnp.zeros_like(s_s)

    def conv_silu(x_ref, halo_ref, w0):
        halo = jnp.where(t == 0, 0.0, halo_ref[...])
        xf = jnp.concatenate([halo, x_ref[...]], axis=0)
        y = xf * cw_ref[GDN_CONV - 1:GDN_CONV, w0:w0 + width]
        for j in range(GDN_CONV - 1):
            y = y + pltpu.roll(xf, GDN_CONV - 1 - j, axis=0) * cw_ref[j:j + 1, w0:w0 + width]
        y = y[8:, :]
        return y * _sigmoid(y)

    def l2n(x):
        parts = []
        for h in range(GDN_HEADS):
            xh = x[:, h * GDN_DK:(h + 1) * GDN_DK]
            parts.append(xh * lax.rsqrt(jnp.sum(xh * xh, axis=-1, keepdims=True) + NORM_EPS))
        return jnp.concatenate(parts, axis=1)

    qs[...] = l2n(conv_silu(q_ref, qh_ref, 0)) * (GDN_DK ** -0.5)
    ks[...] = l2n(conv_silu(k_ref, kh_ref, width))
    vs[...] = conv_silu(v_ref, vh_ref, 2 * width)

    incl = _tri(C, "incl")
    strict = _tri(C, "strict")
    tril_f = incl.astype(F32)
    triu_f = _tri(C, "upper_incl").astype(F32)

    def body(c, carry):
        r0 = pl.multiple_of(c * C, C)
        gcol = gc_ref[pl.ds(r0, C), :]
        g_col = -jnp.exp(pc_ref[0:1, :]) * _softplus(gcol + pc_ref[1:2, :])
        gcum_col = _mm_hi(tril_f, g_col)
        beta_col = _sigmoid(gcol)
        grow = gr_ref[c]
        g_row = -jnp.exp(pr_ref[:, 0:1]) * _softplus(grow + pr_ref[:, 1:2])
        gcum_row = _mm_hi(g_row, triu_f)
        outs = []
        for h in range(GDN_HEADS):
            sl = slice(h * GDN_DK, (h + 1) * GDN_DK)
            q = qs[pl.ds(r0, C), sl]
            k = ks[pl.ds(r0, C), sl]
            v = vs[pl.ds(r0, C), sl]
            gc_c = gcum_col[:, h:h + 1]
            gc_r = gcum_row[h:h + 1, :]
            beta = beta_col[:, 4 + h:5 + h]
            decay = jnp.where(incl, jnp.exp(jnp.where(incl, gc_c - gc_r, 0.0)), 0.0)
            kb = k * beta
            a_mat = jnp.where(strict, _mm_nt(kb, k) * decay, 0.0)
            t_mat = _unit_lower_inverse(a_mat, C)
            egc = jnp.exp(gc_c)
            u = _mm(t_mat, v * beta)
            w = _mm(t_mat, kb * egc)
            qk = _mm_nt(q, k) * decay
            q_dec = q * egc
            g_last = gc_c[C - 1:C, :]
            k_dec = k * jnp.exp(g_last - gc_c)
            s_prev = s_s[h]
            v_new = u - _mm(w, s_prev)
            o = _mm(q_dec, s_prev) + _mm(qk, v_new)
            s_s[h] = s_prev * jnp.exp(g_last) + _mm(k_dec.T, v_new)
            ms = jnp.mean(o * o, axis=-1, keepdims=True)
            outs.append(o * lax.rsqrt(ms + NORM_EPS))
        y = jnp.concatenate(outs, axis=1) * og_ref[...]
        z = z_ref[pl.ds(r0, C), :]
        y = y * (z * _sigmoid(z))
        y_ref[pl.ds(r0, C), :] = y.astype(y_ref.dtype)
        return carry

    lax.fori_loop(0, n_chunks, body, 0)


def _gdn(proj3, gates_r, conv_w, par_c, par_r, og, *, chunk, tb):
    b, t, _ = proj3.shape
    nc = tb // chunk
    width = GDN_HEADS * GDN_DK
    kern = functools.partial(_gdn_kernel, chunk=chunk, n_chunks=nc, tb=tb)
    col = lambda c0: pl.BlockSpec((None, tb, width), lambda i, j, c0=c0: (i, j, c0 // width))
    halo = lambda c0: pl.BlockSpec(
        (None, 8, width), lambda i, j, c0=c0: (i, jnp.maximum(j * (tb // 8) - 1, 0), c0 // width))
    return pl.pallas_call(
        kern,
        grid=(b, t // tb),
        in_specs=[
            col(C_GQ), col(C_GK), col(C_GV), col(C_GZ),
            halo(C_GQ), halo(C_GK), halo(C_GV),
            pl.BlockSpec((None, tb, LANES), lambda i, j: (i, j, C_GATE // LANES)),
            pl.BlockSpec((None, nc, 16, chunk), lambda i, j: (i, j, 0, 0)),
            pl.BlockSpec((GDN_CONV, 3 * width), lambda i, j: (0, 0)),
            pl.BlockSpec((2, LANES), lambda i, j: (0, 0)),
            pl.BlockSpec((16, 2), lambda i, j: (0, 0)),
            pl.BlockSpec((1, width), lambda i, j: (0, 0)),
        ],
        out_specs=pl.BlockSpec((None, tb, width), lambda i, j: (i, j, 0)),
        out_shape=jax.ShapeDtypeStruct((b, t, width), BF16),
        scratch_shapes=[pltpu.VMEM((tb, width), F32), pltpu.VMEM((tb, width), F32),
                        pltpu.VMEM((tb, width), F32),
                        pltpu.VMEM((GDN_HEADS, GDN_DK, GDN_DV), F32)],
        compiler_params=pltpu.CompilerParams(
            dimension_semantics=("arbitrary", "arbitrary"), vmem_limit_bytes=VMEM_LIMIT),
        name="gdn",
    )(proj3, proj3, proj3, proj3, proj3, proj3, proj3, proj3, gates_r, conv_w, par_c, par_r, og)


def _dsa_prep_kernel(cq_ref, ckv_ref, idx_ref, qn_ref, kn_ref, wuq_ref, wqi_ref, wuk_ref,
                     qlat_ref, qidx_ref, ckvn_ref, kidx_ref):
    cq = cq_ref[...]
    ms = jnp.mean(cq * cq, axis=-1, keepdims=True)
    cqn = ((cq * lax.rsqrt(ms + NORM_EPS)) * qn_ref[...]).astype(BF16)
    ckv = ckv_ref[...]
    ms = jnp.mean(ckv * ckv, axis=-1, keepdims=True)
    ckvn_ref[...] = ((ckv * lax.rsqrt(ms + NORM_EPS)) * kn_ref[...]).astype(BF16)
    q = jnp.dot(cqn, wuq_ref[...], preferred_element_type=F32)
    qlat = jnp.dot(q.astype(BF16), wuk_ref[...], preferred_element_type=F32) * (DSA_HEAD_DIM ** -0.5)
    qlat_ref[...] = qlat.astype(BF16)
    qidx_ref[...] = jnp.dot(cqn, wqi_ref[...], preferred_element_type=F32).astype(BF16)
    kidx_ref[...] = idx_ref[:, 0:IDX_DIM].astype(BF16)


def _dsa_prep(proj2, qn, kn, wuq, wqi, wuk_bd):
    m = proj2.shape[0]
    tm = 512
    full = lambda shape: pl.BlockSpec(shape, lambda i: (0, 0))
    return pl.pallas_call(
        _dsa_prep_kernel,
        grid=(m // tm,),
        in_specs=[
            pl.BlockSpec((tm, DSA_Q_RANK), lambda i: (i, C_CQ // DSA_Q_RANK)),
            pl.BlockSpec((tm, DSA_KV_RANK), lambda i: (i, C_CKV // DSA_KV_RANK)),
            pl.BlockSpec((tm, LANES), lambda i: (i, C_IDX // LANES)),
            full((1, DSA_Q_RANK)), full((1, DSA_KV_RANK)),
            full((DSA_Q_RANK, DSA_HEADS * DSA_HEAD_DIM)),
            full((DSA_Q_RANK, IDX_HEADS * IDX_DIM)),
            full((DSA_HEADS * DSA_HEAD_DIM, DSA_HEADS * DSA_KV_RANK)),
        ],
        out_specs=[
            pl.BlockSpec((tm, DSA_HEADS * DSA_KV_RANK), lambda i: (i, 0)),
            pl.BlockSpec((tm, IDX_HEADS * IDX_DIM), lambda i: (i, 0)),
            pl.BlockSpec((tm, DSA_KV_RANK), lambda i: (i, 0)),
            pl.BlockSpec((tm, IDX_DIM), lambda i: (i, 0)),
        ],
        out_shape=[
            jax.ShapeDtypeStruct((m, DSA_HEADS * DSA_KV_RANK), BF16),
            jax.ShapeDtypeStruct((m, IDX_HEADS * IDX_DIM), BF16),
            jax.ShapeDtypeStruct((m, DSA_KV_RANK), BF16),
            jax.ShapeDtypeStruct((m, IDX_DIM), BF16),
        ],
        compiler_params=pltpu.CompilerParams(
            dimension_semantics=("arbitrary",), vmem_limit_bytes=VMEM_LIMIT),
        name="dsa_prep",
    )(proj2, proj2, proj2, qn, kn, wuq, wqi, wuk_bd)


def _dsa_kernel(qidx_ref, widx_ref, qlat_ref, kidx_ref, ckv_ref, wuv_ref, y_ref, key_s,
                *, qb, tk, n_sel):
    i = pl.program_id(1)
    n_tiles = (i * qb + qb + tk - 1) // tk
    q_pos = i * qb + lax.broadcasted_iota(jnp.int32, (qb, tk), 0)
    lane_pos = lax.broadcasted_iota(jnp.int32, (qb, tk), 1)
    w_scale = IDX_HEADS ** -0.5 * IDX_DIM ** -0.5
    w_all = widx_ref[...] * w_scale
    qidx = qidx_ref[...]

    def score_body(kt, carry):
        k0 = pl.multiple_of(kt * tk, tk)
        kk = kidx_ref[pl.ds(k0, tk), :]
        acc = jnp.zeros((qb, tk), F32)
        for h in range(IDX_HEADS):
            sh = lax.dot_general(qidx[:, h * IDX_DIM:(h + 1) * IDX_DIM], kk,
                                 (((1,), (1,)), ((), ())), preferred_element_type=F32)
            acc = acc + jnp.maximum(sh, 0.0) * w_all[:, IDX_DIM + h:IDX_DIM + h + 1]
        acc = acc + 0.0
        bits = pltpu.bitcast(acc, jnp.int32)
        keys = jnp.where(bits < 0, bits ^ jnp.int32(0x7FFFFFFF), bits)
        keys = jnp.where(k0 + lane_pos <= q_pos, keys, jnp.int32(INT_MIN))
        key_s[kt] = keys
        return carry

    lax.fori_loop(0, n_tiles, score_body, 0)

    thr = jnp.full((qb, 1), INT_MIN, jnp.int32)
    for bit in range(31, -1, -1):
        step = INT_MIN if bit == 31 else (1 << bit)
        cand = thr + jnp.int32(step)

        def count_body(kt, acc, cand=cand):
            keys = key_s[kt]
            for j in range(tk // LANES):
                acc = acc + jnp.where(keys[:, j * LANES:(j + 1) * LANES] >= cand, 1, 0)
            return acc

        cnt = lax.fori_loop(0, n_tiles, count_body, jnp.zeros((qb, LANES), jnp.int32))
        cnt = jnp.sum(cnt, axis=1, keepdims=True)
        thr = jnp.where(cnt >= n_sel, cand, thr)
    thr = jnp.maximum(thr, jnp.int32(INT_MIN + 1))

    qlat = qlat_ref[...]
    neg = jnp.float32(-1e30)

    def attn_body(kt, carry):
        m_run, l_run, acc = carry
        k0 = pl.multiple_of(kt * tk, tk)
        ckv = ckv_ref[pl.ds(k0, tk), :]
        sel = key_s[kt] >= thr
        m_out, l_out, a_out = [], [], []
        for h in range(DSA_HEADS):
            lg = lax.dot_general(qlat[:, h * DSA_KV_RANK:(h + 1) * DSA_KV_RANK], ckv,
                                 (((1,), (1,)), ((), ())), preferred_element_type=F32)
            m_new = jnp.maximum(m_run[h], jnp.max(jnp.where(sel, lg, neg), axis=-1, keepdims=True))
            p = jnp.where(sel, jnp.exp(lg - m_new), 0.0)
            alpha = jnp.exp(m_run[h] - m_new)
            l_out.append(alpha * l_run[h] + jnp.sum(p, axis=-1, keepdims=True))
            a_out.append(alpha * acc[h] + jnp.dot(p.astype(BF16), ckv, preferred_element_type=F32))
            m_out.append(m_new)
        return tuple(m_out), tuple(l_out), tuple(a_out)

    init = (tuple(jnp.full((qb, 1), neg, F32) for _ in range(DSA_HEADS)),
            tuple(jnp.zeros((qb, 1), F32) for _ in range(DSA_HEADS)),
            tuple(jnp.zeros((qb, DSA_KV_RANK), F32) for _ in range(DSA_HEADS)))
    _, l_fin, a_fin = lax.fori_loop(0, n_tiles, attn_body, init)
    outs = []
    for h in range(DSA_HEADS):
        o_lat = a_fin[h] / l_fin[h]
        outs.append(jnp.dot(o_lat.astype(BF16), wuv_ref[h], preferred_element_type=F32))
    y_ref[...] = jnp.concatenate(outs, axis=1).astype(y_ref.dtype)


def _dsa(qidx, proj3, qlat, kidx, ckvn, wuv, *, qb, tk):
    b, t, _ = proj3.shape
    n_sel = min(IDX_TOPK, t // 4)
    kern = functools.partial(_dsa_kernel, qb=qb, tk=tk, n_sel=n_sel)
    return pl.pallas_call(
        kern,
        grid=(b, t // qb),
        in_specs=[
            pl.BlockSpec((None, qb, IDX_HEADS * IDX_DIM), lambda i, j: (i, j, 0)),
            pl.BlockSpec((None, qb, LANES), lambda i, j: (i, j, C_IDX // LANES)),
            pl.BlockSpec((None, qb, DSA_HEADS * DSA_KV_RANK), lambda i, j: (i, j, 0)),
            pl.BlockSpec((None, t, IDX_DIM), lambda i, j: (i, 0, 0)),
            pl.BlockSpec((None, t, DSA_KV_RANK), lambda i, j: (i, 0, 0)),
            pl.BlockSpec((DSA_HEADS, DSA_KV_RANK, DSA_HEAD_DIM), lambda i, j: (0, 0, 0)),
        ],
        out_specs=pl.BlockSpec((None, qb, DSA_HEADS * DSA_HEAD_DIM), lambda i, j: (i, j, 0)),
        out_shape=jax.ShapeDtypeStruct((b, t, DSA_HEADS * DSA_HEAD_DIM), BF16),
        scratch_shapes=[pltpu.VMEM((t // tk, qb, tk), jnp.int32)],
        compiler_params=pltpu.CompilerParams(
            dimension_semantics=("arbitrary", "arbitrary"), vmem_limit_bytes=VMEM_LIMIT),
        name="dsa",
    )(qidx, proj3, qlat, kidx, ckvn, wuv)


def _pack_w_in(w):
    pts = [0]
    for wd in IN_WIDTHS:
        pts.append(pts[-1] + wd)
    seg = [w[:, pts[i]:pts[i + 1]] for i in range(len(IN_WIDTHS))]
    (gq, gk, gv, gz, ga, gb, cq, ckv, ik, iw, mq, mk, mv, mo, mi, mf) = seg
    zeros = lambda n: jnp.zeros((w.shape[0], n), w.dtype)
    idx_blk = jnp.concatenate([ik, iw, zeros(LANES - IDX_DIM - IDX_HEADS)], axis=1)
    gate_blk = jnp.concatenate([ga, gb, mi, mf, zeros(LANES - 16)], axis=1)
    out = jnp.concatenate([gq, gk, gv, gz, cq, ckv, idx_blk, mq, mk, mv, mo, gate_blk], axis=1)
    assert out.shape[1] == NP
    return out.astype(BF16)


def _lane_row(vals_at):
    row = jnp.zeros((LANES,), F32)
    for start, v in vals_at:
        row = row.at[start:start + v.shape[0]].set(v.astype(F32))
    return row


def kernel(x, attn_norm, w_in, gdn_conv, gdn_a_log, gdn_dt_bias, gdn_out_norm, dsa_q_norm,
           dsa_kv_norm, dsa_w_uq, dsa_w_qidx, dsa_w_uk, dsa_w_uv, mlstm_i_bias, mlstm_f_bias,
           mlstm_out_norm, w_out, ffn_norm, w_gate, w_up, w_down, final_norm):
    b, t, d = x.shape
    depth = w_in.shape[0]
    gdn_chunk, mlstm_chunk = 64, 64
    tb = min(512, t)
    qb = 128
    tk = min(512, t)
    x2 = x.reshape(b * t, d)
    for l in range(depth):
        proj2 = _inproj(x2, attn_norm[l].reshape(1, d), _pack_w_in(w_in[l]))
        proj3 = proj2.reshape(b, t, NP)
        gates_t = jnp.swapaxes(proj3[:, :, C_GATE:C_GATE + 16], 1, 2)

        def chunk_rows(c):
            return jnp.swapaxes(gates_t.reshape(b, 16, t // c, c), 1, 2)

        gpar_c = jnp.stack([_lane_row([(0, gdn_a_log[l])]), _lane_row([(0, gdn_dt_bias[l])])])
        gpar_r = gpar_c[:, :16].T
        y_a = _gdn(proj3, chunk_rows(gdn_chunk), gdn_conv[l], gpar_c, gpar_r,
                   jnp.tile(gdn_out_norm[l], GDN_HEADS).reshape(1, -1), chunk=gdn_chunk, tb=tb)

        wuk_bd = jnp.zeros((DSA_HEADS * DSA_HEAD_DIM, DSA_HEADS * DSA_KV_RANK), F32)
        for h in range(DSA_HEADS):
            wuk_bd = wuk_bd.at[h * DSA_HEAD_DIM:(h + 1) * DSA_HEAD_DIM,
                               h * DSA_KV_RANK:(h + 1) * DSA_KV_RANK].set(dsa_w_uk[l, h].T)
        qlat, qidx, ckvn, kidx = _dsa_prep(
            proj2, dsa_q_norm[l].reshape(1, -1), dsa_kv_norm[l].reshape(1, -1),
            dsa_w_uq[l].astype(BF16), dsa_w_qidx[l].astype(BF16), wuk_bd.astype(BF16))
        r3 = lambda a: a.reshape(b, t, a.shape[-1])
        y_b = _dsa(r3(qidx), proj3, r3(qlat), r3(kidx), r3(ckvn), dsa_w_uv[l].astype(BF16),
                   qb=qb, tk=tk)

        mb_c = _lane_row([(8, mlstm_i_bias[l]), (12, mlstm_f_bias[l])]).reshape(1, LANES)
        mb_r = mb_c[0, :16].reshape(16, 1)
        y_c = _mlstm(proj3, chunk_rows(mlstm_chunk), mb_c, mb_r,
                     jnp.tile(mlstm_out_norm[l], MLSTM_HEADS).reshape(1, -1),
                     chunk=mlstm_chunk, tb=tb)

        x2 = _out_ffn(x2, y_a.reshape(b * t, -1), y_b.reshape(b * t, -1), y_c.reshape(b * t, -1),
                      w_out[l].astype(BF16), ffn_norm[l].reshape(1, d), w_gate[l].astype(BF16),
                      w_up[l].astype(BF16), w_down[l].astype(BF16), final_norm.reshape(1, d),
                      final_norm=(l == depth - 1))
    return x2.reshape(b, t, d)
```

```python
import functools
import math

import jax
import jax.numpy as jnp
from jax import lax
from jax.experimental import pallas as pl
from jax.experimental.pallas import tpu as pltpu

F32 = jnp.float32
BF16 = jnp.bfloat16
HIGHEST = lax.Precision.HIGHEST

D_MODEL = 1024
GDN_HEADS = 4
GDN_DK = 128
GDN_DV = 128
GDN_CONV = 4
DSA_HEADS = 4
DSA_HEAD_DIM = 64
DSA_Q_RANK = 256
DSA_KV_RANK = 128
IDX_HEADS = 8
IDX_DIM = 32
IDX_TOPK = 256
MLSTM_HEADS = 4
MLSTM_DQK = 64
MLSTM_DV = 64
FFN_HIDDEN = 2816
NORM_EPS = 1e-6

IN_WIDTHS = (512, 512, 512, 512, 4, 4, 256, 128, 32, 8, 256, 256, 256, 256, 4, 4)

C_GQ, C_GK, C_GV, C_GZ = 0, 512, 1024, 1536
C_CQ, C_CKV, C_IDX = 2048, 2304, 2432
C_MQ, C_MK, C_MV, C_MO = 2560, 2816, 3072, 3328
C_GATE = 3584
NP = 3712
LANES = 128

VMEM_LIMIT = 56 * 1024 * 1024

INT_MIN = -2 ** 31


def _mm(a, b):
    return jnp.dot(a.astype(BF16), b.astype(BF16), preferred_element_type=F32)


def _mm_nt(a, b):
    return lax.dot_general(a.astype(BF16), b.astype(BF16), (((1,), (1,)), ((), ())),
                           preferred_element_type=F32)


def _sigmoid(x):
    return 1.0 / (1.0 + jnp.exp(-x))


def _softplus(x):
    return jnp.maximum(x, 0.0) + jnp.log1p(jnp.exp(-jnp.abs(x)))


def _tri(n, kind):
    r = lax.broadcasted_iota(jnp.int32, (n, n), 0)
    c = lax.broadcasted_iota(jnp.int32, (n, n), 1)
    if kind == "incl":
        return r >= c
    if kind == "strict":
        return r > c
    if kind == "upper_incl":
        return r <= c
    raise ValueError(kind)


def _inproj_kernel(x_ref, g_ref, w_ref, o_ref):
    x = x_ref[...]
    ms = jnp.mean(x * x, axis=-1, keepdims=True)
    h = (x * lax.rsqrt(ms + NORM_EPS)) * g_ref[...]
    o_ref[...] = jnp.dot(h.astype(BF16), w_ref[...], preferred_element_type=F32)


def _inproj(x2, g, w):
    m = x2.shape[0]
    tm = 256
    return pl.pallas_call(
        _inproj_kernel,
        grid=(m // tm,),
        in_specs=[
            pl.BlockSpec((tm, D_MODEL), lambda i: (i, 0)),
            pl.BlockSpec((1, D_MODEL), lambda i: (0, 0)),
            pl.BlockSpec((D_MODEL, NP), lambda i: (0, 0)),
        ],
        out_specs=pl.BlockSpec((tm, NP), lambda i: (i, 0)),
        out_shape=jax.ShapeDtypeStruct((m, NP), F32),
        compiler_params=pltpu.CompilerParams(
            dimension_semantics=("arbitrary",), vmem_limit_bytes=VMEM_LIMIT),
        name="inproj",
    )(x2, g, w)


def _ffn_kernel(x_ref, ya_ref, yb_ref, yc_ref, wo_ref, g_ref, wg_ref, wu_ref, wd_ref, fg_ref,
                o_ref, acc_ref, h_ref, *, n_hidden_steps, final_norm):
    j = pl.program_id(1)

    @pl.when(j == 0)
    def _():
        xn = x_ref[...]
        xn = xn + jnp.dot(ya_ref[...], wo_ref[0:512, :], preferred_element_type=F32)
        xn = xn + jnp.dot(yb_ref[...], wo_ref[512:768, :], preferred_element_type=F32)
        xn = xn + jnp.dot(yc_ref[...], wo_ref[768:1024, :], preferred_element_type=F32)
        acc_ref[...] = xn
        ms = jnp.mean(xn * xn, axis=-1, keepdims=True)
        h_ref[...] = ((xn * lax.rsqrt(ms + NORM_EPS)) * g_ref[...]).astype(BF16)

    h = h_ref[...]
    gate = jnp.dot(h, wg_ref[...], preferred_element_type=F32)
    up = jnp.dot(h, wu_ref[...], preferred_element_type=F32)
    act = (gate * _sigmoid(gate)) * up
    acc_ref[...] += jnp.dot(act.astype(BF16), wd_ref[...], preferred_element_type=F32)

    @pl.when(j == n_hidden_steps - 1)
    def _():
        y = acc_ref[...]
        if final_norm:
            ms = jnp.mean(y * y, axis=-1, keepdims=True)
            y = (y * lax.rsqrt(ms + NORM_EPS)) * fg_ref[...]
        o_ref[...] = y


def _out_ffn(x2, ya, yb, yc, wo, g, wg, wu, wd, fg, final_norm):
    m = x2.shape[0]
    tm = 512
    th = 1408
    nh = FFN_HIDDEN // th
    kern = functools.partial(_ffn_kernel, n_hidden_steps=nh, final_norm=final_norm)
    return pl.pallas_call(
        kern,
        grid=(m // tm, nh),
        in_specs=[
            pl.BlockSpec((tm, D_MODEL), lambda i, j: (i, 0)),
            pl.BlockSpec((tm, 512), lambda i, j: (i, 0)),
            pl.BlockSpec((tm, 256), lambda i, j: (i, 0)),
            pl.BlockSpec((tm, 256), lambda i, j: (i, 0)),
            pl.BlockSpec((D_MODEL, D_MODEL), lambda i, j: (0, 0)),
            pl.BlockSpec((1, D_MODEL), lambda i, j: (0, 0)),
            pl.BlockSpec((D_MODEL, th), lambda i, j: (0, j)),
            pl.BlockSpec((D_MODEL, th), lambda i, j: (0, j)),
            pl.BlockSpec((th, D_MODEL), lambda i, j: (j, 0)),
            pl.BlockSpec((1, D_MODEL), lambda i, j: (0, 0)),
        ],
        out_specs=pl.BlockSpec((tm, D_MODEL), lambda i, j: (i, 0)),
        out_shape=jax.ShapeDtypeStruct((m, D_MODEL), F32),
        scratch_shapes=[pltpu.VMEM((tm, D_MODEL), F32), pltpu.VMEM((tm, D_MODEL), BF16)],
        compiler_params=pltpu.CompilerParams(
            dimension_semantics=("arbitrary", "arbitrary"), vmem_limit_bytes=VMEM_LIMIT),
        name="out_ffn",
    )(x2, ya, yb, yc, wo, g, wg, wu, wd, fg)


def _mlstm_kernel(q_ref, k_ref, v_ref, o_ref, gc_ref, gr_ref, bc_ref, br_ref, og_ref,
                  y_ref, sv_s, kv_s, bcol_s, dm_s, km_s, c_s, m_s, *, chunk, n_chunks, group):
    L = chunk
    t = pl.program_id(1)
    heads = range(MLSTM_HEADS)

    @pl.when(t == 0)
    def _():
        c_s[...] = jnp.zeros_like(c_s)
        m_s[...] = jnp.zeros_like(m_s)

    incl = _tri(L, "incl")
    tril_b = incl.astype(BF16)
    triu_b = _tri(L, "upper_incl").astype(BF16)
    e0 = (lax.broadcasted_iota(jnp.int32, (L, MLSTM_DV), 1) == 0).astype(F32)
    lane_l = lax.broadcasted_iota(jnp.int32, (L, LANES), 1)
    lane_8 = lax.broadcasted_iota(jnp.int32, (8, LANES), 1)
    scale = MLSTM_DQK ** -0.5

    def prep(cg, carry):
        ch = []
        gate_tiles = []
        for g in range(group):
            c = cg * group + g
            r0 = pl.multiple_of(c * L, L)
            x_col = gc_ref[pl.ds(r0, L), :] + bc_ref[...]
            x_row = gr_ref[c] + br_ref[...]
            b_col = _cumsum_rows(tril_b, -_softplus(-x_col))
            b_row = _cumsum_lanes(-_softplus(-x_row), triu_b)
            dm_tile = jnp.zeros((L, LANES), F32)
            km_tile = jnp.zeros((8, LANES), F32)
            for h in heads:
                sl = slice(h * MLSTM_DQK, (h + 1) * MLSTM_DQK)
                bx_c = b_col[:, 12 + h:13 + h]
                d_log = jnp.where(incl, bx_c - b_row[12 + h:13 + h, :] + x_row[8 + h:9 + h, :], -jnp.inf)
                dmax = jnp.max(d_log, axis=-1, keepdims=True)
                kmax = dmax[L - 1:L, :]
                kw = jnp.exp(bx_c[L - 1:L, :] - bx_c + x_col[:, 8 + h:9 + h] - kmax)
                k = k_ref[pl.ds(r0, L), sl] * scale
                v_aug = jnp.concatenate([v_ref[pl.ds(r0, L), sl], e0], axis=1)
                ch.append(dict(idx=c * MLSTM_HEADS + h, q=q_ref[pl.ds(r0, L), sl], k=k, v=v_aug,
                               dw=jnp.exp(d_log - dmax), kxw_t=(k * kw).T))
                dm_tile = jnp.where(lane_l == h, dmax, dm_tile)
                km_tile = jnp.where(lane_8 == h, kmax, km_tile)
            gate_tiles.append((c, b_col, dm_tile, km_tile))
        for d in ch:
            d["s"] = _mm_nt(d["q"], d["k"]) * d["dw"]
        for d in ch:
            d["sv"] = _mm(d["s"], d["v"])
        for d in ch:
            d["kv"] = _mm(d["kxw_t"], d["v"])
        for d in ch:
            sv_s[d["idx"]] = d["sv"]
            kv_s[d["idx"]] = d["kv"]
        for c, b_col, dm_tile, km_tile in gate_tiles:
            bcol_s[c] = b_col
            dm_s[c] = dm_tile
            km_s[c] = km_tile
        return carry

    lax.fori_loop(0, n_chunks // group, prep, 0)

    def scan(c, carry):
        r0 = pl.multiple_of(c * L, L)
        b_col = bcol_s[c]
        dm = dm_s[c]
        km = km_s[c]
        c_prev = [c_s[h] for h in heads]
        qc = [_mm(q_ref[pl.ds(r0, L), h * MLSTM_DQK:(h + 1) * MLSTM_DQK], c_prev[h]) for h in heads]
        outs = []
        for h in heads:
            idx = c * MLSTM_HEADS + h
            bx_c = b_col[:, 12 + h:13 + h]
            dmax = dm[:, h:h + 1]
            kmax = km[0:1, h:h + 1]
            m_prev = m_s[h:h + 1, 0:1]
            a_log = bx_c + m_prev
            m_t = jnp.maximum(a_log, dmax)
            num = jnp.exp(a_log - m_t) * qc[h] + jnp.exp(dmax - m_t) * sv_s[idx]
            den = num[:, MLSTM_DV:MLSTM_DV + 1]
            hh = num[:, :MLSTM_DV] / jnp.maximum(jnp.abs(den), jnp.exp(-m_t))
            m_new = m_t[L - 1:L, :]
            dec = jnp.exp(bx_c[L - 1:L, :] + m_prev - m_new)
            c_s[h] = dec * c_prev[h] + jnp.exp(kmax - m_new) * kv_s[idx]
            m_s[h:h + 1, :] = jnp.broadcast_to(m_new, (1, LANES))
            ms = jnp.mean(hh * hh, axis=-1, keepdims=True)
            outs.append(hh * lax.rsqrt(ms + NORM_EPS))
        y = jnp.concatenate(outs, axis=1) * og_ref[...]
        y = y * _sigmoid(o_ref[pl.ds(r0, L), :])
        y_ref[pl.ds(r0, L), :] = y.astype(y_ref.dtype)
        return carry

    lax.fori_loop(0, n_chunks, scan, 0)


def _mlstm(proj3, gates_r, bias_c, bias_r, og, *, chunk, tb):
    b, t, _ = proj3.shape
    nc = tb // chunk
    nch = nc * MLSTM_HEADS
    wq = MLSTM_HEADS * MLSTM_DQK
    group = 2 if nc % 2 == 0 else 1
    kern = functools.partial(_mlstm_kernel, chunk=chunk, n_chunks=nc, group=group)
    col = lambda c0: pl.BlockSpec((None, tb, wq), lambda i, j, c0=c0: (i, j, c0 // wq))
    return pl.pallas_call(
        kern,
        grid=(b, t // tb),
        in_specs=[
            col(C_MQ), col(C_MK), col(C_MV), col(C_MO),
            pl.BlockSpec((None, tb, LANES), lambda i, j: (i, j, C_GATE // LANES)),
            pl.BlockSpec((None, nc, 16, chunk), lambda i, j: (i, j, 0, 0)),
            pl.BlockSpec((1, LANES), lambda i, j: (0, 0)),
            pl.BlockSpec((16, 1), lambda i, j: (0, 0)),
            pl.BlockSpec((1, wq), lambda i, j: (0, 0)),
        ],
        out_specs=pl.BlockSpec((None, tb, wq), lambda i, j: (i, j, 0)),
        out_shape=jax.ShapeDtypeStruct((b, t, wq), BF16),
        scratch_shapes=[pltpu.VMEM((nch, chunk, LANES), F32),
                        pltpu.VMEM((nch, MLSTM_DQK, LANES), F32),
                        pltpu.VMEM((nc, chunk, LANES), F32),
                        pltpu.VMEM((nc, chunk, LANES), F32),
                        pltpu.VMEM((nc, 8, LANES), F32),
                        pltpu.VMEM((MLSTM_HEADS, MLSTM_DQK, LANES), F32),
                        pltpu.VMEM((8, LANES), F32)],
        compiler_params=pltpu.CompilerParams(
            dimension_semantics=("arbitrary", "arbitrary"), vmem_limit_bytes=VMEM_LIMIT),
        name="mlstm",
    )(proj3, proj3, proj3, proj3, proj3, gates_r, bias_c, bias_r, og)


def _split3(x):
    x1 = x.astype(BF16)
    r1 = x - x1.astype(F32)
    x2 = r1.astype(BF16)
    x3 = (r1 - x2.astype(F32)).astype(BF16)
    return x1, x2, x3


def _cumsum_rows(tril_b, x):
    return sum(jnp.dot(tril_b, p, preferred_element_type=F32) for p in _split3(x))


def _cumsum_lanes(x, triu_b):
    return sum(jnp.dot(p, triu_b, preferred_element_type=F32) for p in _split3(x))


def _inverse_masks(n):
    r = lax.broadcasted_iota(jnp.int32, (n, n), 0)
    c = lax.broadcasted_iota(jnp.int32, (n, n), 1)
    offs = []
    lvl = 0
    while (1 << lvl) < n:
        rb = r >> lvl
        cb = c >> lvl
        offs.append(((rb & 1) == 1) & (cb == rb - 1))
        lvl += 1
    return (r == c).astype(F32), offs


def _gdn_kernel(q_ref, k_ref, v_ref, z_ref, qh_ref, kh_ref, vh_ref, gc_ref, gr_ref, cw_ref,
                pc_ref, pr_ref, og_ref, y_ref, qs, ks, vs, u_s, w_s, qk_s, qd_s, kdt_s, el_s, s_s,
                *, chunk, n_chunks, tb, group):
    C = chunk
    t = pl.program_id(1)
    width = GDN_HEADS * GDN_DK

    @pl.when(t == 0)
    def _():
        s_s[...] = jnp.zeros_like(s_s)

    def conv_silu(x_ref, halo_ref, w0):
        halo = jnp.where(t == 0, 0.0, halo_ref[...])
        xf = jnp.concatenate([halo, x_ref[...]], axis=0)
        y = xf * cw_ref[GDN_CONV - 1:GDN_CONV, w0:w0 + width]
        for j in range(GDN_CONV - 1):
            y = y + pltpu.roll(xf, GDN_CONV - 1 - j, axis=0) * cw_ref[j:j + 1, w0:w0 + width]
        y = y[8:, :]
        return y * _sigmoid(y)

    def l2n(x):
        parts = []
        for h in range(GDN_HEADS):
            xh = x[:, h * GDN_DK:(h + 1) * GDN_DK]
            parts.append(xh * lax.rsqrt(jnp.sum(xh * xh, axis=-1, keepdims=True) + NORM_EPS))
        return jnp.concatenate(parts, axis=1)

    qs[...] = l2n(conv_silu(q_ref, qh_ref, 0)) * (GDN_DK ** -0.5)
    ks[...] = l2n(conv_silu(k_ref, kh_ref, width))
    vs[...] = conv_silu(v_ref, vh_ref, 2 * width)

    incl = _tri(C, "incl")
    strict = _tri(C, "strict")
    tril_b = incl.astype(BF16)
    triu_b = _tri(C, "upper_incl").astype(BF16)
    inv_masks = _inverse_masks(C)

    def prep(cg, carry):
        loaded = []
        for g in range(group):
            c = cg * group + g
            r0 = pl.multiple_of(c * C, C)
            gcol = gc_ref[pl.ds(r0, C), :]
            grow = gr_ref[c]
            qkv = [(qs[pl.ds(r0, C), h * GDN_DK:(h + 1) * GDN_DK],
                    ks[pl.ds(r0, C), h * GDN_DK:(h + 1) * GDN_DK],
                    vs[pl.ds(r0, C), h * GDN_DK:(h + 1) * GDN_DK]) for h in range(GDN_HEADS)]
            loaded.append((c, gcol, grow, qkv))
        ch = []
        for c, gcol, grow, qkv in loaded:
            g_col = -jnp.exp(pc_ref[0:1, :]) * _softplus(gcol + pc_ref[1:2, :])
            gcum_col = _cumsum_rows(tril_b, g_col)
            beta_col = _sigmoid(gcol)
            g_row = -jnp.exp(pr_ref[:, 0:1]) * _softplus(grow + pr_ref[:, 1:2])
            gcum_row = _cumsum_lanes(g_row, triu_b)
            for h in range(GDN_HEADS):
                q, k, v = qkv[h]
                gc_c = gcum_col[:, h:h + 1]
                gc_r = gcum_row[h:h + 1, :]
                beta = beta_col[:, 4 + h:5 + h]
                decay = jnp.where(incl, jnp.exp(jnp.where(incl, gc_c - gc_r, 0.0)), 0.0)
                ch.append(dict(idx=c * GDN_HEADS + h, q=q, k=k, v=v, gc_c=gc_c, beta=beta,
                               decay=decay, kb=k * beta))
        for d in ch:
            d["a"] = jnp.where(strict, _mm_nt(d["kb"], d["k"]) * d["decay"], 0.0)
        eye, offs = inv_masks
        for d in ch:
            d["t"] = eye - jnp.where(offs[0], d["a"], 0.0)
        for off in offs[1:]:
            for d in ch:
                d["p"] = _mm(d["t"], jnp.where(off, d["a"], 0.0))
            for d in ch:
                d["t"] = d["t"] - _mm(d["p"], d["t"])
        for d in ch:
            d["egc"] = jnp.exp(d["gc_c"])
            d["u"] = _mm(d["t"], d["v"] * d["beta"])
        for d in ch:
            d["w"] = _mm(d["t"], d["kb"] * d["egc"]).astype(BF16)
        for d in ch:
            d["qk"] = (_mm_nt(d["q"], d["k"]) * d["decay"]).astype(BF16)
        for d in ch:
            idx = d["idx"]
            g_last = d["gc_c"][C - 1:C, :]
            u_s[idx] = d["u"]
            w_s[idx] = d["w"]
            qk_s[idx] = d["qk"]
            qd_s[idx] = (d["q"] * d["egc"]).astype(BF16)
            kdt_s[idx] = (d["k"] * jnp.exp(g_last - d["gc_c"])).T.astype(BF16)
            el_s[idx] = jnp.broadcast_to(jnp.exp(g_last), (8, LANES))
        return carry

    lax.fori_loop(0, n_chunks // group, prep, 0)

    def scan(c, carry):
        r0 = pl.multiple_of(c * C, C)
        heads = range(GDN_HEADS)
        dot = functools.partial(jnp.dot, preferred_element_type=F32)
        idx = [c * GDN_HEADS + h for h in heads]
        s_prev = [s_s[h] for h in heads]
        s_b = [s.astype(BF16) for s in s_prev]
        ws = [dot(w_s[idx[h]], s_b[h]) for h in heads]
        qs_ = [dot(qd_s[idx[h]], s_b[h]) for h in heads]
        v_b = [(u_s[idx[h]] - ws[h]).astype(BF16) for h in heads]
        upd = [dot(kdt_s[idx[h]], v_b[h]) for h in heads]
        o_all = [qs_[h] + dot(qk_s[idx[h]], v_b[h]) for h in heads]
        outs = []
        for h in heads:
            s_s[h] = s_prev[h] * el_s[idx[h]][0:1, :] + upd[h]
            o = o_all[h]
            ms = jnp.mean(o * o, axis=-1, keepdims=True)
            outs.append(o * lax.rsqrt(ms + NORM_EPS))
        y = jnp.concatenate(outs, axis=1) * og_ref[...]
        z = z_ref[pl.ds(r0, C), :]
        y = y * (z * _sigmoid(z))
        y_ref[pl.ds(r0, C), :] = y.astype(y_ref.dtype)
        return carry

    lax.fori_loop(0, n_chunks, scan, 0)


def _gdn(proj3, gates_r, conv_w, par_c, par_r, og, *, chunk, tb):
    b, t, _ = proj3.shape
    nc = tb // chunk
    nch = nc * GDN_HEADS
    width = GDN_HEADS * GDN_DK
    group = 2 if nc % 2 == 0 else 1
    kern = functools.partial(_gdn_kernel, chunk=chunk, n_chunks=nc, tb=tb, group=group)
    col = lambda c0: pl.BlockSpec((None, tb, width), lambda i, j, c0=c0: (i, j, c0 // width))
    halo = lambda c0: pl.BlockSpec(
        (None, 8, width), lambda i, j, c0=c0: (i, jnp.maximum(j * (tb // 8) - 1, 0), c0 // width))
    return pl.pallas_call(
        kern,
        grid=(b, t // tb),
        in_specs=[
            col(C_GQ), col(C_GK), col(C_GV), col(C_GZ),
            halo(C_GQ), halo(C_GK), halo(C_GV),
            pl.BlockSpec((None, tb, LANES), lambda i, j: (i, j, C_GATE // LANES)),
            pl.BlockSpec((None, nc, 16, chunk), lambda i, j: (i, j, 0, 0)),
            pl.BlockSpec((GDN_CONV, 3 * width), lambda i, j: (0, 0)),
            pl.BlockSpec((2, LANES), lambda i, j: (0, 0)),
            pl.BlockSpec((16, 2), lambda i, j: (0, 0)),
            pl.BlockSpec((1, width), lambda i, j: (0, 0)),
        ],
        out_specs=pl.BlockSpec((None, tb, width), lambda i, j: (i, j, 0)),
        out_shape=jax.ShapeDtypeStruct((b, t, width), BF16),
        scratch_shapes=[pltpu.VMEM((tb, width), F32), pltpu.VMEM((tb, width), F32),
                        pltpu.VMEM((tb, width), F32),
                        pltpu.VMEM((nch, chunk, GDN_DV), F32),
                        pltpu.VMEM((nch, chunk, GDN_DK), BF16),
                        pltpu.VMEM((nch, chunk, chunk), BF16),
                        pltpu.VMEM((nch, chunk, GDN_DK), BF16),
                        pltpu.VMEM((nch, GDN_DK, chunk), BF16),
                        pltpu.VMEM((nch, 8, LANES), F32),
                        pltpu.VMEM((GDN_HEADS, GDN_DK, GDN_DV), F32)],
        compiler_params=pltpu.CompilerParams(
            dimension_semantics=("arbitrary", "arbitrary"), vmem_limit_bytes=VMEM_LIMIT),
        name="gdn",
    )(proj3, proj3, proj3, proj3, proj3, proj3, proj3, proj3, gates_r, conv_w, par_c, par_r, og)


def _dsa_prep_kernel(cq_ref, ckv_ref, idx_ref, qn_ref, kn_ref, wuq_ref, wqi_ref, wuk_ref,
                     qlat_ref, qidx_ref, ckvn_ref, kidx_ref):
    cq = cq_ref[...]
    ms = jnp.mean(cq * cq, axis=-1, keepdims=True)
    cqn = ((cq * lax.rsqrt(ms + NORM_EPS)) * qn_ref[...]).astype(BF16)
    ckv = ckv_ref[...]
    ms = jnp.mean(ckv * ckv, axis=-1, keepdims=True)
    ckvn_ref[...] = ((ckv * lax.rsqrt(ms + NORM_EPS)) * kn_ref[...]).astype(BF16)
    q = jnp.dot(cqn, wuq_ref[...], preferred_element_type=F32)
    qlat = jnp.dot(q.astype(BF16), wuk_ref[...], preferred_element_type=F32) * (DSA_HEAD_DIM ** -0.5)
    qlat_ref[...] = qlat.astype(BF16)
    qidx_ref[...] = jnp.dot(cqn, wqi_ref[...], preferred_element_type=F32).astype(BF16)
    kidx_ref[...] = idx_ref[:, 0:IDX_DIM].astype(BF16)


def _dsa_prep(proj2, qn, kn, wuq, wqi, wuk_bd):
    m = proj2.shape[0]
    tm = 512
    full = lambda shape: pl.BlockSpec(shape, lambda i: (0, 0))
    return pl.pallas_call(
        _dsa_prep_kernel,
        grid=(m // tm,),
        in_specs=[
            pl.BlockSpec((tm, DSA_Q_RANK), lambda i: (i, C_CQ // DSA_Q_RANK)),
            pl.BlockSpec((tm, DSA_KV_RANK), lambda i: (i, C_CKV // DSA_KV_RANK)),
            pl.BlockSpec((tm, LANES), lambda i: (i, C_IDX // LANES)),
            full((1, DSA_Q_RANK)), full((1, DSA_KV_RANK)),
            full((DSA_Q_RANK, DSA_HEADS * DSA_HEAD_DIM)),
            full((DSA_Q_RANK, IDX_HEADS * IDX_DIM)),
            full((DSA_HEADS * DSA_HEAD_DIM, DSA_HEADS * DSA_KV_RANK)),
        ],
        out_specs=[
            pl.BlockSpec((tm, DSA_HEADS * DSA_KV_RANK), lambda i: (i, 0)),
            pl.BlockSpec((tm, IDX_HEADS * IDX_DIM), lambda i: (i, 0)),
            pl.BlockSpec((tm, DSA_KV_RANK), lambda i: (i, 0)),
            pl.BlockSpec((tm, IDX_DIM), lambda i: (i, 0)),
        ],
        out_shape=[
            jax.ShapeDtypeStruct((m, DSA_HEADS * DSA_KV_RANK), BF16),
            jax.ShapeDtypeStruct((m, IDX_HEADS * IDX_DIM), BF16),
            jax.ShapeDtypeStruct((m, DSA_KV_RANK), BF16),
            jax.ShapeDtypeStruct((m, IDX_DIM), BF16),
        ],
        compiler_params=pltpu.CompilerParams(
            dimension_semantics=("arbitrary",), vmem_limit_bytes=VMEM_LIMIT),
        name="dsa_prep",
    )(proj2, proj2, proj2, qn, kn, wuq, wqi, wuk_bd)


def _dsa_kernel(qidx_ref, widx_ref, qlat_ref, kidx_ref, ckv_ref, wuv_ref, y_ref,
                key_s, hi_s, lo_s, acc_s, *, qb, tk, n_sel):
    i = pl.program_id(1)
    n_tiles = (i * qb + qb + tk - 1) // tk
    q_pos = i * qb + lax.broadcasted_iota(jnp.int32, (qb, tk), 0)
    lane_pos = lax.broadcasted_iota(jnp.int32, (qb, tk), 1)
    w_scale = IDX_HEADS ** -0.5 * IDX_DIM ** -0.5
    w_all = widx_ref[...] * w_scale
    qidx = qidx_ref[...]
    i16_min = -(1 << 15)

    def score_body(kt, carry):
        k0 = pl.multiple_of(kt * tk, tk)
        kk = kidx_ref[pl.ds(k0, tk), :]
        sh = [lax.dot_general(qidx[:, h * IDX_DIM:(h + 1) * IDX_DIM], kk,
                              (((1,), (1,)), ((), ())), preferred_element_type=F32)
              for h in range(IDX_HEADS)]
        acc = jnp.zeros((qb, tk), F32)
        for h in range(IDX_HEADS):
            acc = acc + jnp.maximum(sh[h], 0.0) * w_all[:, IDX_DIM + h:IDX_DIM + h + 1]
        acc = acc + 0.0
        bits = pltpu.bitcast(acc, jnp.int32)
        keys = jnp.where(bits < 0, bits ^ jnp.int32(0x7FFFFFFF), bits)
        keys = jnp.where(k0 + lane_pos <= q_pos, keys, jnp.int32(INT_MIN))
        key_s[kt] = keys
        hi_s[kt] = (keys >> 16).astype(jnp.int16)
        return carry

    lax.fori_loop(0, n_tiles, score_body, 0)

    def count_ge(src_s, cand):
        cand16 = jnp.broadcast_to(cand, (qb, LANES)).astype(jnp.int16)
        one, zero = jnp.int16(1), jnp.int16(0)

        def body(kt, acc):
            tile = src_s[kt]
            for j in range(tk // LANES):
                acc = acc + jnp.where(tile[:, j * LANES:(j + 1) * LANES] >= cand16, one, zero)
            return acc

        cnt = lax.fori_loop(0, n_tiles, body, jnp.zeros((qb, LANES), jnp.int16))
        return jnp.sum(cnt.astype(jnp.int32), axis=1, keepdims=True)

    def kth_largest16(src_s, rank):
        thr = jnp.full((qb, 1), i16_min, jnp.int32)
        for bit in range(15, -1, -1):
            cand = thr + jnp.int32(1 << bit)
            thr = jnp.where(count_ge(src_s, cand) >= rank, cand, thr)
        return thr

    hi_thr = kth_largest16(hi_s, jnp.int32(n_sel))
    above = count_ge(hi_s, jnp.minimum(hi_thr + 1, jnp.int32((1 << 15) - 1)))
    rank_lo = jnp.int32(n_sel) - jnp.where(hi_thr < (1 << 15) - 1, above, 0)

    def lo_body(kt, carry):
        keys = key_s[kt]
        lo = (keys & jnp.int32(0xFFFF)) + jnp.int32(i16_min)
        lo_s[kt] = jnp.where((keys >> 16) == hi_thr, lo, jnp.int32(i16_min)).astype(jnp.int16)
        return carry

    lax.fori_loop(0, n_tiles, lo_body, 0)
    lo_thr = kth_largest16(lo_s, rank_lo)
    thr = hi_thr * jnp.int32(1 << 16) + (lo_thr - jnp.int32(i16_min))
    thr = jnp.maximum(thr, jnp.int32(INT_MIN + 1))

    qlat = qlat_ref[...]
    neg = jnp.float32(-1e30)
    heads = range(DSA_HEADS)
    acc_s[...] = jnp.zeros_like(acc_s)

    def attn_body(kt, carry):
        m_run, l_run = carry
        k0 = pl.multiple_of(kt * tk, tk)
        ckv = ckv_ref[pl.ds(k0, tk), :]
        sel = key_s[kt] >= thr
        lg = [lax.dot_general(qlat[:, h * DSA_KV_RANK:(h + 1) * DSA_KV_RANK], ckv,
                              (((1,), (1,)), ((), ())), preferred_element_type=F32) for h in heads]
        m_out, l_out, p_all, alphas = [], [], [], []
        for h in heads:
            m_new = jnp.maximum(m_run[h], jnp.max(jnp.where(sel, lg[h], neg), axis=-1, keepdims=True))
            p = jnp.where(sel, jnp.exp(lg[h] - m_new), 0.0)
            alpha = jnp.exp(m_run[h] - m_new)
            l_out.append(alpha * l_run[h] + jnp.sum(p, axis=-1, keepdims=True))
            m_out.append(m_new)
            p_all.append(p.astype(BF16))
            alphas.append(alpha)
        pv = [jnp.dot(p_all[h], ckv, preferred_element_type=F32) for h in heads]
        for h in heads:
            acc_s[h] = alphas[h] * acc_s[h] + pv[h]
        return tuple(m_out), tuple(l_out)

    init = (tuple(jnp.full((qb, 1), neg, F32) for _ in heads),
            tuple(jnp.zeros((qb, 1), F32) for _ in heads))
    _, l_fin = lax.fori_loop(0, n_tiles, attn_body, init)
    o_lat = [(acc_s[h] / l_fin[h]).astype(BF16) for h in heads]
    outs = [jnp.dot(o_lat[h], wuv_ref[h], preferred_element_type=F32) for h in heads]
    y_ref[...] = jnp.concatenate(outs, axis=1).astype(y_ref.dtype)


def _dsa(qidx, proj3, qlat, kidx, ckvn, wuv, *, qb, tk):
    b, t, _ = proj3.shape
    n_sel = min(IDX_TOPK, t // 4)
    kern = functools.partial(_dsa_kernel, qb=qb, tk=tk, n_sel=n_sel)
    return pl.pallas_call(
        kern,
        grid=(b, t // qb),
        in_specs=[
            pl.BlockSpec((None, qb, IDX_HEADS * IDX_DIM), lambda i, j: (i, j, 0)),
            pl.BlockSpec((None, qb, LANES), lambda i, j: (i, j, C_IDX // LANES)),
            pl.BlockSpec((None, qb, DSA_HEADS * DSA_KV_RANK), lambda i, j: (i, j, 0)),
            pl.BlockSpec((None, t, IDX_DIM), lambda i, j: (i, 0, 0)),
            pl.BlockSpec((None, t, DSA_KV_RANK), lambda i, j: (i, 0, 0)),
            pl.BlockSpec((DSA_HEADS, DSA_KV_RANK, DSA_HEAD_DIM), lambda i, j: (0, 0, 0)),
        ],
        out_specs=pl.BlockSpec((None, qb, DSA_HEADS * DSA_HEAD_DIM), lambda i, j: (i, j, 0)),
        out_shape=jax.ShapeDtypeStruct((b, t, DSA_HEADS * DSA_HEAD_DIM), BF16),
        scratch_shapes=[pltpu.VMEM((t // tk, qb, tk), jnp.int32),
                        pltpu.VMEM((t // tk, qb, tk), jnp.int16),
                        pltpu.VMEM((t // tk, qb, tk), jnp.int16),
                        pltpu.VMEM((DSA_HEADS, qb, DSA_KV_RANK), F32)],
        compiler_params=pltpu.CompilerParams(
            dimension_semantics=("arbitrary", "arbitrary"), vmem_limit_bytes=VMEM_LIMIT),
        name="dsa",
    )(qidx, proj3, qlat, kidx, ckvn, wuv)


def _pack_w_in(w):
    pts = [0]
    for wd in IN_WIDTHS:
        pts.append(pts[-1] + wd)
    seg = [w[:, pts[i]:pts[i + 1]] for i in range(len(IN_WIDTHS))]
    (gq, gk, gv, gz, ga, gb, cq, ckv, ik, iw, mq, mk, mv, mo, mi, mf) = seg
    zeros = lambda n: jnp.zeros((w.shape[0], n), w.dtype)
    idx_blk = jnp.concatenate([ik, iw, zeros(LANES - IDX_DIM - IDX_HEADS)], axis=1)
    gate_blk = jnp.concatenate([ga, gb, mi, mf, zeros(LANES - 16)], axis=1)
    out = jnp.concatenate([gq, gk, gv, gz, cq, ckv, idx_blk, mq, mk, mv, mo, gate_blk], axis=1)
    assert out.shape[1] == NP
    return out.astype(BF16)


def _lane_row(vals_at):
    row = jnp.zeros((LANES,), F32)
    for start, v in vals_at:
        row = row.at[start:start + v.shape[0]].set(v.astype(F32))
    return row


def kernel(x, attn_norm, w_in, gdn_conv, gdn_a_log, gdn_dt_bias, gdn_out_norm, dsa_q_norm,
           dsa_kv_norm, dsa_w_uq, dsa_w_qidx, dsa_w_uk, dsa_w_uv, mlstm_i_bias, mlstm_f_bias,
           mlstm_out_norm, w_out, ffn_norm, w_gate, w_up, w_down, final_norm):
    b, t, d = x.shape
    depth = w_in.shape[0]
    gdn_chunk, mlstm_chunk = 64, 64
    tb = min(512, t)
    qb = min(256, t)
    tk = min(512, t)
    x2 = x.reshape(b * t, d)
    for l in range(depth):
        proj2 = _inproj(x2, attn_norm[l].reshape(1, d), _pack_w_in(w_in[l]))
        proj3 = proj2.reshape(b, t, NP)
        gates_t = jnp.swapaxes(proj3[:, :, C_GATE:C_GATE + 16], 1, 2)

        def chunk_rows(c):
            return jnp.swapaxes(gates_t.reshape(b, 16, t // c, c), 1, 2)

        gpar_c = jnp.stack([_lane_row([(0, gdn_a_log[l])]), _lane_row([(0, gdn_dt_bias[l])])])
        gpar_r = gpar_c[:, :16].T
        y_a = _gdn(proj3, chunk_rows(gdn_chunk), gdn_conv[l], gpar_c, gpar_r,
                   jnp.tile(gdn_out_norm[l], GDN_HEADS).reshape(1, -1), chunk=gdn_chunk, tb=tb)

        wuk_bd = jnp.zeros((DSA_HEADS * DSA_HEAD_DIM, DSA_HEADS * DSA_KV_RANK), F32)
        for h in range(DSA_HEADS):
            wuk_bd = wuk_bd.at[h * DSA_HEAD_DIM:(h + 1) * DSA_HEAD_DIM,
                               h * DSA_KV_RANK:(h + 1) * DSA_KV_RANK].set(dsa_w_uk[l, h].T)
        qlat, qidx, ckvn, kidx = _dsa_prep(
            proj2, dsa_q_norm[l].reshape(1, -1), dsa_kv_norm[l].reshape(1, -1),
            dsa_w_uq[l].astype(BF16), dsa_w_qidx[l].astype(BF16), wuk_bd.astype(BF16))
        r3 = lambda a: a.reshape(b, t, a.shape[-1])
        y_b = _dsa(r3(qidx), proj3, r3(qlat), r3(kidx), r3(ckvn), dsa_w_uv[l].astype(BF16),
                   qb=qb, tk=tk)

        mb_c = _lane_row([(8, mlstm_i_bias[l]), (12, mlstm_f_bias[l])]).reshape(1, LANES)
        mb_r = mb_c[0, :16].reshape(16, 1)
        y_c = _mlstm(proj3, chunk_rows(mlstm_chunk), mb_c, mb_r,
                     jnp.tile(mlstm_out_norm[l], MLSTM_HEADS).reshape(1, -1),
                     chunk=mlstm_chunk, tb=tb)

        x2 = _out_ffn(x2, y_a.reshape(b * t, -1), y_b.reshape(b * t, -1), y_c.reshape(b * t, -1),
                      w_out[l].astype(BF16), ffn_norm[l].reshape(1, d), w_gate[l].astype(BF16),
                      w_up[l].astype(BF16), w_down[l].astype(BF16), final_norm.reshape(1, d),
                      final_norm=(l == depth - 1))
    return x2.reshape(b, t, d)
```

```python
import functools
import math

import jax
import jax.numpy as jnp
from jax import lax
from jax.experimental import pallas as pl
from jax.experimental.pallas import tpu as pltpu

F32 = jnp.float32
BF16 = jnp.bfloat16
HIGHEST = lax.Precision.HIGHEST

D_MODEL = 1024
GDN_HEADS = 4
GDN_DK = 128
GDN_DV = 128
GDN_CONV = 4
DSA_HEADS = 4
DSA_HEAD_DIM = 64
DSA_Q_RANK = 256
DSA_KV_RANK = 128
IDX_HEADS = 8
IDX_DIM = 32
IDX_TOPK = 256
MLSTM_HEADS = 4
MLSTM_DQK = 64
MLSTM_DV = 64
FFN_HIDDEN = 2816
NORM_EPS = 1e-6

IN_WIDTHS = (512, 512, 512, 512, 4, 4, 256, 128, 32, 8, 256, 256, 256, 256, 4, 4)

C_GQ, C_GK, C_GV, C_GZ = 0, 512, 1024, 1536
C_CQ, C_CKV, C_IDX = 2048, 2304, 2432
C_MQ, C_MK, C_MV, C_MO = 2560, 2816, 3072, 3328
C_GATE = 3584
NP = 3712
LANES = 128

VMEM_LIMIT = 56 * 1024 * 1024

INT_MIN = -2 ** 31


def _mm(a, b):
    return jnp.dot(a.astype(BF16), b.astype(BF16), preferred_element_type=F32)


def _mm_nt(a, b):
    return lax.dot_general(a.astype(BF16), b.astype(BF16), (((1,), (1,)), ((), ())),
                           preferred_element_type=F32)


def _sigmoid(x):
    return 1.0 / (1.0 + jnp.exp(-x))


def _softplus(x):
    return jnp.maximum(x, 0.0) + jnp.log1p(jnp.exp(-jnp.abs(x)))


def _tri(n, kind):
    r = lax.broadcasted_iota(jnp.int32, (n, n), 0)
    c = lax.broadcasted_iota(jnp.int32, (n, n), 1)
    if kind == "incl":
        return r >= c
    if kind == "strict":
        return r > c
    if kind == "upper_incl":
        return r <= c
    raise ValueError(kind)


def _inproj_kernel(x_ref, g_ref, w_ref, o_ref):
    x = x_ref[...]
    ms = jnp.mean(x * x, axis=-1, keepdims=True)
    h = (x * lax.rsqrt(ms + NORM_EPS)) * g_ref[...]
    o_ref[...] = jnp.dot(h.astype(BF16), w_ref[...], preferred_element_type=F32)


def _inproj(x2, g, w):
    m = x2.shape[0]
    tm = 256
    return pl.pallas_call(
        _inproj_kernel,
        grid=(m // tm,),
        in_specs=[
            pl.BlockSpec((tm, D_MODEL), lambda i: (i, 0)),
            pl.BlockSpec((1, D_MODEL), lambda i: (0, 0)),
            pl.BlockSpec((D_MODEL, NP), lambda i: (0, 0)),
        ],
        out_specs=pl.BlockSpec((tm, NP), lambda i: (i, 0)),
        out_shape=jax.ShapeDtypeStruct((m, NP), F32),
        compiler_params=pltpu.CompilerParams(
            dimension_semantics=("arbitrary",), vmem_limit_bytes=VMEM_LIMIT),
        name="inproj",
    )(x2, g, w)


def _ffn_kernel(x_ref, ya_ref, yb_ref, yc_ref, wo_ref, g_ref, wg_ref, wu_ref, wd_ref, fg_ref,
                o_ref, acc_ref, h_ref, *, n_hidden_steps, final_norm):
    j = pl.program_id(1)

    @pl.when(j == 0)
    def _():
        xn = x_ref[...]
        xn = xn + jnp.dot(ya_ref[...], wo_ref[0:512, :], preferred_element_type=F32)
        xn = xn + jnp.dot(yb_ref[...], wo_ref[512:768, :], preferred_element_type=F32)
        xn = xn + jnp.dot(yc_ref[...], wo_ref[768:1024, :], preferred_element_type=F32)
        acc_ref[...] = xn
        ms = jnp.mean(xn * xn, axis=-1, keepdims=True)
        h_ref[...] = ((xn * lax.rsqrt(ms + NORM_EPS)) * g_ref[...]).astype(BF16)

    h = h_ref[...]
    gate = jnp.dot(h, wg_ref[...], preferred_element_type=F32)
    up = jnp.dot(h, wu_ref[...], preferred_element_type=F32)
    act = (gate * _sigmoid(gate)) * up
    acc_ref[...] += jnp.dot(act.astype(BF16), wd_ref[...], preferred_element_type=F32)

    @pl.when(j == n_hidden_steps - 1)
    def _():
        y = acc_ref[...]
        if final_norm:
            ms = jnp.mean(y * y, axis=-1, keepdims=True)
            y = (y * lax.rsqrt(ms + NORM_EPS)) * fg_ref[...]
        o_ref[...] = y


def _out_ffn(x2, ya, yb, yc, wo, g, wg, wu, wd, fg, final_norm):
    m = x2.shape[0]
    tm = 512
    th = 1408
    nh = FFN_HIDDEN // th
    kern = functools.partial(_ffn_kernel, n_hidden_steps=nh, final_norm=final_norm)
    return pl.pallas_call(
        kern,
        grid=(m // tm, nh),
        in_specs=[
            pl.BlockSpec((tm, D_MODEL), lambda i, j: (i, 0)),
            pl.BlockSpec((tm, 512), lambda i, j: (i, 0)),
            pl.BlockSpec((tm, 256), lambda i, j: (i, 0)),
            pl.BlockSpec((tm, 256), lambda i, j: (i, 0)),
            pl.BlockSpec((D_MODEL, D_MODEL), lambda i, j: (0, 0)),
            pl.BlockSpec((1, D_MODEL), lambda i, j: (0, 0)),
            pl.BlockSpec((D_MODEL, th), lambda i, j: (0, j)),
            pl.BlockSpec((D_MODEL, th), lambda i, j: (0, j)),
            pl.BlockSpec((th, D_MODEL), lambda i, j: (j, 0)),
            pl.BlockSpec((1, D_MODEL), lambda i, j: (0, 0)),
        ],
        out_specs=pl.BlockSpec((tm, D_MODEL), lambda i, j: (i, 0)),
        out_shape=jax.ShapeDtypeStruct((m, D_MODEL), F32),
        scratch_shapes=[pltpu.VMEM((tm, D_MODEL), F32), pltpu.VMEM((tm, D_MODEL), BF16)],
        compiler_params=pltpu.CompilerParams(
            dimension_semantics=("arbitrary", "arbitrary"), vmem_limit_bytes=VMEM_LIMIT),
        name="out_ffn",
    )(x2, ya, yb, yc, wo, g, wg, wu, wd, fg)


def _mlstm_kernel(q_ref, k_ref, v_ref, o_ref, gc_ref, gr_ref, bc_ref, br_ref, og_ref,
                  y_ref, sv_s, kv_s, bcol_s, dm_s, km_s, c_s, m_s, *, chunk, n_chunks, group):
    L = chunk
    t = pl.program_id(1)
    heads = range(MLSTM_HEADS)

    @pl.when(t == 0)
    def _():
        c_s[...] = jnp.zeros_like(c_s)
        m_s[...] = jnp.zeros_like(m_s)

    incl = _tri(L, "incl")
    tril_b = incl.astype(BF16)
    triu_b = _tri(L, "upper_incl").astype(BF16)
    e0 = (lax.broadcasted_iota(jnp.int32, (L, MLSTM_DV), 1) == 0).astype(F32)
    lane_l = lax.broadcasted_iota(jnp.int32, (L, LANES), 1)
    lane_8 = lax.broadcasted_iota(jnp.int32, (8, LANES), 1)
    scale = MLSTM_DQK ** -0.5

    def prep(cg, carry):
        ch = []
        gate_tiles = []
        for g in range(group):
            c = cg * group + g
            r0 = pl.multiple_of(c * L, L)
            x_col = gc_ref[pl.ds(r0, L), :] + bc_ref[...]
            x_row = gr_ref[c] + br_ref[...]
            b_col = _cumsum_rows(tril_b, -_softplus(-x_col))
            b_row = _cumsum_lanes(-_softplus(-x_row), triu_b)
            dm_tile = jnp.zeros((L, LANES), F32)
            km_tile = jnp.zeros((8, LANES), F32)
            for h in heads:
                sl = slice(h * MLSTM_DQK, (h + 1) * MLSTM_DQK)
                bx_c = b_col[:, 12 + h:13 + h]
                d_log = jnp.where(incl, bx_c - b_row[12 + h:13 + h, :] + x_row[8 + h:9 + h, :], -jnp.inf)
                dmax = jnp.max(d_log, axis=-1, keepdims=True)
                kmax = dmax[L - 1:L, :]
                kw = jnp.exp(bx_c[L - 1:L, :] - bx_c + x_col[:, 8 + h:9 + h] - kmax)
                k = k_ref[pl.ds(r0, L), sl] * scale
                v_aug = jnp.concatenate([v_ref[pl.ds(r0, L), sl], e0], axis=1)
                ch.append(dict(idx=c * MLSTM_HEADS + h, q=q_ref[pl.ds(r0, L), sl], k=k, v=v_aug,
                               dw=jnp.exp(d_log - dmax), kxw_t=(k * kw).T))
                dm_tile = jnp.where(lane_l == h, dmax, dm_tile)
                km_tile = jnp.where(lane_8 == h, kmax, km_tile)
            gate_tiles.append((c, b_col, dm_tile, km_tile))
        for d in ch:
            d["s"] = _mm_nt(d["q"], d["k"]) * d["dw"]
        for d in ch:
            d["sv"] = _mm(d["s"], d["v"])
        for d in ch:
            d["kv"] = _mm(d["kxw_t"], d["v"])
        for d in ch:
            sv_s[d["idx"]] = d["sv"]
            kv_s[d["idx"]] = d["kv"]
        for c, b_col, dm_tile, km_tile in gate_tiles:
            bcol_s[c] = b_col
            dm_s[c] = dm_tile
            km_s[c] = km_tile
        return carry

    lax.fori_loop(0, n_chunks // group, prep, 0)

    def scan(c, carry):
        r0 = pl.multiple_of(c * L, L)
        b_col = bcol_s[c]
        dm = dm_s[c]
        km = km_s[c]
        c_prev = [c_s[h] for h in heads]
        qc = [_mm(q_ref[pl.ds(r0, L), h * MLSTM_DQK:(h + 1) * MLSTM_DQK], c_prev[h]) for h in heads]
        outs = []
        for h in heads:
            idx = c * MLSTM_HEADS + h
            bx_c = b_col[:, 12 + h:13 + h]
            dmax = dm[:, h:h + 1]
            kmax = km[0:1, h:h + 1]
            m_prev = m_s[h:h + 1, 0:1]
            a_log = bx_c + m_prev
            m_t = jnp.maximum(a_log, dmax)
            num = jnp.exp(a_log - m_t) * qc[h] + jnp.exp(dmax - m_t) * sv_s[idx]
            den = num[:, MLSTM_DV:MLSTM_DV + 1]
            hh = num[:, :MLSTM_DV] / jnp.maximum(jnp.abs(den), jnp.exp(-m_t))
            m_new = m_t[L - 1:L, :]
            dec = jnp.exp(bx_c[L - 1:L, :] + m_prev - m_new)
            c_s[h] = dec * c_prev[h] + jnp.exp(kmax - m_new) * kv_s[idx]
            m_s[h:h + 1, :] = jnp.broadcast_to(m_new, (1, LANES))
            ms = jnp.mean(hh * hh, axis=-1, keepdims=True)
            outs.append(hh * lax.rsqrt(ms + NORM_EPS))
        y = jnp.concatenate(outs, axis=1) * og_ref[...]
        y = y * _sigmoid(o_ref[pl.ds(r0, L), :])
        y_ref[pl.ds(r0, L), :] = y.astype(y_ref.dtype)
        return carry

    lax.fori_loop(0, n_chunks, scan, 0)


def _mlstm(proj3, gates_r, bias_c, bias_r, og, *, chunk, tb):
    b, t, _ = proj3.shape
    nc = tb // chunk
    nch = nc * MLSTM_HEADS
    wq = MLSTM_HEADS * MLSTM_DQK
    group = 2 if nc % 2 == 0 else 1
    kern = functools.partial(_mlstm_kernel, chunk=chunk, n_chunks=nc, group=group)
    col = lambda c0: pl.BlockSpec((None, tb, wq), lambda i, j, c0=c0: (i, j, c0 // wq))
    return pl.pallas_call(
        kern,
        grid=(b, t // tb),
        in_specs=[
            col(C_MQ), col(C_MK), col(C_MV), col(C_MO),
            pl.BlockSpec((None, tb, LANES), lambda i, j: (i, j, C_GATE // LANES)),
            pl.BlockSpec((None, nc, 16, chunk), lambda i, j: (i, j, 0, 0)),
            pl.BlockSpec((1, LANES), lambda i, j: (0, 0)),
            pl.BlockSpec((16, 1), lambda i, j: (0, 0)),
            pl.BlockSpec((1, wq), lambda i, j: (0, 0)),
        ],
        out_specs=pl.BlockSpec((None, tb, wq), lambda i, j: (i, j, 0)),
        out_shape=jax.ShapeDtypeStruct((b, t, wq), BF16),
        scratch_shapes=[pltpu.VMEM((nch, chunk, LANES), F32),
                        pltpu.VMEM((nch, MLSTM_DQK, LANES), F32),
                        pltpu.VMEM((nc, chunk, LANES), F32),
                        pltpu.VMEM((nc, chunk, LANES), F32),
                        pltpu.VMEM((nc, 8, LANES), F32),
                        pltpu.VMEM((MLSTM_HEADS, MLSTM_DQK, LANES), F32),
                        pltpu.VMEM((8, LANES), F32)],
        compiler_params=pltpu.CompilerParams(
            dimension_semantics=("arbitrary", "arbitrary"), vmem_limit_bytes=VMEM_LIMIT),
        name="mlstm",
    )(proj3, proj3, proj3, proj3, proj3, gates_r, bias_c, bias_r, og)


def _split3(x):
    x1 = x.astype(BF16)
    r1 = x - x1.astype(F32)
    x2 = r1.astype(BF16)
    x3 = (r1 - x2.astype(F32)).astype(BF16)
    return x1, x2, x3


def _cumsum_rows(tril_b, x):
    return sum(jnp.dot(tril_b, p, preferred_element_type=F32) for p in _split3(x))


def _cumsum_lanes(x, triu_b):
    return sum(jnp.dot(p, triu_b, preferred_element_type=F32) for p in _split3(x))


def _inverse_masks(n):
    r = lax.broadcasted_iota(jnp.int32, (n, n), 0)
    c = lax.broadcasted_iota(jnp.int32, (n, n), 1)
    offs = []
    lvl = 0
    while (1 << lvl) < n:
        rb = r >> lvl
        cb = c >> lvl
        offs.append(((rb & 1) == 1) & (cb == rb - 1))
        lvl += 1
    return (r == c).astype(F32), offs


def _gdn_kernel(q_ref, k_ref, v_ref, z_ref, qh_ref, kh_ref, vh_ref, gc_ref, gr_ref, cw_ref,
                pc_ref, pr_ref, og_ref, y_ref, qs, ks, vs, u_s, w_s, qk_s, qd_s, kdt_s, el_s, s_s,
                *, chunk, n_chunks, tb, group):
    C = chunk
    t = pl.program_id(1)
    width = GDN_HEADS * GDN_DK

    @pl.when(t == 0)
    def _():
        s_s[...] = jnp.zeros_like(s_s)

    def conv_silu(x_ref, halo_ref, w0):
        halo = jnp.where(t == 0, 0.0, halo_ref[...])
        xf = jnp.concatenate([halo, x_ref[...]], axis=0)
        y = xf * cw_ref[GDN_CONV - 1:GDN_CONV, w0:w0 + width]
        for j in range(GDN_CONV - 1):
            y = y + pltpu.roll(xf, GDN_CONV - 1 - j, axis=0) * cw_ref[j:j + 1, w0:w0 + width]
        y = y[8:, :]
        return y * _sigmoid(y)

    def l2n(x):
        parts = []
        for h in range(GDN_HEADS):
            xh = x[:, h * GDN_DK:(h + 1) * GDN_DK]
            parts.append(xh * lax.rsqrt(jnp.sum(xh * xh, axis=-1, keepdims=True) + NORM_EPS))
        return jnp.concatenate(parts, axis=1)

    qs[...] = l2n(conv_silu(q_ref, qh_ref, 0)) * (GDN_DK ** -0.5)
    ks[...] = l2n(conv_silu(k_ref, kh_ref, width))
    vs[...] = conv_silu(v_ref, vh_ref, 2 * width)

    incl = _tri(C, "incl")
    strict = _tri(C, "strict")
    tril_b = incl.astype(BF16)
    triu_b = _tri(C, "upper_incl").astype(BF16)
    inv_masks = _inverse_masks(C)

    def prep(cg, carry):
        loaded = []
        for g in range(group):
            c = cg * group + g
            r0 = pl.multiple_of(c * C, C)
            gcol = gc_ref[pl.ds(r0, C), :]
            grow = gr_ref[c]
            qkv = [(qs[pl.ds(r0, C), h * GDN_DK:(h + 1) * GDN_DK],
                    ks[pl.ds(r0, C), h * GDN_DK:(h + 1) * GDN_DK],
                    vs[pl.ds(r0, C), h * GDN_DK:(h + 1) * GDN_DK]) for h in range(GDN_HEADS)]
            loaded.append((c, gcol, grow, qkv))
        ch = []
        for c, gcol, grow, qkv in loaded:
            g_col = -jnp.exp(pc_ref[0:1, :]) * _softplus(gcol + pc_ref[1:2, :])
            gcum_col = _cumsum_rows(tril_b, g_col)
            beta_col = _sigmoid(gcol)
            g_row = -jnp.exp(pr_ref[:, 0:1]) * _softplus(grow + pr_ref[:, 1:2])
            gcum_row = _cumsum_lanes(g_row, triu_b)
            for h in range(GDN_HEADS):
                q, k, v = qkv[h]
                gc_c = gcum_col[:, h:h + 1]
                gc_r = gcum_row[h:h + 1, :]
                beta = beta_col[:, 4 + h:5 + h]
                decay = jnp.where(incl, jnp.exp(jnp.where(incl, gc_c - gc_r, 0.0)), 0.0)
                ch.append(dict(idx=c * GDN_HEADS + h, q=q, k=k, v=v, gc_c=gc_c, beta=beta,
                               decay=decay, kb=k * beta))
        for d in ch:
            d["a"] = jnp.where(strict, _mm_nt(d["kb"], d["k"]) * d["decay"], 0.0)
        eye, offs = inv_masks
        for d in ch:
            d["t"] = eye - jnp.where(offs[0], d["a"], 0.0)
        for off in offs[1:]:
            for d in ch:
                d["p"] = _mm(d["t"], jnp.where(off, d["a"], 0.0))
            for d in ch:
                d["t"] = d["t"] - _mm(d["p"], d["t"])
        for d in ch:
            d["egc"] = jnp.exp(d["gc_c"])
            d["u"] = _mm(d["t"], d["v"] * d["beta"])
        for d in ch:
            d["w"] = _mm(d["t"], d["kb"] * d["egc"]).astype(BF16)
        for d in ch:
            d["qk"] = (_mm_nt(d["q"], d["k"]) * d["decay"]).astype(BF16)
        for d in ch:
            idx = d["idx"]
            g_last = d["gc_c"][C - 1:C, :]
            u_s[idx] = d["u"]
            w_s[idx] = d["w"]
            qk_s[idx] = d["qk"]
            qd_s[idx] = (d["q"] * d["egc"]).astype(BF16)
            kdt_s[idx] = (d["k"] * jnp.exp(g_last - d["gc_c"])).T.astype(BF16)
            el_s[idx] = jnp.broadcast_to(jnp.exp(g_last), (8, LANES))
        return carry

    lax.fori_loop(0, n_chunks // group, prep, 0)

    def scan(c, carry):
        r0 = pl.multiple_of(c * C, C)
        heads = range(GDN_HEADS)
        dot = functools.partial(jnp.dot, preferred_element_type=F32)
        idx = [c * GDN_HEADS + h for h in heads]
        s_prev = [s_s[h] for h in heads]
        s_b = [s.astype(BF16) for s in s_prev]
        ws = [dot(w_s[idx[h]], s_b[h]) for h in heads]
        qs_ = [dot(qd_s[idx[h]], s_b[h]) for h in heads]
        v_b = [(u_s[idx[h]] - ws[h]).astype(BF16) for h in heads]
        upd = [dot(kdt_s[idx[h]], v_b[h]) for h in heads]
        o_all = [qs_[h] + dot(qk_s[idx[h]], v_b[h]) for h in heads]
        outs = []
        for h in heads:
            s_s[h] = s_prev[h] * el_s[idx[h]][0:1, :] + upd[h]
            o = o_all[h]
            ms = jnp.mean(o * o, axis=-1, keepdims=True)
            outs.append(o * lax.rsqrt(ms + NORM_EPS))
        y = jnp.concatenate(outs, axis=1) * og_ref[...]
        z = z_ref[pl.ds(r0, C), :]
        y = y * (z * _sigmoid(z))
        y_ref[pl.ds(r0, C), :] = y.astype(y_ref.dtype)
        return carry

    lax.fori_loop(0, n_chunks, scan, 0)


def _gdn(proj3, gates_r, conv_w, par_c, par_r, og, *, chunk, tb):
    b, t, _ = proj3.shape
    nc = tb // chunk
    nch = nc * GDN_HEADS
    width = GDN_HEADS * GDN_DK
    group = 2 if nc % 2 == 0 else 1
    kern = functools.partial(_gdn_kernel, chunk=chunk, n_chunks=nc, tb=tb, group=group)
    col = lambda c0: pl.BlockSpec((None, tb, width), lambda i, j, c0=c0: (i, j, c0 // width))
    halo = lambda c0: pl.BlockSpec(
        (None, 8, width), lambda i, j, c0=c0: (i, jnp.maximum(j * (tb // 8) - 1, 0), c0 // width))
    return pl.pallas_call(
        kern,
        grid=(b, t // tb),
        in_specs=[
            col(C_GQ), col(C_GK), col(C_GV), col(C_GZ),
            halo(C_GQ), halo(C_GK), halo(C_GV),
            pl.BlockSpec((None, tb, LANES), lambda i, j: (i, j, C_GATE // LANES)),
            pl.BlockSpec((None, nc, 16, chunk), lambda i, j: (i, j, 0, 0)),
            pl.BlockSpec((GDN_CONV, 3 * width), lambda i, j: (0, 0)),
            pl.BlockSpec((2, LANES), lambda i, j: (0, 0)),
            pl.BlockSpec((16, 2), lambda i, j: (0, 0)),
            pl.BlockSpec((1, width), lambda i, j: (0, 0)),
        ],
        out_specs=pl.BlockSpec((None, tb, width), lambda i, j: (i, j, 0)),
        out_shape=jax.ShapeDtypeStruct((b, t, width), BF16),
        scratch_shapes=[pltpu.VMEM((tb, width), F32), pltpu.VMEM((tb, width), F32),
                        pltpu.VMEM((tb, width), F32),
                        pltpu.VMEM((nch, chunk, GDN_DV), F32),
                        pltpu.VMEM((nch, chunk, GDN_DK), BF16),
                        pltpu.VMEM((nch, chunk, chunk), BF16),
                        pltpu.VMEM((nch, chunk, GDN_DK), BF16),
                        pltpu.VMEM((nch, GDN_DK, chunk), BF16),
                        pltpu.VMEM((nch, 8, LANES), F32),
                        pltpu.VMEM((GDN_HEADS, GDN_DK, GDN_DV), F32)],
        compiler_params=pltpu.CompilerParams(
            dimension_semantics=("arbitrary", "arbitrary"), vmem_limit_bytes=VMEM_LIMIT),
        name="gdn",
    )(proj3, proj3, proj3, proj3, proj3, proj3, proj3, proj3, gates_r, conv_w, par_c, par_r, og)


def _dsa_prep_kernel(cq_ref, ckv_ref, idx_ref, qn_ref, kn_ref, wuq_ref, wqi_ref, wuk_ref,
                     qlat_ref, qidx_ref, ckvn_ref, ckvt_ref, kidx_ref):
    cq = cq_ref[...]
    ms = jnp.mean(cq * cq, axis=-1, keepdims=True)
    cqn = ((cq * lax.rsqrt(ms + NORM_EPS)) * qn_ref[...]).astype(BF16)
    ckv = ckv_ref[...]
    ms = jnp.mean(ckv * ckv, axis=-1, keepdims=True)
    ckvn = (ckv * lax.rsqrt(ms + NORM_EPS)) * kn_ref[...]
    ckvn_ref[...] = ckvn.astype(BF16)
    ckvt_ref[...] = ckvn.T.astype(BF16)
    q = jnp.dot(cqn, wuq_ref[...], preferred_element_type=F32)
    qlat = jnp.dot(q.astype(BF16), wuk_ref[...], preferred_element_type=F32) * (DSA_HEAD_DIM ** -0.5)
    qlat_ref[...] = qlat.astype(BF16)
    qidx_ref[...] = jnp.dot(cqn, wqi_ref[...], preferred_element_type=F32).astype(BF16)
    kidx_ref[...] = idx_ref[:, 0:IDX_DIM].astype(BF16)


def _dsa_prep(proj2, qn, kn, wuq, wqi, wuk_bd):
    m = proj2.shape[0]
    tm = 512
    full = lambda shape: pl.BlockSpec(shape, lambda i: (0, 0))
    return pl.pallas_call(
        _dsa_prep_kernel,
        grid=(m // tm,),
        in_specs=[
            pl.BlockSpec((tm, DSA_Q_RANK), lambda i: (i, C_CQ // DSA_Q_RANK)),
            pl.BlockSpec((tm, DSA_KV_RANK), lambda i: (i, C_CKV // DSA_KV_RANK)),
            pl.BlockSpec((tm, LANES), lambda i: (i, C_IDX // LANES)),
            full((1, DSA_Q_RANK)), full((1, DSA_KV_RANK)),
            full((DSA_Q_RANK, DSA_HEADS * DSA_HEAD_DIM)),
            full((DSA_Q_RANK, IDX_HEADS * IDX_DIM)),
            full((DSA_HEADS * DSA_HEAD_DIM, DSA_HEADS * DSA_KV_RANK)),
        ],
        out_specs=[
            pl.BlockSpec((tm, DSA_HEADS * DSA_KV_RANK), lambda i: (i, 0)),
            pl.BlockSpec((tm, IDX_HEADS * IDX_DIM), lambda i: (i, 0)),
            pl.BlockSpec((tm, DSA_KV_RANK), lambda i: (i, 0)),
            pl.BlockSpec((DSA_KV_RANK, tm), lambda i: (0, i)),
            pl.BlockSpec((tm, IDX_DIM), lambda i: (i, 0)),
        ],
        out_shape=[
            jax.ShapeDtypeStruct((m, DSA_HEADS * DSA_KV_RANK), BF16),
            jax.ShapeDtypeStruct((m, IDX_HEADS * IDX_DIM), BF16),
            jax.ShapeDtypeStruct((m, DSA_KV_RANK), BF16),
            jax.ShapeDtypeStruct((DSA_KV_RANK, m), BF16),
            jax.ShapeDtypeStruct((m, IDX_DIM), BF16),
        ],
        compiler_params=pltpu.CompilerParams(
            dimension_semantics=("arbitrary",), vmem_limit_bytes=VMEM_LIMIT),
        name="dsa_prep",
    )(proj2, proj2, proj2, qn, kn, wuq, wqi, wuk_bd)


def _tree_sum(parts):
    while len(parts) > 1:
        parts = [parts[j] + parts[j + 1] for j in range(0, len(parts) - 1, 2)] + (
            [parts[-1]] if len(parts) % 2 else [])
    return parts[0]


def _dsa_kernel(qidx_ref, wrow_ref, qlat_ref, kidx_ref, ckv_ref, ckvt_ref, wuvt_ref, y_ref,
                key_s, hi_s, lo_s, acc_s, *, qb, tk, n_sel):
    i = pl.program_id(1)
    n_tiles = (i * qb + qb + tk - 1) // tk
    key_pos = lax.broadcasted_iota(jnp.int32, (tk, qb), 0)
    q_pos = i * qb + lax.broadcasted_iota(jnp.int32, (tk, qb), 1)
    w_rows = wrow_ref[...] * (IDX_HEADS ** -0.5 * IDX_DIM ** -0.5)
    qidx = qidx_ref[...]
    q_heads = [qidx[:, h * IDX_DIM:(h + 1) * IDX_DIM] for h in range(IDX_HEADS)]
    i16_min = -(1 << 15)
    i16_max = (1 << 15) - 1
    sub16 = 16
    nt_dims = (((1,), (1,)), ((), ()))

    def score_body(kt, carry):
        k0 = pl.multiple_of(kt * tk, tk)
        kk = kidx_ref[pl.ds(k0, tk), :]
        sh = [lax.dot_general(kk, q_heads[h], nt_dims, preferred_element_type=F32)
              for h in range(IDX_HEADS)]
        acc = jnp.zeros((tk, qb), F32)
        for h in range(IDX_HEADS):
            acc = acc + jnp.maximum(sh[h], 0.0) * w_rows[h:h + 1, :]
        acc = acc + 0.0
        bits = pltpu.bitcast(acc, jnp.int32)
        keys = jnp.where(bits < 0, bits ^ jnp.int32(0x7FFFFFFF), bits)
        keys = jnp.where(k0 + key_pos <= q_pos, keys, jnp.int32(INT_MIN))
        key_s[kt] = keys
        hi_s[kt] = (keys >> 16).astype(jnp.int16)
        return carry

    lax.fori_loop(0, n_tiles, score_body, 0)

    def count_ge(src_s, cand):
        cand16 = jnp.broadcast_to(cand, (sub16, qb)).astype(jnp.int16)
        one, zero = jnp.int16(1), jnp.int16(0)

        def body(kt, acc):
            tile = src_s[kt]
            return acc + _tree_sum([jnp.where(tile[j * sub16:(j + 1) * sub16, :] >= cand16, one, zero)
                                    for j in range(tk // sub16)])

        cnt = lax.fori_loop(0, n_tiles, body, jnp.zeros((sub16, qb), jnp.int16))
        return jnp.sum(cnt.astype(jnp.int32), axis=0, keepdims=True)

    def kth_largest16(src_s, rank):
        thr = jnp.full((1, qb), i16_min, jnp.int32)
        for bit in range(15, -1, -1):
            cand = thr + jnp.int32(1 << bit)
            thr = jnp.where(count_ge(src_s, cand) >= rank, cand, thr)
        return thr

    hi_thr = kth_largest16(hi_s, jnp.int32(n_sel))
    above = count_ge(hi_s, jnp.minimum(hi_thr + 1, jnp.int32(i16_max)))
    rank_lo = jnp.int32(n_sel) - jnp.where(hi_thr < i16_max, above, 0)

    def lo_body(kt, carry):
        keys = key_s[kt]
        lo = (keys & jnp.int32(0xFFFF)) + jnp.int32(i16_min)
        lo_s[kt] = jnp.where((keys >> 16) == hi_thr, lo, jnp.int32(i16_min)).astype(jnp.int16)
        return carry

    lax.fori_loop(0, n_tiles, lo_body, 0)
    lo_thr = kth_largest16(lo_s, rank_lo)
    thr = hi_thr * jnp.int32(1 << 16) + (lo_thr - jnp.int32(i16_min))
    thr = jnp.maximum(thr, jnp.int32(INT_MIN + 1))

    qlat = qlat_ref[...]
    heads = range(DSA_HEADS)
    ql = [qlat[:, h * DSA_KV_RANK:(h + 1) * DSA_KV_RANK] for h in heads]
    acc_s[...] = jnp.zeros_like(acc_s)

    def attn_body(kt, carry):
        m_run, l_run = carry
        k0 = pl.multiple_of(kt * tk, tk)
        ckv = ckv_ref[pl.ds(k0, tk), :]
        ckvt = ckvt_ref[:, pl.ds(k0, tk)]
        bias = jnp.where(key_s[kt] >= thr, 0.0, -jnp.inf)
        lg = [lax.dot_general(ckv, ql[h], nt_dims, preferred_element_type=F32) for h in heads]
        m_out, l_out, p_all, alphas = [], [], [], []
        for h in heads:
            lgm = lg[h] + bias
            m_new = jnp.maximum(m_run[h], jnp.max(lgm, axis=0, keepdims=True))
            p = jnp.exp(lgm - m_new)
            alpha = jnp.exp(m_run[h] - m_new)
            l_out.append(alpha * l_run[h] + jnp.sum(p, axis=0, keepdims=True))
            m_out.append(m_new)
            p_all.append(p.astype(BF16))
            alphas.append(alpha)
        pv = [jnp.dot(ckvt, p_all[h], preferred_element_type=F32) for h in heads]
        for h in heads:
            acc_s[h] = alphas[h] * acc_s[h] + pv[h]
        return tuple(m_out), tuple(l_out)

    init = (tuple(jnp.full((1, qb), -1e30, F32) for _ in heads),
            tuple(jnp.zeros((1, qb), F32) for _ in heads))
    _, l_fin = lax.fori_loop(0, n_tiles, attn_body, init)
    o_lat = [(acc_s[h] / l_fin[h]).astype(BF16) for h in heads]
    outs = [jnp.dot(wuvt_ref[h], o_lat[h], preferred_element_type=F32) for h in heads]
    y_ref[...] = jnp.concatenate(outs, axis=0).T.astype(y_ref.dtype)


def _dsa(qidx, widx_rows, qlat, kidx, ckvn, ckvt, wuvt, *, qb, tk):
    b, t, _ = qidx.shape
    n_sel = min(IDX_TOPK, t // 4)
    kern = functools.partial(_dsa_kernel, qb=qb, tk=tk, n_sel=n_sel)
    return pl.pallas_call(
        kern,
        grid=(b, t // qb),
        in_specs=[
            pl.BlockSpec((None, qb, IDX_HEADS * IDX_DIM), lambda i, j: (i, j, 0)),
            pl.BlockSpec((None, IDX_HEADS, qb), lambda i, j: (i, 0, j)),
            pl.BlockSpec((None, qb, DSA_HEADS * DSA_KV_RANK), lambda i, j: (i, j, 0)),
            pl.BlockSpec((None, t, IDX_DIM), lambda i, j: (i, 0, 0)),
            pl.BlockSpec((None, t, DSA_KV_RANK), lambda i, j: (i, 0, 0)),
            pl.BlockSpec((DSA_KV_RANK, t), lambda i, j: (0, i)),
            pl.BlockSpec((DSA_HEADS, DSA_HEAD_DIM, DSA_KV_RANK), lambda i, j: (0, 0, 0)),
        ],
        out_specs=pl.BlockSpec((None, qb, DSA_HEADS * DSA_HEAD_DIM), lambda i, j: (i, j, 0)),
        out_shape=jax.ShapeDtypeStruct((b, t, DSA_HEADS * DSA_HEAD_DIM), BF16),
        scratch_shapes=[pltpu.VMEM((t // tk, tk, qb), jnp.int32),
                        pltpu.VMEM((t // tk, tk, qb), jnp.int16),
                        pltpu.VMEM((t // tk, tk, qb), jnp.int16),
                        pltpu.VMEM((DSA_HEADS, DSA_KV_RANK, qb), F32)],
        compiler_params=pltpu.CompilerParams(
            dimension_semantics=("arbitrary", "arbitrary"), vmem_limit_bytes=VMEM_LIMIT),
        name="dsa",
    )(qidx, widx_rows, qlat, kidx, ckvn, ckvt, wuvt)


def _pack_w_in(w):
    pts = [0]
    for wd in IN_WIDTHS:
        pts.append(pts[-1] + wd)
    seg = [w[:, pts[i]:pts[i + 1]] for i in range(len(IN_WIDTHS))]
    (gq, gk, gv, gz, ga, gb, cq, ckv, ik, iw, mq, mk, mv, mo, mi, mf) = seg
    zeros = lambda n: jnp.zeros((w.shape[0], n), w.dtype)
    idx_blk = jnp.concatenate([ik, iw, zeros(LANES - IDX_DIM - IDX_HEADS)], axis=1)
    gate_blk = jnp.concatenate([ga, gb, mi, mf, zeros(LANES - 16)], axis=1)
    out = jnp.concatenate([gq, gk, gv, gz, cq, ckv, idx_blk, mq, mk, mv, mo, gate_blk], axis=1)
    assert out.shape[1] == NP
    return out.astype(BF16)


def _lane_row(vals_at):
    row = jnp.zeros((LANES,), F32)
    for start, v in vals_at:
        row = row.at[start:start + v.shape[0]].set(v.astype(F32))
    return row


def kernel(x, attn_norm, w_in, gdn_conv, gdn_a_log, gdn_dt_bias, gdn_out_norm, dsa_q_norm,
           dsa_kv_norm, dsa_w_uq, dsa_w_qidx, dsa_w_uk, dsa_w_uv, mlstm_i_bias, mlstm_f_bias,
           mlstm_out_norm, w_out, ffn_norm, w_gate, w_up, w_down, final_norm):
    b, t, d = x.shape
    depth = w_in.shape[0]
    gdn_chunk, mlstm_chunk = 64, 64
    tb = min(512, t)
    qb = min(256, t)
    tk = min(512, t)
    x2 = x.reshape(b * t, d)
    for l in range(depth):
        proj2 = _inproj(x2, attn_norm[l].reshape(1, d), _pack_w_in(w_in[l]))
        proj3 = proj2.reshape(b, t, NP)
        gates_t = jnp.swapaxes(proj3[:, :, C_GATE:C_GATE + 16], 1, 2)

        def chunk_rows(c):
            return jnp.swapaxes(gates_t.reshape(b, 16, t // c, c), 1, 2)

        gpar_c = jnp.stack([_lane_row([(0, gdn_a_log[l])]), _lane_row([(0, gdn_dt_bias[l])])])
        gpar_r = gpar_c[:, :16].T
        y_a = _gdn(proj3, chunk_rows(gdn_chunk), gdn_conv[l], gpar_c, gpar_r,
                   jnp.tile(gdn_out_norm[l], GDN_HEADS).reshape(1, -1), chunk=gdn_chunk, tb=tb)

        wuk_bd = jnp.zeros((DSA_HEADS * DSA_HEAD_DIM, DSA_HEADS * DSA_KV_RANK), F32)
        for h in range(DSA_HEADS):
            wuk_bd = wuk_bd.at[h * DSA_HEAD_DIM:(h + 1) * DSA_HEAD_DIM,
                               h * DSA_KV_RANK:(h + 1) * DSA_KV_RANK].set(dsa_w_uk[l, h].T)
        qlat, qidx, ckvn, ckvt, kidx = _dsa_prep(
            proj2, dsa_q_norm[l].reshape(1, -1), dsa_kv_norm[l].reshape(1, -1),
            dsa_w_uq[l].astype(BF16), dsa_w_qidx[l].astype(BF16), wuk_bd.astype(BF16))
        r3 = lambda a: a.reshape(b, t, a.shape[-1])
        widx_rows = jnp.swapaxes(proj3[:, :, C_IDX + IDX_DIM:C_IDX + IDX_DIM + IDX_HEADS], 1, 2)
        y_b = _dsa(r3(qidx), widx_rows, r3(qlat), r3(kidx), r3(ckvn), ckvt,
                   jnp.swapaxes(dsa_w_uv[l], 1, 2).astype(BF16), qb=qb, tk=tk)

        mb_c = _lane_row([(8, mlstm_i_bias[l]), (12, mlstm_f_bias[l])]).reshape(1, LANES)
        mb_r = mb_c[0, :16].reshape(16, 1)
        y_c = _mlstm(proj3, chunk_rows(mlstm_chunk), mb_c, mb_r,
                     jnp.tile(mlstm_out_norm[l], MLSTM_HEADS).reshape(1, -1),
                     chunk=mlstm_chunk, tb=tb)

        x2 = _out_ffn(x2, y_a.reshape(b * t, -1), y_b.reshape(b * t, -1), y_c.reshape(b * t, -1),
                      w_out[l].astype(BF16), ffn_norm[l].reshape(1, d), w_gate[l].astype(BF16),
                      w_up[l].astype(BF16), w_down[l].astype(BF16), final_norm.reshape(1, d),
                      final_norm=(l == depth - 1))
    return x2.reshape(b, t, d)
```

```python
import functools
import math

import jax
import jax.numpy as jnp
from jax import lax
from jax.experimental import pallas as pl
from jax.experimental.pallas import tpu as pltpu

F32 = jnp.float32
BF16 = jnp.bfloat16
HIGHEST = lax.Precision.HIGHEST

D_MODEL = 1024
GDN_HEADS = 4
GDN_DK = 128
GDN_DV = 128
GDN_CONV = 4
DSA_HEADS = 4
DSA_HEAD_DIM = 64
DSA_Q_RANK = 256
DSA_KV_RANK = 128
IDX_HEADS = 8
IDX_DIM = 32
IDX_TOPK = 256
MLSTM_HEADS = 4
MLSTM_DQK = 64
MLSTM_DV = 64
FFN_HIDDEN = 2816
NORM_EPS = 1e-6

IN_WIDTHS = (512, 512, 512, 512, 4, 4, 256, 128, 32, 8, 256, 256, 256, 256, 4, 4)

C_GQ, C_GK, C_GV, C_GZ = 0, 512, 1024, 1536
C_CQ, C_CKV, C_IDX = 2048, 2304, 2432
C_MQ, C_MK, C_MV, C_MO = 2560, 2816, 3072, 3328
C_GATE = 3584
NP = 3712
LANES = 128

VMEM_LIMIT = 56 * 1024 * 1024

INT_MIN = -2 ** 31


def _mm(a, b):
    return jnp.dot(a.astype(BF16), b.astype(BF16), preferred_element_type=F32)


def _mm_nt(a, b):
    return lax.dot_general(a.astype(BF16), b.astype(BF16), (((1,), (1,)), ((), ())),
                           preferred_element_type=F32)


def _sigmoid(x):
    return 1.0 / (1.0 + jnp.exp(-x))


def _softplus(x):
    return jnp.maximum(x, 0.0) + jnp.log1p(jnp.exp(-jnp.abs(x)))


def _tri(n, kind):
    r = lax.broadcasted_iota(jnp.int32, (n, n), 0)
    c = lax.broadcasted_iota(jnp.int32, (n, n), 1)
    if kind == "incl":
        return r >= c
    if kind == "strict":
        return r > c
    if kind == "upper_incl":
        return r <= c
    raise ValueError(kind)


def _inproj_kernel(x_ref, g_ref, w_ref, o_ref):
    x = x_ref[...]
    ms = jnp.mean(x * x, axis=-1, keepdims=True)
    h = (x * lax.rsqrt(ms + NORM_EPS)) * g_ref[...]
    o_ref[...] = jnp.dot(h.astype(BF16), w_ref[...], preferred_element_type=F32)


def _inproj(x2, g, w):
    m = x2.shape[0]
    tm = 256
    return pl.pallas_call(
        _inproj_kernel,
        grid=(m // tm,),
        in_specs=[
            pl.BlockSpec((tm, D_MODEL), lambda i: (i, 0)),
            pl.BlockSpec((1, D_MODEL), lambda i: (0, 0)),
            pl.BlockSpec((D_MODEL, NP), lambda i: (0, 0)),
        ],
        out_specs=pl.BlockSpec((tm, NP), lambda i: (i, 0)),
        out_shape=jax.ShapeDtypeStruct((m, NP), F32),
        compiler_params=pltpu.CompilerParams(
            dimension_semantics=("arbitrary",), vmem_limit_bytes=VMEM_LIMIT),
        name="inproj",
    )(x2, g, w)


def _ffn_kernel(x_ref, ya_ref, yb_ref, yc_ref, wo_ref, g_ref, wg_ref, wu_ref, wd_ref, fg_ref,
                o_ref, acc_ref, h_ref, *, n_hidden_steps, final_norm):
    j = pl.program_id(1)

    @pl.when(j == 0)
    def _():
        xn = x_ref[...]
        xn = xn + jnp.dot(ya_ref[...], wo_ref[0:512, :], preferred_element_type=F32)
        xn = xn + jnp.dot(yb_ref[...], wo_ref[512:768, :], preferred_element_type=F32)
        xn = xn + jnp.dot(yc_ref[...], wo_ref[768:1024, :], preferred_element_type=F32)
        acc_ref[...] = xn
        ms = jnp.mean(xn * xn, axis=-1, keepdims=True)
        h_ref[...] = ((xn * lax.rsqrt(ms + NORM_EPS)) * g_ref[...]).astype(BF16)

    h = h_ref[...]
    gate = jnp.dot(h, wg_ref[...], preferred_element_type=F32)
    up = jnp.dot(h, wu_ref[...], preferred_element_type=F32)
    act = (gate * _sigmoid(gate)) * up
    acc_ref[...] += jnp.dot(act.astype(BF16), wd_ref[...], preferred_element_type=F32)

    @pl.when(j == n_hidden_steps - 1)
    def _():
        y = acc_ref[...]
        if final_norm:
            ms = jnp.mean(y * y, axis=-1, keepdims=True)
            y = (y * lax.rsqrt(ms + NORM_EPS)) * fg_ref[...]
        o_ref[...] = y


def _out_ffn(x2, ya, yb, yc, wo, g, wg, wu, wd, fg, final_norm):
    m = x2.shape[0]
    tm = 512
    th = 1408
    nh = FFN_HIDDEN // th
    kern = functools.partial(_ffn_kernel, n_hidden_steps=nh, final_norm=final_norm)
    return pl.pallas_call(
        kern,
        grid=(m // tm, nh),
        in_specs=[
            pl.BlockSpec((tm, D_MODEL), lambda i, j: (i, 0)),
            pl.BlockSpec((tm, 512), lambda i, j: (i, 0)),
            pl.BlockSpec((tm, 256), lambda i, j: (i, 0)),
            pl.BlockSpec((tm, 256), lambda i, j: (i, 0)),
            pl.BlockSpec((D_MODEL, D_MODEL), lambda i, j: (0, 0)),
            pl.BlockSpec((1, D_MODEL), lambda i, j: (0, 0)),
            pl.BlockSpec((D_MODEL, th), lambda i, j: (0, j)),
            pl.BlockSpec((D_MODEL, th), lambda i, j: (0, j)),
            pl.BlockSpec((th, D_MODEL), lambda i, j: (j, 0)),
            pl.BlockSpec((1, D_MODEL), lambda i, j: (0, 0)),
        ],
        out_specs=pl.BlockSpec((tm, D_MODEL), lambda i, j: (i, 0)),
        out_shape=jax.ShapeDtypeStruct((m, D_MODEL), F32),
        scratch_shapes=[pltpu.VMEM((tm, D_MODEL), F32), pltpu.VMEM((tm, D_MODEL), BF16)],
        compiler_params=pltpu.CompilerParams(
            dimension_semantics=("arbitrary", "arbitrary"), vmem_limit_bytes=VMEM_LIMIT),
        name="out_ffn",
    )(x2, ya, yb, yc, wo, g, wg, wu, wd, fg)


def _mlstm_kernel(q_ref, k_ref, v_ref, o_ref, gc_ref, gr_ref, bc_ref, br_ref, og_ref,
                  y_ref, sv_s, kv_s, bcol_s, dm_s, km_s, c_s, m_s, *, chunk, n_chunks, group):
    L = chunk
    t = pl.program_id(1)
    heads = range(MLSTM_HEADS)

    @pl.when(t == 0)
    def _():
        c_s[...] = jnp.zeros_like(c_s)
        m_s[...] = jnp.zeros_like(m_s)

    incl = _tri(L, "incl")
    tril_b = incl.astype(BF16)
    triu_b = _tri(L, "upper_incl").astype(BF16)
    e0 = (lax.broadcasted_iota(jnp.int32, (L, MLSTM_DV), 1) == 0).astype(F32)
    lane_l = lax.broadcasted_iota(jnp.int32, (L, LANES), 1)
    lane_8 = lax.broadcasted_iota(jnp.int32, (8, LANES), 1)
    scale = MLSTM_DQK ** -0.5

    def prep(cg, carry):
        ch = []
        gate_tiles = []
        for g in range(group):
            c = cg * group + g
            r0 = pl.multiple_of(c * L, L)
            x_col = gc_ref[pl.ds(r0, L), :] + bc_ref[...]
            x_row = gr_ref[c] + br_ref[...]
            b_col = _cumsum_rows(tril_b, -_softplus(-x_col))
            b_row = _cumsum_lanes(-_softplus(-x_row), triu_b)
            dm_tile = jnp.zeros((L, LANES), F32)
            km_tile = jnp.zeros((8, LANES), F32)
            for h in heads:
                sl = slice(h * MLSTM_DQK, (h + 1) * MLSTM_DQK)
                bx_c = b_col[:, 12 + h:13 + h]
                d_log = jnp.where(incl, bx_c - b_row[12 + h:13 + h, :] + x_row[8 + h:9 + h, :], -jnp.inf)
                dmax = jnp.max(d_log, axis=-1, keepdims=True)
                kmax = dmax[L - 1:L, :]
                kw = jnp.exp(bx_c[L - 1:L, :] - bx_c + x_col[:, 8 + h:9 + h] - kmax)
                k = k_ref[pl.ds(r0, L), sl] * scale
                v_aug = jnp.concatenate([v_ref[pl.ds(r0, L), sl], e0], axis=1)
                ch.append(dict(idx=c * MLSTM_HEADS + h, q=q_ref[pl.ds(r0, L), sl], k=k, v=v_aug,
                               dw=jnp.exp(d_log - dmax), kxw_t=(k * kw).T))
                dm_tile = jnp.where(lane_l == h, dmax, dm_tile)
                km_tile = jnp.where(lane_8 == h, kmax, km_tile)
            gate_tiles.append((c, b_col, dm_tile, km_tile))
        for d in ch:
            d["s"] = _mm_nt(d["q"], d["k"]) * d["dw"]
        for d in ch:
            d["sv"] = _mm(d["s"], d["v"])
        for d in ch:
            d["kv"] = _mm(d["kxw_t"], d["v"])
        for d in ch:
            sv_s[d["idx"]] = d["sv"]
            kv_s[d["idx"]] = d["kv"]
        for c, b_col, dm_tile, km_tile in gate_tiles:
            bcol_s[c] = b_col
            dm_s[c] = dm_tile
            km_s[c] = km_tile
        return carry

    lax.fori_loop(0, n_chunks // group, prep, 0)

    def scan(c, carry):
        r0 = pl.multiple_of(c * L, L)
        b_col = bcol_s[c]
        dm = dm_s[c]
        km = km_s[c]
        c_prev = [c_s[h] for h in heads]
        qc = [_mm(q_ref[pl.ds(r0, L), h * MLSTM_DQK:(h + 1) * MLSTM_DQK], c_prev[h]) for h in heads]
        outs = []
        for h in heads:
            idx = c * MLSTM_HEADS + h
            bx_c = b_col[:, 12 + h:13 + h]
            dmax = dm[:, h:h + 1]
            kmax = km[0:1, h:h + 1]
            m_prev = m_s[h:h + 1, 0:1]
            a_log = bx_c + m_prev
            m_t = jnp.maximum(a_log, dmax)
            num = jnp.exp(a_log - m_t) * qc[h] + jnp.exp(dmax - m_t) * sv_s[idx]
            den = num[:, MLSTM_DV:MLSTM_DV + 1]
            hh = num[:, :MLSTM_DV] / jnp.maximum(jnp.abs(den), jnp.exp(-m_t))
            m_new = m_t[L - 1:L, :]
            dec = jnp.exp(bx_c[L - 1:L, :] + m_prev - m_new)
            c_s[h] = dec * c_prev[h] + jnp.exp(kmax - m_new) * kv_s[idx]
            m_s[h:h + 1, :] = jnp.broadcast_to(m_new, (1, LANES))
            ms = jnp.mean(hh * hh, axis=-1, keepdims=True)
            outs.append(hh * lax.rsqrt(ms + NORM_EPS))
        y = jnp.concatenate(outs, axis=1) * og_ref[...]
        y = y * _sigmoid(o_ref[pl.ds(r0, L), :])
        y_ref[pl.ds(r0, L), :] = y.astype(y_ref.dtype)
        return carry

    lax.fori_loop(0, n_chunks, scan, 0)


def _mlstm(proj3, gates_r, bias_c, bias_r, og, *, chunk, tb):
    b, t, _ = proj3.shape
    nc = tb // chunk
    nch = nc * MLSTM_HEADS
    wq = MLSTM_HEADS * MLSTM_DQK
    group = 2 if nc % 2 == 0 else 1
    kern = functools.partial(_mlstm_kernel, chunk=chunk, n_chunks=nc, group=group)
    col = lambda c0: pl.BlockSpec((None, tb, wq), lambda i, j, c0=c0: (i, j, c0 // wq))
    return pl.pallas_call(
        kern,
        grid=(b, t // tb),
        in_specs=[
            col(C_MQ), col(C_MK), col(C_MV), col(C_MO),
            pl.BlockSpec((None, tb, LANES), lambda i, j: (i, j, C_GATE // LANES)),
            pl.BlockSpec((None, nc, 16, chunk), lambda i, j: (i, j, 0, 0)),
            pl.BlockSpec((1, LANES), lambda i, j: (0, 0)),
            pl.BlockSpec((16, 1), lambda i, j: (0, 0)),
            pl.BlockSpec((1, wq), lambda i, j: (0, 0)),
        ],
        out_specs=pl.BlockSpec((None, tb, wq), lambda i, j: (i, j, 0)),
        out_shape=jax.ShapeDtypeStruct((b, t, wq), BF16),
        scratch_shapes=[pltpu.VMEM((nch, chunk, LANES), F32),
                        pltpu.VMEM((nch, MLSTM_DQK, LANES), F32),
                        pltpu.VMEM((nc, chunk, LANES), F32),
                        pltpu.VMEM((nc, chunk, LANES), F32),
                        pltpu.VMEM((nc, 8, LANES), F32),
                        pltpu.VMEM((MLSTM_HEADS, MLSTM_DQK, LANES), F32),
                        pltpu.VMEM((8, LANES), F32)],
        compiler_params=pltpu.CompilerParams(
            dimension_semantics=("arbitrary", "arbitrary"), vmem_limit_bytes=VMEM_LIMIT),
        name="mlstm",
    )(proj3, proj3, proj3, proj3, proj3, gates_r, bias_c, bias_r, og)


def _split3(x):
    x1 = x.astype(BF16)
    r1 = x - x1.astype(F32)
    x2 = r1.astype(BF16)
    x3 = (r1 - x2.astype(F32)).astype(BF16)
    return x1, x2, x3


def _cumsum_rows(tril_b, x):
    return sum(jnp.dot(tril_b, p, preferred_element_type=F32) for p in _split3(x))


def _cumsum_lanes(x, triu_b):
    return sum(jnp.dot(p, triu_b, preferred_element_type=F32) for p in _split3(x))


def _inverse_masks(n):
    r = lax.broadcasted_iota(jnp.int32, (n, n), 0)
    c = lax.broadcasted_iota(jnp.int32, (n, n), 1)
    offs = []
    lvl = 0
    while (1 << lvl) < n:
        rb = r >> lvl
        cb = c >> lvl
        offs.append(((rb & 1) == 1) & (cb == rb - 1))
        lvl += 1
    return (r == c).astype(F32), offs


def _gdn_kernel(q_ref, k_ref, v_ref, z_ref, qh_ref, kh_ref, vh_ref, gc_ref, gr_ref, cw_ref,
                pc_ref, pr_ref, og_ref, y_ref, qs, ks, vs, u_s, w_s, qk_s, qd_s, kdt_s, el_s, s_s,
                *, chunk, n_chunks, tb, group):
    C = chunk
    t = pl.program_id(1)
    width = GDN_HEADS * GDN_DK

    @pl.when(t == 0)
    def _():
        s_s[...] = jnp.zeros_like(s_s)

    def conv_silu(x_ref, halo_ref, w0):
        halo = jnp.where(t == 0, 0.0, halo_ref[...])
        xf = jnp.concatenate([halo, x_ref[...]], axis=0)
        y = xf * cw_ref[GDN_CONV - 1:GDN_CONV, w0:w0 + width]
        for j in range(GDN_CONV - 1):
            y = y + pltpu.roll(xf, GDN_CONV - 1 - j, axis=0) * cw_ref[j:j + 1, w0:w0 + width]
        y = y[8:, :]
        return y * _sigmoid(y)

    def l2n(x):
        parts = []
        for h in range(GDN_HEADS):
            xh = x[:, h * GDN_DK:(h + 1) * GDN_DK]
            parts.append(xh * lax.rsqrt(jnp.sum(xh * xh, axis=-1, keepdims=True) + NORM_EPS))
        return jnp.concatenate(parts, axis=1)

    qs[...] = l2n(conv_silu(q_ref, qh_ref, 0)) * (GDN_DK ** -0.5)
    ks[...] = l2n(conv_silu(k_ref, kh_ref, width))
    vs[...] = conv_silu(v_ref, vh_ref, 2 * width)

    incl = _tri(C, "incl")
    strict = _tri(C, "strict")
    tril_b = incl.astype(BF16)
    triu_b = _tri(C, "upper_incl").astype(BF16)
    inv_masks = _inverse_masks(C)

    def prep(cg, carry):
        loaded = []
        for g in range(group):
            c = cg * group + g
            r0 = pl.multiple_of(c * C, C)
            gcol = gc_ref[pl.ds(r0, C), :]
            grow = gr_ref[c]
            qkv = [(qs[pl.ds(r0, C), h * GDN_DK:(h + 1) * GDN_DK],
                    ks[pl.ds(r0, C), h * GDN_DK:(h + 1) * GDN_DK],
                    vs[pl.ds(r0, C), h * GDN_DK:(h + 1) * GDN_DK]) for h in range(GDN_HEADS)]
            loaded.append((c, gcol, grow, qkv))
        ch = []
        for c, gcol, grow, qkv in loaded:
            g_col = -jnp.exp(pc_ref[0:1, :]) * _softplus(gcol + pc_ref[1:2, :])
            gcum_col = _cumsum_rows(tril_b, g_col)
            beta_col = _sigmoid(gcol)
            g_row = -jnp.exp(pr_ref[:, 0:1]) * _softplus(grow + pr_ref[:, 1:2])
            gcum_row = _cumsum_lanes(g_row, triu_b)
            for h in range(GDN_HEADS):
                q, k, v = qkv[h]
                gc_c = gcum_col[:, h:h + 1]
                gc_r = gcum_row[h:h + 1, :]
                beta = beta_col[:, 4 + h:5 + h]
                decay = jnp.where(incl, jnp.exp(jnp.where(incl, gc_c - gc_r, 0.0)), 0.0)
                ch.append(dict(idx=c * GDN_HEADS + h, q=q, k=k, v=v, gc_c=gc_c, beta=beta,
                               decay=decay, kb=k * beta))
        for d in ch:
            d["a"] = jnp.where(strict, _mm_nt(d["kb"], d["k"]) * d["decay"], 0.0)
        eye, offs = inv_masks
        for d in ch:
            d["t"] = eye - jnp.where(offs[0], d["a"], 0.0)
        for off in offs[1:]:
            for d in ch:
                d["p"] = _mm(d["t"], jnp.where(off, d["a"], 0.0))
            for d in ch:
                d["t"] = d["t"] - _mm(d["p"], d["t"])
        for d in ch:
            d["egc"] = jnp.exp(d["gc_c"])
            d["u"] = _mm(d["t"], d["v"] * d["beta"])
        for d in ch:
            d["w"] = _mm(d["t"], d["kb"] * d["egc"]).astype(BF16)
        for d in ch:
            d["qk"] = (_mm_nt(d["q"], d["k"]) * d["decay"]).astype(BF16)
        for d in ch:
            idx = d["idx"]
            g_last = d["gc_c"][C - 1:C, :]
            u_s[idx] = d["u"]
            w_s[idx] = d["w"]
            qk_s[idx] = d["qk"]
            qd_s[idx] = (d["q"] * d["egc"]).astype(BF16)
            kdt_s[idx] = (d["k"] * jnp.exp(g_last - d["gc_c"])).T.astype(BF16)
            el_s[idx] = jnp.broadcast_to(jnp.exp(g_last), (8, LANES))
        return carry

    lax.fori_loop(0, n_chunks // group, prep, 0)

    def scan(c, carry):
        r0 = pl.multiple_of(c * C, C)
        heads = range(GDN_HEADS)
        dot = functools.partial(jnp.dot, preferred_element_type=F32)
        idx = [c * GDN_HEADS + h for h in heads]
        s_prev = [s_s[h] for h in heads]
        s_b = [s.astype(BF16) for s in s_prev]
        ws = [dot(w_s[idx[h]], s_b[h]) for h in heads]
        qs_ = [dot(qd_s[idx[h]], s_b[h]) for h in heads]
        v_b = [(u_s[idx[h]] - ws[h]).astype(BF16) for h in heads]
        upd = [dot(kdt_s[idx[h]], v_b[h]) for h in heads]
        o_all = [qs_[h] + dot(qk_s[idx[h]], v_b[h]) for h in heads]
        outs = []
        for h in heads:
            s_s[h] = s_prev[h] * el_s[idx[h]][0:1, :] + upd[h]
            o = o_all[h]
            ms = jnp.mean(o * o, axis=-1, keepdims=True)
            outs.append(o * lax.rsqrt(ms + NORM_EPS))
        y = jnp.concatenate(outs, axis=1) * og_ref[...]
        z = z_ref[pl.ds(r0, C), :]
        y = y * (z * _sigmoid(z))
        y_ref[pl.ds(r0, C), :] = y.astype(y_ref.dtype)
        return carry

    lax.fori_loop(0, n_chunks, scan, 0)


def _gdn(proj3, gates_r, conv_w, par_c, par_r, og, *, chunk, tb):
    b, t, _ = proj3.shape
    nc = tb // chunk
    nch = nc * GDN_HEADS
    width = GDN_HEADS * GDN_DK
    group = 2 if nc % 2 == 0 else 1
    kern = functools.partial(_gdn_kernel, chunk=chunk, n_chunks=nc, tb=tb, group=group)
    col = lambda c0: pl.BlockSpec((None, tb, width), lambda i, j, c0=c0: (i, j, c0 // width))
    halo = lambda c0: pl.BlockSpec(
        (None, 8, width), lambda i, j, c0=c0: (i, jnp.maximum(j * (tb // 8) - 1, 0), c0 // width))
    return pl.pallas_call(
        kern,
        grid=(b, t // tb),
        in_specs=[
            col(C_GQ), col(C_GK), col(C_GV), col(C_GZ),
            halo(C_GQ), halo(C_GK), halo(C_GV),
            pl.BlockSpec((None, tb, LANES), lambda i, j: (i, j, C_GATE // LANES)),
            pl.BlockSpec((None, nc, 16, chunk), lambda i, j: (i, j, 0, 0)),
            pl.BlockSpec((GDN_CONV, 3 * width), lambda i, j: (0, 0)),
            pl.BlockSpec((2, LANES), lambda i, j: (0, 0)),
            pl.BlockSpec((16, 2), lambda i, j: (0, 0)),
            pl.BlockSpec((1, width), lambda i, j: (0, 0)),
        ],
        out_specs=pl.BlockSpec((None, tb, width), lambda i, j: (i, j, 0)),
        out_shape=jax.ShapeDtypeStruct((b, t, width), BF16),
        scratch_shapes=[pltpu.VMEM((tb, width), F32), pltpu.VMEM((tb, width), F32),
                        pltpu.VMEM((tb, width), F32),
                        pltpu.VMEM((nch, chunk, GDN_DV), F32),
                        pltpu.VMEM((nch, chunk, GDN_DK), BF16),
                        pltpu.VMEM((nch, chunk, chunk), BF16),
                        pltpu.VMEM((nch, chunk, GDN_DK), BF16),
                        pltpu.VMEM((nch, GDN_DK, chunk), BF16),
                        pltpu.VMEM((nch, 8, LANES), F32),
                        pltpu.VMEM((GDN_HEADS, GDN_DK, GDN_DV), F32)],
        compiler_params=pltpu.CompilerParams(
            dimension_semantics=("arbitrary", "arbitrary"), vmem_limit_bytes=VMEM_LIMIT),
        name="gdn",
    )(proj3, proj3, proj3, proj3, proj3, proj3, proj3, proj3, gates_r, conv_w, par_c, par_r, og)


def _dsa_prep_kernel(cq_ref, ckv_ref, idx_ref, qn_ref, kn_ref, wuq_ref, wqi_ref, wuk_ref,
                     qlat_ref, qidx_ref, ckvn_ref, ckvt_ref, kidx_ref):
    cq = cq_ref[...]
    ms = jnp.mean(cq * cq, axis=-1, keepdims=True)
    cqn = ((cq * lax.rsqrt(ms + NORM_EPS)) * qn_ref[...]).astype(BF16)
    ckv = ckv_ref[...]
    ms = jnp.mean(ckv * ckv, axis=-1, keepdims=True)
    ckvn = (ckv * lax.rsqrt(ms + NORM_EPS)) * kn_ref[...]
    ckvn_ref[...] = ckvn.astype(BF16)
    ckvt_ref[...] = ckvn.T.astype(BF16)
    q = jnp.dot(cqn, wuq_ref[...], preferred_element_type=F32)
    qlat = jnp.dot(q.astype(BF16), wuk_ref[...], preferred_element_type=F32) * (DSA_HEAD_DIM ** -0.5)
    qlat_ref[...] = qlat.astype(BF16)
    qidx_ref[...] = jnp.dot(cqn, wqi_ref[...], preferred_element_type=F32).astype(BF16)
    kidx_ref[...] = idx_ref[:, 0:IDX_DIM].astype(BF16)


def _dsa_prep(proj2, qn, kn, wuq, wqi, wuk_bd):
    m = proj2.shape[0]
    tm = 512
    full = lambda shape: pl.BlockSpec(shape, lambda i: (0, 0))
    return pl.pallas_call(
        _dsa_prep_kernel,
        grid=(m // tm,),
        in_specs=[
            pl.BlockSpec((tm, DSA_Q_RANK), lambda i: (i, C_CQ // DSA_Q_RANK)),
            pl.BlockSpec((tm, DSA_KV_RANK), lambda i: (i, C_CKV // DSA_KV_RANK)),
            pl.BlockSpec((tm, LANES), lambda i: (i, C_IDX // LANES)),
            full((1, DSA_Q_RANK)), full((1, DSA_KV_RANK)),
            full((DSA_Q_RANK, DSA_HEADS * DSA_HEAD_DIM)),
            full((DSA_Q_RANK, IDX_HEADS * IDX_DIM)),
            full((DSA_HEADS * DSA_HEAD_DIM, DSA_HEADS * DSA_KV_RANK)),
        ],
        out_specs=[
            pl.BlockSpec((tm, DSA_HEADS * DSA_KV_RANK), lambda i: (i, 0)),
            pl.BlockSpec((tm, IDX_HEADS * IDX_DIM), lambda i: (i, 0)),
            pl.BlockSpec((tm, DSA_KV_RANK), lambda i: (i, 0)),
            pl.BlockSpec((DSA_KV_RANK, tm), lambda i: (0, i)),
            pl.BlockSpec((tm, IDX_DIM), lambda i: (i, 0)),
        ],
        out_shape=[
            jax.ShapeDtypeStruct((m, DSA_HEADS * DSA_KV_RANK), BF16),
            jax.ShapeDtypeStruct((m, IDX_HEADS * IDX_DIM), BF16),
            jax.ShapeDtypeStruct((m, DSA_KV_RANK), BF16),
            jax.ShapeDtypeStruct((DSA_KV_RANK, m), BF16),
            jax.ShapeDtypeStruct((m, IDX_DIM), BF16),
        ],
        compiler_params=pltpu.CompilerParams(
            dimension_semantics=("arbitrary",), vmem_limit_bytes=VMEM_LIMIT),
        name="dsa_prep",
    )(proj2, proj2, proj2, qn, kn, wuq, wqi, wuk_bd)


def _tree_sum(parts):
    while len(parts) > 1:
        parts = [parts[j] + parts[j + 1] for j in range(0, len(parts) - 1, 2)] + (
            [parts[-1]] if len(parts) % 2 else [])
    return parts[0]


def _dsa_kernel(qidx_ref, wrow_ref, qlat_ref, kidx_ref, ckv_ref, ckvt_ref, wuvt_ref, y_ref,
                key_s, hi_s, lo_s, acc_s, *, qb, tk, n_sel):
    i = pl.program_id(1)
    n_tiles = (i * qb + qb + tk - 1) // tk
    key_pos = lax.broadcasted_iota(jnp.int32, (tk, qb), 0)
    q_pos = i * qb + lax.broadcasted_iota(jnp.int32, (tk, qb), 1)
    w_rows = wrow_ref[...] * (IDX_HEADS ** -0.5 * IDX_DIM ** -0.5)
    qidx = qidx_ref[...]
    q_heads = [qidx[:, h * IDX_DIM:(h + 1) * IDX_DIM] for h in range(IDX_HEADS)]
    i16_min = -(1 << 15)
    i16_max = (1 << 15) - 1
    sub16 = 16
    nt_dims = (((1,), (1,)), ((), ()))

    def score_body(kt, carry):
        k0 = pl.multiple_of(kt * tk, tk)
        kk = kidx_ref[pl.ds(k0, tk), :]
        sh = [lax.dot_general(kk, q_heads[h], nt_dims, preferred_element_type=F32)
              for h in range(IDX_HEADS)]
        acc = jnp.zeros((tk, qb), F32)
        for h in range(IDX_HEADS):
            acc = acc + jnp.maximum(sh[h], 0.0) * w_rows[h:h + 1, :]
        acc = acc + 0.0
        bits = pltpu.bitcast(acc, jnp.int32)
        keys = jnp.where(bits < 0, bits ^ jnp.int32(0x7FFFFFFF), bits)
        keys = jnp.where(k0 + key_pos <= q_pos, keys, jnp.int32(INT_MIN))
        key_s[kt] = keys
        hi_s[kt] = (keys >> 16).astype(jnp.int16)
        return carry

    lax.fori_loop(0, n_tiles, score_body, 0)

    def count_ge(src_s, cand):
        cand16 = jnp.broadcast_to(cand, (sub16, qb)).astype(jnp.int16)
        one, zero = jnp.int16(1), jnp.int16(0)

        def body(kt, acc):
            tile = src_s[kt]
            return acc + _tree_sum([jnp.where(tile[j * sub16:(j + 1) * sub16, :] >= cand16, one, zero)
                                    for j in range(tk // sub16)])

        cnt = lax.fori_loop(0, n_tiles, body, jnp.zeros((sub16, qb), jnp.int16))
        return jnp.sum(cnt.astype(jnp.int32), axis=0, keepdims=True)

    def kth_largest16(src_s, rank):
        thr = jnp.full((1, qb), i16_min, jnp.int32)
        for bit in range(15, -1, -1):
            cand = thr + jnp.int32(1 << bit)
            thr = jnp.where(count_ge(src_s, cand) >= rank, cand, thr)
        return thr

    hi_thr = kth_largest16(hi_s, jnp.int32(n_sel))
    above = count_ge(hi_s, jnp.minimum(hi_thr + 1, jnp.int32(i16_max)))
    rank_lo = jnp.int32(n_sel) - jnp.where(hi_thr < i16_max, above, 0)

    def lo_body(kt, carry):
        keys = key_s[kt]
        lo = (keys & jnp.int32(0xFFFF)) + jnp.int32(i16_min)
        lo_s[kt] = jnp.where((keys >> 16) == hi_thr, lo, jnp.int32(i16_min)).astype(jnp.int16)
        return carry

    lax.fori_loop(0, n_tiles, lo_body, 0)
    lo_thr = kth_largest16(lo_s, rank_lo)
    thr_raw = hi_thr * jnp.int32(1 << 16) + (lo_thr - jnp.int32(i16_min))
    thr = jnp.maximum(thr_raw, jnp.int32(INT_MIN + 1))

    def selected_body(kt, acc):
        return acc + jnp.sum(jnp.where(key_s[kt] >= thr_raw, 1, 0), axis=0, keepdims=True)

    n_selected = lax.fori_loop(0, n_tiles, selected_body, jnp.zeros((1, qb), jnp.int32))
    excess = jnp.where(thr_raw > INT_MIN, n_selected - jnp.int32(n_sel), 0)

    @pl.when(jnp.max(excess) > 0)
    def _():
        def tied_before(cand):
            def body(kt, acc):
                k0 = kt * tk
                hit = jnp.where(key_s[kt] == thr_raw, jnp.where(k0 + key_pos < cand, 1, 0), 0)
                return acc + jnp.sum(hit, axis=0, keepdims=True)
            return lax.fori_loop(0, n_tiles, body, jnp.zeros((1, qb), jnp.int32))

        n_keys = n_tiles * tk
        keep = tied_before(n_keys) - excess
        cut = jnp.zeros((1, qb), jnp.int32)
        for bit in range(max(tk * key_s.shape[0] - 1, 1).bit_length() - 1, -1, -1):
            cand = cut + jnp.int32(1 << bit)
            cut = jnp.where(tied_before(cand) < keep, cand, cut)
        cut = jnp.where(excess > 0, cut, n_keys)

        def demote(kt, carry):
            keys = key_s[kt]
            demoted = jnp.where(kt * tk + key_pos > cut, jnp.int32(INT_MIN), keys)
            key_s[kt] = jnp.where(keys == thr_raw, demoted, keys)
            return carry

        lax.fori_loop(0, n_tiles, demote, 0)

    qlat = qlat_ref[...]
    heads = range(DSA_HEADS)
    ql = [qlat[:, h * DSA_KV_RANK:(h + 1) * DSA_KV_RANK] for h in heads]
    acc_s[...] = jnp.zeros_like(acc_s)

    def attn_body(kt, carry):
        m_run, l_run = carry
        k0 = pl.multiple_of(kt * tk, tk)
        ckv = ckv_ref[pl.ds(k0, tk), :]
        ckvt = ckvt_ref[:, pl.ds(k0, tk)]
        bias = jnp.where(key_s[kt] >= thr, 0.0, -jnp.inf)
        lg = [lax.dot_general(ckv, ql[h], nt_dims, preferred_element_type=F32) for h in heads]
        m_out, l_out, p_all, alphas = [], [], [], []
        for h in heads:
            lgm = lg[h] + bias
            m_new = jnp.maximum(m_run[h], jnp.max(lgm, axis=0, keepdims=True))
            p = jnp.exp(lgm - m_new)
            alpha = jnp.exp(m_run[h] - m_new)
            l_out.append(alpha * l_run[h] + jnp.sum(p, axis=0, keepdims=True))
            m_out.append(m_new)
            p_all.append(p.astype(BF16))
            alphas.append(alpha)
        pv = [jnp.dot(ckvt, p_all[h], preferred_element_type=F32) for h in heads]
        for h in heads:
            acc_s[h] = alphas[h] * acc_s[h] + pv[h]
        return tuple(m_out), tuple(l_out)

    init = (tuple(jnp.full((1, qb), -1e30, F32) for _ in heads),
            tuple(jnp.zeros((1, qb), F32) for _ in heads))
    _, l_fin = lax.fori_loop(0, n_tiles, attn_body, init)
    o_lat = [(acc_s[h] / l_fin[h]).astype(BF16) for h in heads]
    outs = [jnp.dot(wuvt_ref[h], o_lat[h], preferred_element_type=F32) for h in heads]
    y_ref[...] = jnp.concatenate(outs, axis=0).T.astype(y_ref.dtype)


def _dsa(qidx, widx_rows, qlat, kidx, ckvn, ckvt, wuvt, *, qb, tk):
    b, t, _ = qidx.shape
    n_sel = min(IDX_TOPK, t // 4)
    kern = functools.partial(_dsa_kernel, qb=qb, tk=tk, n_sel=n_sel)
    return pl.pallas_call(
        kern,
        grid=(b, t // qb),
        in_specs=[
            pl.BlockSpec((None, qb, IDX_HEADS * IDX_DIM), lambda i, j: (i, j, 0)),
            pl.BlockSpec((None, IDX_HEADS, qb), lambda i, j: (i, 0, j)),
            pl.BlockSpec((None, qb, DSA_HEADS * DSA_KV_RANK), lambda i, j: (i, j, 0)),
            pl.BlockSpec((None, t, IDX_DIM), lambda i, j: (i, 0, 0)),
            pl.BlockSpec((None, t, DSA_KV_RANK), lambda i, j: (i, 0, 0)),
            pl.BlockSpec((DSA_KV_RANK, t), lambda i, j: (0, i)),
            pl.BlockSpec((DSA_HEADS, DSA_HEAD_DIM, DSA_KV_RANK), lambda i, j: (0, 0, 0)),
        ],
        out_specs=pl.BlockSpec((None, qb, DSA_HEADS * DSA_HEAD_DIM), lambda i, j: (i, j, 0)),
        out_shape=jax.ShapeDtypeStruct((b, t, DSA_HEADS * DSA_HEAD_DIM), BF16),
        scratch_shapes=[pltpu.VMEM((t // tk, tk, qb), jnp.int32),
                        pltpu.VMEM((t // tk, tk, qb), jnp.int16),
                        pltpu.VMEM((t // tk, tk, qb), jnp.int16),
                        pltpu.VMEM((DSA_HEADS, DSA_KV_RANK, qb), F32)],
        compiler_params=pltpu.CompilerParams(
            dimension_semantics=("arbitrary", "arbitrary"), vmem_limit_bytes=VMEM_LIMIT),
        name="dsa",
    )(qidx, widx_rows, qlat, kidx, ckvn, ckvt, wuvt)


def _pack_w_in(w):
    pts = [0]
    for wd in IN_WIDTHS:
        pts.append(pts[-1] + wd)
    seg = [w[:, pts[i]:pts[i + 1]] for i in range(len(IN_WIDTHS))]
    (gq, gk, gv, gz, ga, gb, cq, ckv, ik, iw, mq, mk, mv, mo, mi, mf) = seg
    zeros = lambda n: jnp.zeros((w.shape[0], n), w.dtype)
    idx_blk = jnp.concatenate([ik, iw, zeros(LANES - IDX_DIM - IDX_HEADS)], axis=1)
    gate_blk = jnp.concatenate([ga, gb, mi, mf, zeros(LANES - 16)], axis=1)
    out = jnp.concatenate([gq, gk, gv, gz, cq, ckv, idx_blk, mq, mk, mv, mo, gate_blk], axis=1)
    assert out.shape[1] == NP
    return out.astype(BF16)


def _lane_row(vals_at):
    row = jnp.zeros((LANES,), F32)
    for start, v in vals_at:
        row = row.at[start:start + v.shape[0]].set(v.astype(F32))
    return row


def kernel(x, attn_norm, w_in, gdn_conv, gdn_a_log, gdn_dt_bias, gdn_out_norm, dsa_q_norm,
           dsa_kv_norm, dsa_w_uq, dsa_w_qidx, dsa_w_uk, dsa_w_uv, mlstm_i_bias, mlstm_f_bias,
           mlstm_out_norm, w_out, ffn_norm, w_gate, w_up, w_down, final_norm):
    b, t, d = x.shape
    depth = w_in.shape[0]
    gdn_chunk, mlstm_chunk = min(128, t), min(64, t)
    tb = min(512, t)
    qb = min(256, t)
    tk = min(512, t)
    x2 = x.reshape(b * t, d)
    for l in range(depth):
        proj2 = _inproj(x2, attn_norm[l].reshape(1, d), _pack_w_in(w_in[l]))
        proj3 = proj2.reshape(b, t, NP)
        gates_t = jnp.swapaxes(proj3[:, :, C_GATE:C_GATE + 16], 1, 2)

        def chunk_rows(c):
            return jnp.swapaxes(gates_t.reshape(b, 16, t // c, c), 1, 2)

        gpar_c = jnp.stack([_lane_row([(0, gdn_a_log[l])]), _lane_row([(0, gdn_dt_bias[l])])])
        gpar_r = gpar_c[:, :16].T
        y_a = _gdn(proj3, chunk_rows(gdn_chunk), gdn_conv[l], gpar_c, gpar_r,
                   jnp.tile(gdn_out_norm[l], GDN_HEADS).reshape(1, -1), chunk=gdn_chunk, tb=tb)

        wuk_bd = jnp.zeros((DSA_HEADS * DSA_HEAD_DIM, DSA_HEADS * DSA_KV_RANK), F32)
        for h in range(DSA_HEADS):
            wuk_bd = wuk_bd.at[h * DSA_HEAD_DIM:(h + 1) * DSA_HEAD_DIM,
                               h * DSA_KV_RANK:(h + 1) * DSA_KV_RANK].set(dsa_w_uk[l, h].T)
        qlat, qidx, ckvn, ckvt, kidx = _dsa_prep(
            proj2, dsa_q_norm[l].reshape(1, -1), dsa_kv_norm[l].reshape(1, -1),
            dsa_w_uq[l].astype(BF16), dsa_w_qidx[l].astype(BF16), wuk_bd.astype(BF16))
        r3 = lambda a: a.reshape(b, t, a.shape[-1])
        widx_rows = jnp.swapaxes(proj3[:, :, C_IDX + IDX_DIM:C_IDX + IDX_DIM + IDX_HEADS], 1, 2)
        y_b = _dsa(r3(qidx), widx_rows, r3(qlat), r3(kidx), r3(ckvn), ckvt,
                   jnp.swapaxes(dsa_w_uv[l], 1, 2).astype(BF16), qb=qb, tk=tk)

        mb_c = _lane_row([(8, mlstm_i_bias[l]), (12, mlstm_f_bias[l])]).reshape(1, LANES)
        mb_r = mb_c[0, :16].reshape(16, 1)
        y_c = _mlstm(proj3, chunk_rows(mlstm_chunk), mb_c, mb_r,
                     jnp.tile(mlstm_out_norm[l], MLSTM_HEADS).reshape(1, -1),
                     chunk=mlstm_chunk, tb=tb)

        x2 = _out_ffn(x2, y_a.reshape(b * t, -1), y_b.reshape(b * t, -1), y_c.reshape(b * t, -1),
                      w_out[l].astype(BF16), ffn_norm[l].reshape(1, d), w_gate[l].astype(BF16),
                      w_up[l].astype(BF16), w_down[l].astype(BF16), final_norm.reshape(1, d),
                      final_norm=(l == depth - 1))
    return x2.reshape(b, t, d)
```

```python
import functools
import math

import jax
import jax.numpy as jnp
from jax import lax
from jax.experimental import pallas as pl
from jax.experimental.pallas import tpu as pltpu

F32 = jnp.float32
BF16 = jnp.bfloat16
HIGHEST = lax.Precision.HIGHEST

D_MODEL = 1024
GDN_HEADS = 4
GDN_DK = 128
GDN_DV = 128
GDN_CONV = 4
DSA_HEADS = 4
DSA_HEAD_DIM = 64
DSA_Q_RANK = 256
DSA_KV_RANK = 128
IDX_HEADS = 8
IDX_DIM = 32
IDX_TOPK = 256
MLSTM_HEADS = 4
MLSTM_DQK = 64
MLSTM_DV = 64
FFN_HIDDEN = 2816
NORM_EPS = 1e-6

IN_WIDTHS = (512, 512, 512, 512, 4, 4, 256, 128, 32, 8, 256, 256, 256, 256, 4, 4)

C_GQ, C_GK, C_GV, C_GZ = 0, 512, 1024, 1536
C_CQ, C_CKV, C_IDX = 2048, 2304, 2432
C_MQ, C_MK, C_MV, C_MO = 2560, 2816, 3072, 3328
C_GATE = 3584
NP = 3712
LANES = 128

VMEM_LIMIT = 56 * 1024 * 1024

INT_MIN = -2 ** 31


def _mm(a, b):
    return jnp.dot(a.astype(BF16), b.astype(BF16), preferred_element_type=F32)


def _mm_nt(a, b):
    return lax.dot_general(a.astype(BF16), b.astype(BF16), (((1,), (1,)), ((), ())),
                           preferred_element_type=F32)


def _sigmoid(x):
    return 1.0 / (1.0 + jnp.exp(-x))


def _softplus(x):
    return jnp.maximum(x, 0.0) + jnp.log1p(jnp.exp(-jnp.abs(x)))


def _tri(n, kind):
    r = lax.broadcasted_iota(jnp.int32, (n, n), 0)
    c = lax.broadcasted_iota(jnp.int32, (n, n), 1)
    if kind == "incl":
        return r >= c
    if kind == "strict":
        return r > c
    if kind == "upper_incl":
        return r <= c
    raise ValueError(kind)


def _inproj_kernel(x_ref, g_ref, w_ref, o_ref):
    x = x_ref[...]
    ms = jnp.mean(x * x, axis=-1, keepdims=True)
    h = (x * lax.rsqrt(ms + NORM_EPS)) * g_ref[...]
    o_ref[...] = jnp.dot(h.astype(BF16), w_ref[...], preferred_element_type=F32)


def _inproj(x2, g, w):
    m = x2.shape[0]
    tm = 256
    return pl.pallas_call(
        _inproj_kernel,
        grid=(m // tm,),
        in_specs=[
            pl.BlockSpec((tm, D_MODEL), lambda i: (i, 0)),
            pl.BlockSpec((1, D_MODEL), lambda i: (0, 0)),
            pl.BlockSpec((D_MODEL, NP), lambda i: (0, 0)),
        ],
        out_specs=pl.BlockSpec((tm, NP), lambda i: (i, 0)),
        out_shape=jax.ShapeDtypeStruct((m, NP), F32),
        compiler_params=pltpu.CompilerParams(
            dimension_semantics=("arbitrary",), vmem_limit_bytes=VMEM_LIMIT),
        name="inproj",
    )(x2, g, w)


def _ffn_kernel(x_ref, ya_ref, yb_ref, yc_ref, wo_ref, g_ref, wg_ref, wu_ref, wd_ref, fg_ref,
                o_ref, acc_ref, h_ref, *, n_hidden_steps, final_norm):
    j = pl.program_id(1)

    @pl.when(j == 0)
    def _():
        xn = x_ref[...]
        xn = xn + jnp.dot(ya_ref[...], wo_ref[0:512, :], preferred_element_type=F32)
        xn = xn + jnp.dot(yb_ref[...], wo_ref[512:768, :], preferred_element_type=F32)
        xn = xn + jnp.dot(yc_ref[...], wo_ref[768:1024, :], preferred_element_type=F32)
        acc_ref[...] = xn
        ms = jnp.mean(xn * xn, axis=-1, keepdims=True)
        h_ref[...] = ((xn * lax.rsqrt(ms + NORM_EPS)) * g_ref[...]).astype(BF16)

    h = h_ref[...]
    gate = jnp.dot(h, wg_ref[...], preferred_element_type=F32)
    up = jnp.dot(h, wu_ref[...], preferred_element_type=F32)
    act = (gate * _sigmoid(gate)) * up
    acc_ref[...] += jnp.dot(act.astype(BF16), wd_ref[...], preferred_element_type=F32)

    @pl.when(j == n_hidden_steps - 1)
    def _():
        y = acc_ref[...]
        if final_norm:
            ms = jnp.mean(y * y, axis=-1, keepdims=True)
            y = (y * lax.rsqrt(ms + NORM_EPS)) * fg_ref[...]
        o_ref[...] = y


def _out_ffn(x2, ya, yb, yc, wo, g, wg, wu, wd, fg, final_norm):
    m = x2.shape[0]
    tm = 512
    th = 1408
    nh = FFN_HIDDEN // th
    kern = functools.partial(_ffn_kernel, n_hidden_steps=nh, final_norm=final_norm)
    return pl.pallas_call(
        kern,
        grid=(m // tm, nh),
        in_specs=[
            pl.BlockSpec((tm, D_MODEL), lambda i, j: (i, 0)),
            pl.BlockSpec((tm, 512), lambda i, j: (i, 0)),
            pl.BlockSpec((tm, 256), lambda i, j: (i, 0)),
            pl.BlockSpec((tm, 256), lambda i, j: (i, 0)),
            pl.BlockSpec((D_MODEL, D_MODEL), lambda i, j: (0, 0)),
            pl.BlockSpec((1, D_MODEL), lambda i, j: (0, 0)),
            pl.BlockSpec((D_MODEL, th), lambda i, j: (0, j)),
            pl.BlockSpec((D_MODEL, th), lambda i, j: (0, j)),
            pl.BlockSpec((th, D_MODEL), lambda i, j: (j, 0)),
            pl.BlockSpec((1, D_MODEL), lambda i, j: (0, 0)),
        ],
        out_specs=pl.BlockSpec((tm, D_MODEL), lambda i, j: (i, 0)),
        out_shape=jax.ShapeDtypeStruct((m, D_MODEL), F32),
        scratch_shapes=[pltpu.VMEM((tm, D_MODEL), F32), pltpu.VMEM((tm, D_MODEL), BF16)],
        compiler_params=pltpu.CompilerParams(
            dimension_semantics=("arbitrary", "arbitrary"), vmem_limit_bytes=VMEM_LIMIT),
        name="out_ffn",
    )(x2, ya, yb, yc, wo, g, wg, wu, wd, fg)


def _mlstm_kernel(q_ref, k_ref, v_ref, o_ref, gc_ref, gr_ref, bc_ref, br_ref, og_ref,
                  y_ref, sv_s, kv_s, bcol_s, dm_s, km_s, c_s, m_s, *, chunk, n_chunks, group):
    L = chunk
    t = pl.program_id(1)
    heads = range(MLSTM_HEADS)

    @pl.when(t == 0)
    def _():
        c_s[...] = jnp.zeros_like(c_s)
        m_s[...] = jnp.zeros_like(m_s)

    incl = _tri(L, "incl")
    tril_b = incl.astype(BF16)
    triu_b = _tri(L, "upper_incl").astype(BF16)
    e0 = (lax.broadcasted_iota(jnp.int32, (L, MLSTM_DV), 1) == 0).astype(F32)
    lane_l = lax.broadcasted_iota(jnp.int32, (L, LANES), 1)
    lane_8 = lax.broadcasted_iota(jnp.int32, (8, LANES), 1)
    scale = MLSTM_DQK ** -0.5

    def prep(cg, carry):
        ch = []
        gate_tiles = []
        for g in range(group):
            c = cg * group + g
            r0 = pl.multiple_of(c * L, L)
            x_col = gc_ref[pl.ds(r0, L), :] + bc_ref[...]
            x_row = gr_ref[c] + br_ref[...]
            b_col = _cumsum_rows(tril_b, -_softplus(-x_col))
            b_row = _cumsum_lanes(-_softplus(-x_row), triu_b)
            dm_tile = jnp.zeros((L, LANES), F32)
            km_tile = jnp.zeros((8, LANES), F32)
            for h in heads:
                sl = slice(h * MLSTM_DQK, (h + 1) * MLSTM_DQK)
                bx_c = b_col[:, 12 + h:13 + h]
                d_log = jnp.where(incl, bx_c - b_row[12 + h:13 + h, :] + x_row[8 + h:9 + h, :], -jnp.inf)
                dmax = jnp.max(d_log, axis=-1, keepdims=True)
                kmax = dmax[L - 1:L, :]
                kw = jnp.exp(bx_c[L - 1:L, :] - bx_c + x_col[:, 8 + h:9 + h] - kmax)
                k = k_ref[pl.ds(r0, L), sl] * scale
                v_aug = jnp.concatenate([v_ref[pl.ds(r0, L), sl], e0], axis=1)
                ch.append(dict(idx=c * MLSTM_HEADS + h, q=q_ref[pl.ds(r0, L), sl], k=k, v=v_aug,
                               dw=jnp.exp(d_log - dmax), kxw_t=(k * kw).T))
                dm_tile = jnp.where(lane_l == h, dmax, dm_tile)
                km_tile = jnp.where(lane_8 == h, kmax, km_tile)
            gate_tiles.append((c, b_col, dm_tile, km_tile))
        for d in ch:
            d["s"] = _mm_nt(d["q"], d["k"]) * d["dw"]
        for d in ch:
            d["sv"] = _mm(d["s"], d["v"])
        for d in ch:
            d["kv"] = _mm(d["kxw_t"], d["v"])
        for d in ch:
            sv_s[d["idx"]] = d["sv"]
            kv_s[d["idx"]] = d["kv"]
        for c, b_col, dm_tile, km_tile in gate_tiles:
            bcol_s[c] = b_col
            dm_s[c] = dm_tile
            km_s[c] = km_tile
        return carry

    lax.fori_loop(0, n_chunks // group, prep, 0)

    def scan(c, carry):
        r0 = pl.multiple_of(c * L, L)
        b_col = bcol_s[c]
        dm = dm_s[c]
        km = km_s[c]
        c_prev = [c_s[h] for h in heads]
        qc = [_mm(q_ref[pl.ds(r0, L), h * MLSTM_DQK:(h + 1) * MLSTM_DQK], c_prev[h]) for h in heads]
        outs = []
        for h in heads:
            idx = c * MLSTM_HEADS + h
            bx_c = b_col[:, 12 + h:13 + h]
            dmax = dm[:, h:h + 1]
            kmax = km[0:1, h:h + 1]
            m_prev = m_s[h:h + 1, 0:1]
            a_log = bx_c + m_prev
            m_t = jnp.maximum(a_log, dmax)
            num = jnp.exp(a_log - m_t) * qc[h] + jnp.exp(dmax - m_t) * sv_s[idx]
            den = num[:, MLSTM_DV:MLSTM_DV + 1]
            hh = num[:, :MLSTM_DV] / jnp.maximum(jnp.abs(den), jnp.exp(-m_t))
            m_new = m_t[L - 1:L, :]
            dec = jnp.exp(bx_c[L - 1:L, :] + m_prev - m_new)
            c_s[h] = dec * c_prev[h] + jnp.exp(kmax - m_new) * kv_s[idx]
            m_s[h:h + 1, :] = jnp.broadcast_to(m_new, (1, LANES))
            ms = jnp.mean(hh * hh, axis=-1, keepdims=True)
            outs.append(hh * lax.rsqrt(ms + NORM_EPS))
        y = jnp.concatenate(outs, axis=1) * og_ref[...]
        y = y * _sigmoid(o_ref[pl.ds(r0, L), :])
        y_ref[pl.ds(r0, L), :] = y.astype(y_ref.dtype)
        return carry

    lax.fori_loop(0, n_chunks, scan, 0)


def _mlstm(proj3, gates_r, bias_c, bias_r, og, *, chunk, tb):
    b, t, _ = proj3.shape
    nc = tb // chunk
    nch = nc * MLSTM_HEADS
    wq = MLSTM_HEADS * MLSTM_DQK
    group = 2 if nc % 2 == 0 else 1
    kern = functools.partial(_mlstm_kernel, chunk=chunk, n_chunks=nc, group=group)
    col = lambda c0: pl.BlockSpec((None, tb, wq), lambda i, j, c0=c0: (i, j, c0 // wq))
    return pl.pallas_call(
        kern,
        grid=(b, t // tb),
        in_specs=[
            col(C_MQ), col(C_MK), col(C_MV), col(C_MO),
            pl.BlockSpec((None, tb, LANES), lambda i, j: (i, j, C_GATE // LANES)),
            pl.BlockSpec((None, nc, 16, chunk), lambda i, j: (i, j, 0, 0)),
            pl.BlockSpec((1, LANES), lambda i, j: (0, 0)),
            pl.BlockSpec((16, 1), lambda i, j: (0, 0)),
            pl.BlockSpec((1, wq), lambda i, j: (0, 0)),
        ],
        out_specs=pl.BlockSpec((None, tb, wq), lambda i, j: (i, j, 0)),
        out_shape=jax.ShapeDtypeStruct((b, t, wq), BF16),
        scratch_shapes=[pltpu.VMEM((nch, chunk, LANES), F32),
                        pltpu.VMEM((nch, MLSTM_DQK, LANES), F32),
                        pltpu.VMEM((nc, chunk, LANES), F32),
                        pltpu.VMEM((nc, chunk, LANES), F32),
                        pltpu.VMEM((nc, 8, LANES), F32),
                        pltpu.VMEM((MLSTM_HEADS, MLSTM_DQK, LANES), F32),
                        pltpu.VMEM((8, LANES), F32)],
        compiler_params=pltpu.CompilerParams(
            dimension_semantics=("arbitrary", "arbitrary"), vmem_limit_bytes=VMEM_LIMIT),
        name="mlstm",
    )(proj3, proj3, proj3, proj3, proj3, gates_r, bias_c, bias_r, og)


def _split3(x):
    x1 = x.astype(BF16)
    r1 = x - x1.astype(F32)
    x2 = r1.astype(BF16)
    x3 = (r1 - x2.astype(F32)).astype(BF16)
    return x1, x2, x3


def _cumsum_rows(tril_b, x):
    return sum(jnp.dot(tril_b, p, preferred_element_type=F32) for p in _split3(x))


def _cumsum_lanes(x, triu_b):
    return sum(jnp.dot(p, triu_b, preferred_element_type=F32) for p in _split3(x))


def _inverse_masks(n):
    r = lax.broadcasted_iota(jnp.int32, (n, n), 0)
    c = lax.broadcasted_iota(jnp.int32, (n, n), 1)
    offs = []
    lvl = 0
    while (1 << lvl) < n:
        rb = r >> lvl
        cb = c >> lvl
        offs.append(((rb & 1) == 1) & (cb == rb - 1))
        lvl += 1
    return (r == c).astype(F32), offs


def _gdn_kernel(q_ref, k_ref, v_ref, z_ref, qh_ref, kh_ref, vh_ref, gc_ref, gr_ref, cw_ref,
                pc_ref, pr_ref, og_ref, y_ref, qs, ks, vs, u_s, w_s, qk_s, qd_s, kdt_s, el_s, s_s,
                *, chunk, n_chunks, tb, group):
    C = chunk
    t = pl.program_id(1)
    width = GDN_HEADS * GDN_DK

    @pl.when(t == 0)
    def _():
        s_s[...] = jnp.zeros_like(s_s)

    def conv_silu(x_ref, halo_ref, w0):
        halo = jnp.where(t == 0, 0.0, halo_ref[...])
        xf = jnp.concatenate([halo, x_ref[...]], axis=0)
        y = xf * cw_ref[GDN_CONV - 1:GDN_CONV, w0:w0 + width]
        for j in range(GDN_CONV - 1):
            y = y + pltpu.roll(xf, GDN_CONV - 1 - j, axis=0) * cw_ref[j:j + 1, w0:w0 + width]
        y = y[8:, :]
        return y * _sigmoid(y)

    def l2n(x):
        parts = []
        for h in range(GDN_HEADS):
            xh = x[:, h * GDN_DK:(h + 1) * GDN_DK]
            parts.append(xh * lax.rsqrt(jnp.sum(xh * xh, axis=-1, keepdims=True) + NORM_EPS))
        return jnp.concatenate(parts, axis=1)

    qs[...] = l2n(conv_silu(q_ref, qh_ref, 0)) * (GDN_DK ** -0.5)
    ks[...] = l2n(conv_silu(k_ref, kh_ref, width))
    vs[...] = conv_silu(v_ref, vh_ref, 2 * width)

    incl = _tri(C, "incl")
    strict = _tri(C, "strict")
    tril_b = incl.astype(BF16)
    triu_b = _tri(C, "upper_incl").astype(BF16)
    inv_masks = _inverse_masks(C)

    def prep(cg, carry):
        loaded = []
        for g in range(group):
            c = cg * group + g
            r0 = pl.multiple_of(c * C, C)
            gcol = gc_ref[pl.ds(r0, C), :]
            grow = gr_ref[c]
            qkv = [(qs[pl.ds(r0, C), h * GDN_DK:(h + 1) * GDN_DK],
                    ks[pl.ds(r0, C), h * GDN_DK:(h + 1) * GDN_DK],
                    vs[pl.ds(r0, C), h * GDN_DK:(h + 1) * GDN_DK]) for h in range(GDN_HEADS)]
            loaded.append((c, gcol, grow, qkv))
        ch = []
        for c, gcol, grow, qkv in loaded:
            g_col = -jnp.exp(pc_ref[0:1, :]) * _softplus(gcol + pc_ref[1:2, :])
            gcum_col = _cumsum_rows(tril_b, g_col)
            beta_col = _sigmoid(gcol)
            g_row = -jnp.exp(pr_ref[:, 0:1]) * _softplus(grow + pr_ref[:, 1:2])
            gcum_row = _cumsum_lanes(g_row, triu_b)
            for h in range(GDN_HEADS):
                q, k, v = qkv[h]
                gc_c = gcum_col[:, h:h + 1]
                gc_r = gcum_row[h:h + 1, :]
                beta = beta_col[:, 4 + h:5 + h]
                decay = jnp.where(incl, jnp.exp(jnp.where(incl, gc_c - gc_r, 0.0)), 0.0)
                ch.append(dict(idx=c * GDN_HEADS + h, q=q, k=k, v=v, gc_c=gc_c, beta=beta,
                               decay=decay, kb=k * beta))
        for d in ch:
            d["a"] = jnp.where(strict, _mm_nt(d["kb"], d["k"]) * d["decay"], 0.0)
        eye, offs = inv_masks
        for d in ch:
            d["t"] = eye - jnp.where(offs[0], d["a"], 0.0)
        for off in offs[1:]:
            for d in ch:
                d["p"] = _mm(d["t"], jnp.where(off, d["a"], 0.0))
            for d in ch:
                d["t"] = d["t"] - _mm(d["p"], d["t"])
        for d in ch:
            d["egc"] = jnp.exp(d["gc_c"])
            d["u"] = _mm(d["t"], d["v"] * d["beta"])
        for d in ch:
            d["w"] = _mm(d["t"], d["kb"] * d["egc"]).astype(BF16)
        for d in ch:
            d["qk"] = (_mm_nt(d["q"], d["k"]) * d["decay"]).astype(BF16)
        for d in ch:
            idx = d["idx"]
            g_last = d["gc_c"][C - 1:C, :]
            u_s[idx] = d["u"]
            w_s[idx] = d["w"]
            qk_s[idx] = d["qk"]
            qd_s[idx] = (d["q"] * d["egc"]).astype(BF16)
            kdt_s[idx] = (d["k"] * jnp.exp(g_last - d["gc_c"])).T.astype(BF16)
            el_s[idx] = jnp.broadcast_to(jnp.exp(g_last), (8, LANES))
        return carry

    lax.fori_loop(0, n_chunks // group, prep, 0)

    def scan(c, carry):
        r0 = pl.multiple_of(c * C, C)
        heads = range(GDN_HEADS)
        dot = functools.partial(jnp.dot, preferred_element_type=F32)
        idx = [c * GDN_HEADS + h for h in heads]
        s_prev = [s_s[h] for h in heads]
        s_b = [s.astype(BF16) for s in s_prev]
        ws = [dot(w_s[idx[h]], s_b[h]) for h in heads]
        qs_ = [dot(qd_s[idx[h]], s_b[h]) for h in heads]
        v_b = [(u_s[idx[h]] - ws[h]).astype(BF16) for h in heads]
        upd = [dot(kdt_s[idx[h]], v_b[h]) for h in heads]
        o_all = [qs_[h] + dot(qk_s[idx[h]], v_b[h]) for h in heads]
        outs = []
        for h in heads:
            s_s[h] = s_prev[h] * el_s[idx[h]][0:1, :] + upd[h]
            o = o_all[h]
            ms = jnp.mean(o * o, axis=-1, keepdims=True)
            outs.append(o * lax.rsqrt(ms + NORM_EPS))
        y = jnp.concatenate(outs, axis=1) * og_ref[...]
        z = z_ref[pl.ds(r0, C), :]
        y = y * (z * _sigmoid(z))
        y_ref[pl.ds(r0, C), :] = y.astype(y_ref.dtype)
        return carry

    lax.fori_loop(0, n_chunks, scan, 0)


def _gdn(proj3, gates_r, conv_w, par_c, par_r, og, *, chunk, tb):
    b, t, _ = proj3.shape
    nc = tb // chunk
    nch = nc * GDN_HEADS
    width = GDN_HEADS * GDN_DK
    group = 2 if nc % 2 == 0 else 1
    kern = functools.partial(_gdn_kernel, chunk=chunk, n_chunks=nc, tb=tb, group=group)
    col = lambda c0: pl.BlockSpec((None, tb, width), lambda i, j, c0=c0: (i, j, c0 // width))
    halo = lambda c0: pl.BlockSpec(
        (None, 8, width), lambda i, j, c0=c0: (i, jnp.maximum(j * (tb // 8) - 1, 0), c0 // width))
    return pl.pallas_call(
        kern,
        grid=(b, t // tb),
        in_specs=[
            col(C_GQ), col(C_GK), col(C_GV), col(C_GZ),
            halo(C_GQ), halo(C_GK), halo(C_GV),
            pl.BlockSpec((None, tb, LANES), lambda i, j: (i, j, C_GATE // LANES)),
            pl.BlockSpec((None, nc, 16, chunk), lambda i, j: (i, j, 0, 0)),
            pl.BlockSpec((GDN_CONV, 3 * width), lambda i, j: (0, 0)),
            pl.BlockSpec((2, LANES), lambda i, j: (0, 0)),
            pl.BlockSpec((16, 2), lambda i, j: (0, 0)),
            pl.BlockSpec((1, width), lambda i, j: (0, 0)),
        ],
        out_specs=pl.BlockSpec((None, tb, width), lambda i, j: (i, j, 0)),
        out_shape=jax.ShapeDtypeStruct((b, t, width), BF16),
        scratch_shapes=[pltpu.VMEM((tb, width), F32), pltpu.VMEM((tb, width), F32),
                        pltpu.VMEM((tb, width), F32),
                        pltpu.VMEM((nch, chunk, GDN_DV), F32),
                        pltpu.VMEM((nch, chunk, GDN_DK), BF16),
                        pltpu.VMEM((nch, chunk, chunk), BF16),
                        pltpu.VMEM((nch, chunk, GDN_DK), BF16),
                        pltpu.VMEM((nch, GDN_DK, chunk), BF16),
                        pltpu.VMEM((nch, 8, LANES), F32),
                        pltpu.VMEM((GDN_HEADS, GDN_DK, GDN_DV), F32)],
        compiler_params=pltpu.CompilerParams(
            dimension_semantics=("arbitrary", "arbitrary"), vmem_limit_bytes=VMEM_LIMIT),
        name="gdn",
    )(proj3, proj3, proj3, proj3, proj3, proj3, proj3, proj3, gates_r, conv_w, par_c, par_r, og)


def _dsa_prep_kernel(cq_ref, ckv_ref, idx_ref, qn_ref, kn_ref, wuq_ref, wqi_ref, wuk_ref,
                     qlat_ref, qidx_ref, ckvn_ref, ckvt_ref, kidx_ref):
    cq = cq_ref[...]
    ms = jnp.mean(cq * cq, axis=-1, keepdims=True)
    cqn = ((cq * lax.rsqrt(ms + NORM_EPS)) * qn_ref[...]).astype(BF16)
    ckv = ckv_ref[...]
    ms = jnp.mean(ckv * ckv, axis=-1, keepdims=True)
    ckvn = (ckv * lax.rsqrt(ms + NORM_EPS)) * kn_ref[...]
    ckvn_ref[...] = ckvn.astype(BF16)
    ckvt_ref[...] = ckvn.T.astype(BF16)
    q = jnp.dot(cqn, wuq_ref[...], preferred_element_type=F32)
    qlat = jnp.dot(q.astype(BF16), wuk_ref[...], preferred_element_type=F32) * (DSA_HEAD_DIM ** -0.5)
    qlat_ref[...] = qlat.astype(BF16)
    qidx_ref[...] = jnp.dot(cqn, wqi_ref[...], preferred_element_type=F32).astype(BF16)
    kidx_ref[...] = idx_ref[:, 0:IDX_DIM].astype(BF16)


def _dsa_prep(proj2, qn, kn, wuq, wqi, wuk_bd):
    m = proj2.shape[0]
    tm = 512
    full = lambda shape: pl.BlockSpec(shape, lambda i: (0, 0))
    return pl.pallas_call(
        _dsa_prep_kernel,
        grid=(m // tm,),
        in_specs=[
            pl.BlockSpec((tm, DSA_Q_RANK), lambda i: (i, C_CQ // DSA_Q_RANK)),
            pl.BlockSpec((tm, DSA_KV_RANK), lambda i: (i, C_CKV // DSA_KV_RANK)),
            pl.BlockSpec((tm, LANES), lambda i: (i, C_IDX // LANES)),
            full((1, DSA_Q_RANK)), full((1, DSA_KV_RANK)),
            full((DSA_Q_RANK, DSA_HEADS * DSA_HEAD_DIM)),
            full((DSA_Q_RANK, IDX_HEADS * IDX_DIM)),
            full((DSA_HEADS * DSA_HEAD_DIM, DSA_HEADS * DSA_KV_RANK)),
        ],
        out_specs=[
            pl.BlockSpec((tm, DSA_HEADS * DSA_KV_RANK), lambda i: (i, 0)),
            pl.BlockSpec((tm, IDX_HEADS * IDX_DIM), lambda i: (i, 0)),
            pl.BlockSpec((tm, DSA_KV_RANK), lambda i: (i, 0)),
            pl.BlockSpec((DSA_KV_RANK, tm), lambda i: (0, i)),
            pl.BlockSpec((tm, IDX_DIM), lambda i: (i, 0)),
        ],
        out_shape=[
            jax.ShapeDtypeStruct((m, DSA_HEADS * DSA_KV_RANK), BF16),
            jax.ShapeDtypeStruct((m, IDX_HEADS * IDX_DIM), BF16),
            jax.ShapeDtypeStruct((m, DSA_KV_RANK), BF16),
            jax.ShapeDtypeStruct((DSA_KV_RANK, m), BF16),
            jax.ShapeDtypeStruct((m, IDX_DIM), BF16),
        ],
        compiler_params=pltpu.CompilerParams(
            dimension_semantics=("arbitrary",), vmem_limit_bytes=VMEM_LIMIT),
        name="dsa_prep",
    )(proj2, proj2, proj2, qn, kn, wuq, wqi, wuk_bd)


def _tree_sum(parts):
    while len(parts) > 1:
        parts = [parts[j] + parts[j + 1] for j in range(0, len(parts) - 1, 2)] + (
            [parts[-1]] if len(parts) % 2 else [])
    return parts[0]


def _dsa_kernel(qidx_ref, wrow_ref, qlat_ref, kidx_ref, ckv_ref, ckvt_ref, wuvt_ref, y_ref,
                key_s, hi_s, lo_s, acc_s, *, qb, tk, n_sel):
    i = pl.program_id(1)
    n_tiles = (i * qb + qb + tk - 1) // tk
    key_pos = lax.broadcasted_iota(jnp.int32, (tk, qb), 0)
    q_pos = i * qb + lax.broadcasted_iota(jnp.int32, (tk, qb), 1)
    w_rows = wrow_ref[...] * (IDX_HEADS ** -0.5 * IDX_DIM ** -0.5)
    qidx = qidx_ref[...]
    q_heads = [qidx[:, h * IDX_DIM:(h + 1) * IDX_DIM] for h in range(IDX_HEADS)]
    i16_min = -(1 << 15)
    i16_max = (1 << 15) - 1
    sub16 = 16
    nt_dims = (((1,), (1,)), ((), ()))

    def score_body(kt, carry):
        k0 = pl.multiple_of(kt * tk, tk)
        kk = kidx_ref[pl.ds(k0, tk), :]
        sh = [lax.dot_general(kk, q_heads[h], nt_dims, preferred_element_type=F32)
              for h in range(IDX_HEADS)]
        acc = jnp.zeros((tk, qb), F32)
        for h in range(IDX_HEADS):
            acc = acc + jnp.maximum(sh[h], 0.0) * w_rows[h:h + 1, :]
        acc = acc + 0.0
        bits = pltpu.bitcast(acc, jnp.int32)
        keys = jnp.where(bits < 0, bits ^ jnp.int32(0x7FFFFFFF), bits)
        keys = jnp.where(k0 + key_pos <= q_pos, keys, jnp.int32(INT_MIN))
        key_s[kt] = keys
        hi_s[kt] = (keys >> 16).astype(jnp.int16)
        return carry

    lax.fori_loop(0, n_tiles, score_body, 0)

    def count_ge(src_s, cand):
        cand16 = jnp.broadcast_to(cand, (sub16, qb)).astype(jnp.int16)
        one, zero = jnp.int16(1), jnp.int16(0)

        def body(kt, acc):
            tile = src_s[kt]
            return acc + _tree_sum([jnp.where(tile[j * sub16:(j + 1) * sub16, :] >= cand16, one, zero)
                                    for j in range(tk // sub16)])

        cnt = lax.fori_loop(0, n_tiles, body, jnp.zeros((sub16, qb), jnp.int16))
        return jnp.sum(cnt.astype(jnp.int32), axis=0, keepdims=True)

    def kth_largest16(src_s, rank):
        thr = jnp.full((1, qb), i16_min, jnp.int32)
        for bit in range(15, -1, -1):
            cand = thr + jnp.int32(1 << bit)
            thr = jnp.where(count_ge(src_s, cand) >= rank, cand, thr)
        return thr

    hi_thr = kth_largest16(hi_s, jnp.int32(n_sel))
    above = count_ge(hi_s, jnp.minimum(hi_thr + 1, jnp.int32(i16_max)))
    rank_lo = jnp.int32(n_sel) - jnp.where(hi_thr < i16_max, above, 0)

    def lo_body(kt, carry):
        keys = key_s[kt]
        lo = (keys & jnp.int32(0xFFFF)) + jnp.int32(i16_min)
        lo_s[kt] = jnp.where((keys >> 16) == hi_thr, lo, jnp.int32(i16_min)).astype(jnp.int16)
        return carry

    lax.fori_loop(0, n_tiles, lo_body, 0)
    lo_thr = kth_largest16(lo_s, rank_lo)
    thr_raw = hi_thr * jnp.int32(1 << 16) + (lo_thr - jnp.int32(i16_min))
    thr = jnp.maximum(thr_raw, jnp.int32(INT_MIN + 1))

    def selected_body(kt, acc):
        return acc + jnp.sum(jnp.where(key_s[kt] >= thr_raw, 1, 0), axis=0, keepdims=True)

    n_selected = lax.fori_loop(0, n_tiles, selected_body, jnp.zeros((1, qb), jnp.int32))
    excess = jnp.where(thr_raw > INT_MIN, n_selected - jnp.int32(n_sel), 0)

    @pl.when(jnp.max(excess) > 0)
    def _():
        last = tk * key_s.shape[0] - 1
        assert last <= i16_max

        def fill(kt, carry):
            rev = jnp.int32(last) - (kt * tk + key_pos)
            lo_s[kt] = jnp.where(key_s[kt] == thr_raw, rev, jnp.int32(-1)).astype(jnp.int16)
            return carry

        lax.fori_loop(0, n_tiles, fill, 0)
        keep = count_ge(lo_s, jnp.zeros((1, qb), jnp.int32)) - excess
        cut = jnp.zeros((1, qb), jnp.int32)
        for bit in range(max(last, 1).bit_length() - 1, -1, -1):
            cand = cut + jnp.int32(1 << bit)
            cut = jnp.where(count_ge(lo_s, cand) >= keep, cand, cut)
        cut = jnp.where(excess > 0, cut, 0)

        def demote(kt, carry):
            keys = key_s[kt]
            rev = jnp.int32(last) - (kt * tk + key_pos)
            demoted = jnp.where(rev < cut, jnp.int32(INT_MIN), keys)
            key_s[kt] = jnp.where(keys == thr_raw, demoted, keys)
            return carry

        lax.fori_loop(0, n_tiles, demote, 0)

    qlat = qlat_ref[...]
    heads = range(DSA_HEADS)
    ql = [qlat[:, h * DSA_KV_RANK:(h + 1) * DSA_KV_RANK] for h in heads]
    acc_s[...] = jnp.zeros_like(acc_s)

    def attn_body(kt, carry):
        m_run, l_run = carry
        k0 = pl.multiple_of(kt * tk, tk)
        ckv = ckv_ref[pl.ds(k0, tk), :]
        ckvt = ckvt_ref[:, pl.ds(k0, tk)]
        bias = jnp.where(key_s[kt] >= thr, 0.0, -jnp.inf)
        lg = [lax.dot_general(ckv, ql[h], nt_dims, preferred_element_type=F32) for h in heads]
        m_out, l_out, p_all, alphas = [], [], [], []
        for h in heads:
            lgm = lg[h] + bias
            m_new = jnp.maximum(m_run[h], jnp.max(lgm, axis=0, keepdims=True))
            p = jnp.exp(lgm - m_new)
            alpha = jnp.exp(m_run[h] - m_new)
            l_out.append(alpha * l_run[h] + jnp.sum(p, axis=0, keepdims=True))
            m_out.append(m_new)
            p_all.append(p.astype(BF16))
            alphas.append(alpha)
        pv = [jnp.dot(ckvt, p_all[h], preferred_element_type=F32) for h in heads]
        for h in heads:
            acc_s[h] = alphas[h] * acc_s[h] + pv[h]
        return tuple(m_out), tuple(l_out)

    init = (tuple(jnp.full((1, qb), -1e30, F32) for _ in heads),
            tuple(jnp.zeros((1, qb), F32) for _ in heads))
    _, l_fin = lax.fori_loop(0, n_tiles, attn_body, init)
    o_lat = [(acc_s[h] / l_fin[h]).astype(BF16) for h in heads]
    outs = [jnp.dot(wuvt_ref[h], o_lat[h], preferred_element_type=F32) for h in heads]
    y_ref[...] = jnp.concatenate(outs, axis=0).T.astype(y_ref.dtype)


def _dsa(qidx, widx_rows, qlat, kidx, ckvn, ckvt, wuvt, *, qb, tk):
    b, t, _ = qidx.shape
    n_sel = min(IDX_TOPK, t // 4)
    kern = functools.partial(_dsa_kernel, qb=qb, tk=tk, n_sel=n_sel)
    return pl.pallas_call(
        kern,
        grid=(b, t // qb),
        in_specs=[
            pl.BlockSpec((None, qb, IDX_HEADS * IDX_DIM), lambda i, j: (i, j, 0)),
            pl.BlockSpec((None, IDX_HEADS, qb), lambda i, j: (i, 0, j)),
            pl.BlockSpec((None, qb, DSA_HEADS * DSA_KV_RANK), lambda i, j: (i, j, 0)),
            pl.BlockSpec((None, t, IDX_DIM), lambda i, j: (i, 0, 0)),
            pl.BlockSpec((None, t, DSA_KV_RANK), lambda i, j: (i, 0, 0)),
            pl.BlockSpec((DSA_KV_RANK, t), lambda i, j: (0, i)),
            pl.BlockSpec((DSA_HEADS, DSA_HEAD_DIM, DSA_KV_RANK), lambda i, j: (0, 0, 0)),
        ],
        out_specs=pl.BlockSpec((None, qb, DSA_HEADS * DSA_HEAD_DIM), lambda i, j: (i, j, 0)),
        out_shape=jax.ShapeDtypeStruct((b, t, DSA_HEADS * DSA_HEAD_DIM), BF16),
        scratch_shapes=[pltpu.VMEM((t // tk, tk, qb), jnp.int32),
                        pltpu.VMEM((t // tk, tk, qb), jnp.int16),
                        pltpu.VMEM((t // tk, tk, qb), jnp.int16),
                        pltpu.VMEM((DSA_HEADS, DSA_KV_RANK, qb), F32)],
        compiler_params=pltpu.CompilerParams(
            dimension_semantics=("arbitrary", "arbitrary"), vmem_limit_bytes=VMEM_LIMIT),
        name="dsa",
    )(qidx, widx_rows, qlat, kidx, ckvn, ckvt, wuvt)


def _pack_w_in(w):
    pts = [0]
    for wd in IN_WIDTHS:
        pts.append(pts[-1] + wd)
    seg = [w[..., pts[i]:pts[i + 1]] for i in range(len(IN_WIDTHS))]
    (gq, gk, gv, gz, ga, gb, cq, ckv, ik, iw, mq, mk, mv, mo, mi, mf) = seg
    zeros = lambda n: jnp.zeros(w.shape[:-1] + (n,), w.dtype)
    idx_blk = jnp.concatenate([ik, iw, zeros(LANES - IDX_DIM - IDX_HEADS)], axis=-1)
    gate_blk = jnp.concatenate([ga, gb, mi, mf, zeros(LANES - 16)], axis=-1)
    out = jnp.concatenate([gq, gk, gv, gz, cq, ckv, idx_blk, mq, mk, mv, mo, gate_blk], axis=-1)
    assert out.shape[-1] == NP
    return out.astype(BF16)


def _lane_rows(vals_at, depth):
    rows = jnp.zeros((depth, LANES), F32)
    for start, v in vals_at:
        rows = rows.at[:, start:start + v.shape[1]].set(v.astype(F32))
    return rows


def kernel(x, attn_norm, w_in, gdn_conv, gdn_a_log, gdn_dt_bias, gdn_out_norm, dsa_q_norm,
           dsa_kv_norm, dsa_w_uq, dsa_w_qidx, dsa_w_uk, dsa_w_uv, mlstm_i_bias, mlstm_f_bias,
           mlstm_out_norm, w_out, ffn_norm, w_gate, w_up, w_down, final_norm):
    b, t, d = x.shape
    depth = w_in.shape[0]
    gdn_chunk, mlstm_chunk = min(128, t), min(64, t)
    tb = min(512, t)
    qb = min(256, t)
    tk = min(512, t)

    w_in_p = _pack_w_in(w_in)
    gdn_par_c = jnp.stack([_lane_rows([(0, gdn_a_log)], depth), _lane_rows([(0, gdn_dt_bias)], depth)],
                          axis=1)
    gdn_par_r = jnp.swapaxes(gdn_par_c[:, :, :16], 1, 2)
    gdn_og = jnp.tile(gdn_out_norm, (1, GDN_HEADS))
    wuk_bd = jnp.zeros((depth, DSA_HEADS * DSA_HEAD_DIM, DSA_HEADS * DSA_KV_RANK), F32)
    for h in range(DSA_HEADS):
        wuk_bd = wuk_bd.at[:, h * DSA_HEAD_DIM:(h + 1) * DSA_HEAD_DIM,
                           h * DSA_KV_RANK:(h + 1) * DSA_KV_RANK].set(jnp.swapaxes(dsa_w_uk[:, h], 1, 2))
    wuk_bd = wuk_bd.astype(BF16)
    wuq_b, wqi_b = dsa_w_uq.astype(BF16), dsa_w_qidx.astype(BF16)
    wuvt_b = jnp.swapaxes(dsa_w_uv, 2, 3).astype(BF16)
    ml_bias_c = _lane_rows([(8, mlstm_i_bias), (12, mlstm_f_bias)], depth)
    ml_og = jnp.tile(mlstm_out_norm, (1, MLSTM_HEADS))
    w_out_b, w_gate_b, w_up_b, w_down_b = (a.astype(BF16) for a in (w_out, w_gate, w_up, w_down))

    x2 = x.reshape(b * t, d)
    for l in range(depth):
        proj2 = _inproj(x2, attn_norm[l].reshape(1, d), w_in_p[l])
        proj3 = proj2.reshape(b, t, NP)
        gates_t = jnp.swapaxes(proj3[:, :, C_GATE:C_GATE + 16], 1, 2)

        def chunk_rows(c):
            return jnp.swapaxes(gates_t.reshape(b, 16, t // c, c), 1, 2)

        y_a = _gdn(proj3, chunk_rows(gdn_chunk), gdn_conv[l], gdn_par_c[l], gdn_par_r[l],
                   gdn_og[l].reshape(1, -1), chunk=gdn_chunk, tb=tb)

        qlat, qidx, ckvn, ckvt, kidx = _dsa_prep(
            proj2, dsa_q_norm[l].reshape(1, -1), dsa_kv_norm[l].reshape(1, -1),
            wuq_b[l], wqi_b[l], wuk_bd[l])
        r3 = lambda a: a.reshape(b, t, a.shape[-1])
        widx_rows = jnp.swapaxes(proj3[:, :, C_IDX + IDX_DIM:C_IDX + IDX_DIM + IDX_HEADS], 1, 2)
        y_b = _dsa(r3(qidx), widx_rows, r3(qlat), r3(kidx), r3(ckvn), ckvt, wuvt_b[l], qb=qb, tk=tk)

        y_c = _mlstm(proj3, chunk_rows(mlstm_chunk), ml_bias_c[l].reshape(1, LANES),
                     ml_bias_c[l, :16].reshape(16, 1), ml_og[l].reshape(1, -1),
                     chunk=mlstm_chunk, tb=tb)

        x2 = _out_ffn(x2, y_a.reshape(b * t, -1), y_b.reshape(b * t, -1), y_c.reshape(b * t, -1),
                      w_out_b[l], ffn_norm[l].reshape(1, d), w_gate_b[l], w_up_b[l], w_down_b[l],
                      final_norm.reshape(1, d), final_norm=(l == depth - 1))
    return x2.reshape(b, t, d)
```

```python
import functools
import math

import jax
import jax.numpy as jnp
from jax import lax
from jax.experimental import pallas as pl
from jax.experimental.pallas import tpu as pltpu

F32 = jnp.float32
BF16 = jnp.bfloat16
HIGHEST = lax.Precision.HIGHEST

D_MODEL = 1024
GDN_HEADS = 4
GDN_DK = 128
GDN_DV = 128
GDN_CONV = 4
DSA_HEADS = 4
DSA_HEAD_DIM = 64
DSA_Q_RANK = 256
DSA_KV_RANK = 128
IDX_HEADS = 8
IDX_DIM = 32
IDX_TOPK = 256
MLSTM_HEADS = 4
MLSTM_DQK = 64
MLSTM_DV = 64
FFN_HIDDEN = 2816
NORM_EPS = 1e-6

IN_WIDTHS = (512, 512, 512, 512, 4, 4, 256, 128, 32, 8, 256, 256, 256, 256, 4, 4)

C_GQ, C_GK, C_GV, C_GZ = 0, 512, 1024, 1536
C_CQ, C_CKV, C_IDX = 2048, 2304, 2432
C_MQ, C_MK, C_MV, C_MO = 2560, 2816, 3072, 3328
C_GATE = 3584
NP = 3712
LANES = 128

VMEM_LIMIT = 56 * 1024 * 1024

INT_MIN = -2 ** 31


def _mm(a, b):
    return jnp.dot(a.astype(BF16), b.astype(BF16), preferred_element_type=F32)


def _mm_nt(a, b):
    return lax.dot_general(a.astype(BF16), b.astype(BF16), (((1,), (1,)), ((), ())),
                           preferred_element_type=F32)


def _sigmoid(x):
    return 1.0 / (1.0 + jnp.exp(-x))


def _softplus(x):
    return jnp.maximum(x, 0.0) + jnp.log1p(jnp.exp(-jnp.abs(x)))


def _tri(n, kind):
    r = lax.broadcasted_iota(jnp.int32, (n, n), 0)
    c = lax.broadcasted_iota(jnp.int32, (n, n), 1)
    if kind == "incl":
        return r >= c
    if kind == "strict":
        return r > c
    if kind == "upper_incl":
        return r <= c
    raise ValueError(kind)


def _inproj_kernel(x_ref, g_ref, w_ref, o_ref):
    x = x_ref[...]
    ms = jnp.mean(x * x, axis=-1, keepdims=True)
    h = (x * lax.rsqrt(ms + NORM_EPS)) * g_ref[...]
    o_ref[...] = jnp.dot(h.astype(BF16), w_ref[...], preferred_element_type=F32)


def _inproj(x2, g, w):
    m = x2.shape[0]
    tm = 256
    return pl.pallas_call(
        _inproj_kernel,
        grid=(m // tm,),
        in_specs=[
            pl.BlockSpec((tm, D_MODEL), lambda i: (i, 0)),
            pl.BlockSpec((1, D_MODEL), lambda i: (0, 0)),
            pl.BlockSpec((D_MODEL, NP), lambda i: (0, 0)),
        ],
        out_specs=pl.BlockSpec((tm, NP), lambda i: (i, 0)),
        out_shape=jax.ShapeDtypeStruct((m, NP), F32),
        compiler_params=pltpu.CompilerParams(
            dimension_semantics=("arbitrary",), vmem_limit_bytes=VMEM_LIMIT),
        name="inproj",
    )(x2, g, w)


def _ffn_kernel(x_ref, ya_ref, yb_ref, yc_ref, wo_ref, g_ref, wg_ref, wu_ref, wd_ref, fg_ref,
                o_ref, acc_ref, h_ref, *, n_hidden_steps, final_norm):
    j = pl.program_id(1)

    @pl.when(j == 0)
    def _():
        xn = x_ref[...]
        xn = xn + jnp.dot(ya_ref[...], wo_ref[0:512, :], preferred_element_type=F32)
        xn = xn + jnp.dot(yb_ref[...], wo_ref[512:768, :], preferred_element_type=F32)
        xn = xn + jnp.dot(yc_ref[...], wo_ref[768:1024, :], preferred_element_type=F32)
        acc_ref[...] = xn
        ms = jnp.mean(xn * xn, axis=-1, keepdims=True)
        h_ref[...] = ((xn * lax.rsqrt(ms + NORM_EPS)) * g_ref[...]).astype(BF16)

    h = h_ref[...]
    gate = jnp.dot(h, wg_ref[...], preferred_element_type=F32)
    up = jnp.dot(h, wu_ref[...], preferred_element_type=F32)
    act = (gate * _sigmoid(gate)) * up
    acc_ref[...] += jnp.dot(act.astype(BF16), wd_ref[...], preferred_element_type=F32)

    @pl.when(j == n_hidden_steps - 1)
    def _():
        y = acc_ref[...]
        if final_norm:
            ms = jnp.mean(y * y, axis=-1, keepdims=True)
            y = (y * lax.rsqrt(ms + NORM_EPS)) * fg_ref[...]
        o_ref[...] = y


def _out_ffn(x2, ya, yb, yc, wo, g, wg, wu, wd, fg, final_norm):
    m = x2.shape[0]
    tm = 512
    th = 2816
    nh = FFN_HIDDEN // th
    kern = functools.partial(_ffn_kernel, n_hidden_steps=nh, final_norm=final_norm)
    return pl.pallas_call(
        kern,
        grid=(m // tm, nh),
        in_specs=[
            pl.BlockSpec((tm, D_MODEL), lambda i, j: (i, 0)),
            pl.BlockSpec((tm, 512), lambda i, j: (i, 0)),
            pl.BlockSpec((tm, 256), lambda i, j: (i, 0)),
            pl.BlockSpec((tm, 256), lambda i, j: (i, 0)),
            pl.BlockSpec((D_MODEL, D_MODEL), lambda i, j: (0, 0), pipeline_mode=pl.Buffered(1)),
            pl.BlockSpec((1, D_MODEL), lambda i, j: (0, 0)),
            pl.BlockSpec((D_MODEL, th), lambda i, j: (0, j), pipeline_mode=pl.Buffered(1)),
            pl.BlockSpec((D_MODEL, th), lambda i, j: (0, j), pipeline_mode=pl.Buffered(1)),
            pl.BlockSpec((th, D_MODEL), lambda i, j: (j, 0), pipeline_mode=pl.Buffered(1)),
            pl.BlockSpec((1, D_MODEL), lambda i, j: (0, 0)),
        ],
        out_specs=pl.BlockSpec((tm, D_MODEL), lambda i, j: (i, 0)),
        out_shape=jax.ShapeDtypeStruct((m, D_MODEL), F32),
        scratch_shapes=[pltpu.VMEM((tm, D_MODEL), F32), pltpu.VMEM((tm, D_MODEL), BF16)],
        compiler_params=pltpu.CompilerParams(
            dimension_semantics=("arbitrary", "arbitrary"), vmem_limit_bytes=VMEM_LIMIT),
        name="out_ffn",
    )(x2, ya, yb, yc, wo, g, wg, wu, wd, fg)


def _mlstm_kernel(q_ref, k_ref, v_ref, o_ref, gc_ref, gr_ref, bc_ref, br_ref, og_ref,
                  y_ref, sv_s, kv_s, bcol_s, dm_s, km_s, c_s, m_s, *, chunk, n_chunks, group):
    L = chunk
    t = pl.program_id(1)
    heads = range(MLSTM_HEADS)

    @pl.when(t == 0)
    def _():
        c_s[...] = jnp.zeros_like(c_s)
        m_s[...] = jnp.zeros_like(m_s)

    incl = _tri(L, "incl")
    tril_b = incl.astype(BF16)
    triu_b = _tri(L, "upper_incl").astype(BF16)
    e0 = (lax.broadcasted_iota(jnp.int32, (L, MLSTM_DV), 1) == 0).astype(F32)
    lane_l = lax.broadcasted_iota(jnp.int32, (L, LANES), 1)
    lane_8 = lax.broadcasted_iota(jnp.int32, (8, LANES), 1)
    scale = MLSTM_DQK ** -0.5

    def prep(cg, carry):
        ch = []
        gate_tiles = []
        for g in range(group):
            c = cg * group + g
            r0 = pl.multiple_of(c * L, L)
            x_col = gc_ref[pl.ds(r0, L), :] + bc_ref[...]
            x_row = gr_ref[c] + br_ref[...]
            b_col = _cumsum_rows(tril_b, -_softplus(-x_col))
            b_row = _cumsum_lanes(-_softplus(-x_row), triu_b)
            dm_tile = jnp.zeros((L, LANES), F32)
            km_tile = jnp.zeros((8, LANES), F32)
            for h in heads:
                sl = slice(h * MLSTM_DQK, (h + 1) * MLSTM_DQK)
                bx_c = b_col[:, 12 + h:13 + h]
                d_log = jnp.where(incl, bx_c - b_row[12 + h:13 + h, :] + x_row[8 + h:9 + h, :], -jnp.inf)
                dmax = jnp.max(d_log, axis=-1, keepdims=True)
                kmax = dmax[L - 1:L, :]
                kw = jnp.exp(bx_c[L - 1:L, :] - bx_c + x_col[:, 8 + h:9 + h] - kmax)
                k = k_ref[pl.ds(r0, L), sl] * scale
                v_aug = jnp.concatenate([v_ref[pl.ds(r0, L), sl], e0], axis=1)
                ch.append(dict(idx=c * MLSTM_HEADS + h, q=q_ref[pl.ds(r0, L), sl], k=k, v=v_aug,
                               dw=jnp.exp(d_log - dmax), kxw_t=(k * kw).T))
                dm_tile = jnp.where(lane_l == h, dmax, dm_tile)
                km_tile = jnp.where(lane_8 == h, kmax, km_tile)
            gate_tiles.append((c, b_col, dm_tile, km_tile))
        for d in ch:
            d["s"] = _mm_nt(d["q"], d["k"]) * d["dw"]
        for d in ch:
            d["sv"] = _mm(d["s"], d["v"])
        for d in ch:
            d["kv"] = _mm(d["kxw_t"], d["v"])
        for d in ch:
            sv_s[d["idx"]] = d["sv"]
            kv_s[d["idx"]] = d["kv"]
        for c, b_col, dm_tile, km_tile in gate_tiles:
            bcol_s[c] = b_col
            dm_s[c] = dm_tile
            km_s[c] = km_tile
        return carry

    lax.fori_loop(0, n_chunks // group, prep, 0)

    def scan(c, carry):
        r0 = pl.multiple_of(c * L, L)
        b_col = bcol_s[c]
        dm = dm_s[c]
        km = km_s[c]
        c_prev = [c_s[h] for h in heads]
        qc = [_mm(q_ref[pl.ds(r0, L), h * MLSTM_DQK:(h + 1) * MLSTM_DQK], c_prev[h]) for h in heads]
        outs = []
        for h in heads:
            idx = c * MLSTM_HEADS + h
            bx_c = b_col[:, 12 + h:13 + h]
            dmax = dm[:, h:h + 1]
            kmax = km[0:1, h:h + 1]
            m_prev = m_s[h:h + 1, 0:1]
            a_log = bx_c + m_prev
            m_t = jnp.maximum(a_log, dmax)
            num = jnp.exp(a_log - m_t) * qc[h] + jnp.exp(dmax - m_t) * sv_s[idx]
            den = num[:, MLSTM_DV:MLSTM_DV + 1]
            hh = num[:, :MLSTM_DV] / jnp.maximum(jnp.abs(den), jnp.exp(-m_t))
            m_new = m_t[L - 1:L, :]
            dec = jnp.exp(bx_c[L - 1:L, :] + m_prev - m_new)
            c_s[h] = dec * c_prev[h] + jnp.exp(kmax - m_new) * kv_s[idx]
            m_s[h:h + 1, :] = jnp.broadcast_to(m_new, (1, LANES))
            ms = jnp.mean(hh * hh, axis=-1, keepdims=True)
            outs.append(hh * lax.rsqrt(ms + NORM_EPS))
        y = jnp.concatenate(outs, axis=1) * og_ref[...]
        y = y * _sigmoid(o_ref[pl.ds(r0, L), :])
        y_ref[pl.ds(r0, L), :] = y.astype(y_ref.dtype)
        return carry

    lax.fori_loop(0, n_chunks, scan, 0)


def _mlstm(proj3, gates_r, bias_c, bias_r, og, *, chunk, tb):
    b, t, _ = proj3.shape
    nc = tb // chunk
    nch = nc * MLSTM_HEADS
    wq = MLSTM_HEADS * MLSTM_DQK
    group = 2 if nc % 2 == 0 else 1
    kern = functools.partial(_mlstm_kernel, chunk=chunk, n_chunks=nc, group=group)
    col = lambda c0: pl.BlockSpec((None, tb, wq), lambda i, j, c0=c0: (i, j, c0 // wq))
    return pl.pallas_call(
        kern,
        grid=(b, t // tb),
        in_specs=[
            col(C_MQ), col(C_MK), col(C_MV), col(C_MO),
            pl.BlockSpec((None, tb, LANES), lambda i, j: (i, j, C_GATE // LANES)),
            pl.BlockSpec((None, nc, 16, chunk), lambda i, j: (i, j, 0, 0)),
            pl.BlockSpec((1, LANES), lambda i, j: (0, 0)),
            pl.BlockSpec((16, 1), lambda i, j: (0, 0)),
            pl.BlockSpec((1, wq), lambda i, j: (0, 0)),
        ],
        out_specs=pl.BlockSpec((None, tb, wq), lambda i, j: (i, j, 0)),
        out_shape=jax.ShapeDtypeStruct((b, t, wq), BF16),
        scratch_shapes=[pltpu.VMEM((nch, chunk, LANES), F32),
                        pltpu.VMEM((nch, MLSTM_DQK, LANES), F32),
                        pltpu.VMEM((nc, chunk, LANES), F32),
                        pltpu.VMEM((nc, chunk, LANES), F32),
                        pltpu.VMEM((nc, 8, LANES), F32),
                        pltpu.VMEM((MLSTM_HEADS, MLSTM_DQK, LANES), F32),
                        pltpu.VMEM((8, LANES), F32)],
        compiler_params=pltpu.CompilerParams(
            dimension_semantics=("arbitrary", "arbitrary"), vmem_limit_bytes=VMEM_LIMIT),
        name="mlstm",
    )(proj3, proj3, proj3, proj3, proj3, gates_r, bias_c, bias_r, og)


def _split3(x):
    x1 = x.astype(BF16)
    r1 = x - x1.astype(F32)
    x2 = r1.astype(BF16)
    x3 = (r1 - x2.astype(F32)).astype(BF16)
    return x1, x2, x3


def _cumsum_rows(tril_b, x):
    return sum(jnp.dot(tril_b, p, preferred_element_type=F32) for p in _split3(x))


def _cumsum_lanes(x, triu_b):
    return sum(jnp.dot(p, triu_b, preferred_element_type=F32) for p in _split3(x))


def _inverse_masks(n):
    r = lax.broadcasted_iota(jnp.int32, (n, n), 0)
    c = lax.broadcasted_iota(jnp.int32, (n, n), 1)
    offs = []
    lvl = 0
    while (1 << lvl) < n:
        rb = r >> lvl
        cb = c >> lvl
        offs.append(((rb & 1) == 1) & (cb == rb - 1))
        lvl += 1
    return (r == c).astype(F32), offs


def _gdn_kernel(q_ref, k_ref, v_ref, z_ref, qh_ref, kh_ref, vh_ref, gc_ref, gr_ref, cw_ref,
                pc_ref, pr_ref, og_ref, y_ref, qs, ks, vs, xp_s, u_s, w_s, qk_s, qd_s, kdt_s, el_s, s_s,
                *, chunk, n_chunks, tb, group):
    C = chunk
    t = pl.program_id(1)
    width = GDN_HEADS * GDN_DK

    @pl.when(t == 0)
    def _():
        s_s[...] = jnp.zeros_like(s_s)

    def conv_silu(x_ref, halo_ref, w0):
        xp_s[0:8, :] = jnp.where(t == 0, 0.0, halo_ref[...])
        xp_s[8:, :] = x_ref[...]
        y = xp_s[8:, :] * cw_ref[GDN_CONV - 1:GDN_CONV, w0:w0 + width]
        for j in range(GDN_CONV - 1):
            y = y + xp_s[pl.ds(8 - (GDN_CONV - 1) + j, tb), :] * cw_ref[j:j + 1, w0:w0 + width]
        return y * _sigmoid(y)

    def l2n(x):
        parts = []
        for h in range(GDN_HEADS):
            xh = x[:, h * GDN_DK:(h + 1) * GDN_DK]
            parts.append(xh * lax.rsqrt(jnp.sum(xh * xh, axis=-1, keepdims=True) + NORM_EPS))
        return jnp.concatenate(parts, axis=1)

    qs[...] = l2n(conv_silu(q_ref, qh_ref, 0)) * (GDN_DK ** -0.5)
    ks[...] = l2n(conv_silu(k_ref, kh_ref, width))
    vs[...] = conv_silu(v_ref, vh_ref, 2 * width)

    incl = _tri(C, "incl")
    strict = _tri(C, "strict")
    tril_b = incl.astype(BF16)
    triu_b = _tri(C, "upper_incl").astype(BF16)
    inv_masks = _inverse_masks(C)

    def prep(cg, carry):
        loaded = []
        for g in range(group):
            c = cg * group + g
            r0 = pl.multiple_of(c * C, C)
            gcol = gc_ref[pl.ds(r0, C), :]
            grow = gr_ref[c]
            qkv = [(qs[pl.ds(r0, C), h * GDN_DK:(h + 1) * GDN_DK],
                    ks[pl.ds(r0, C), h * GDN_DK:(h + 1) * GDN_DK],
                    vs[pl.ds(r0, C), h * GDN_DK:(h + 1) * GDN_DK]) for h in range(GDN_HEADS)]
            loaded.append((c, gcol, grow, qkv))
        ch = []
        for c, gcol, grow, qkv in loaded:
            g_col = -jnp.exp(pc_ref[0:1, :]) * _softplus(gcol + pc_ref[1:2, :])
            gcum_col = _cumsum_rows(tril_b, g_col)
            beta_col = _sigmoid(gcol)
            g_row = -jnp.exp(pr_ref[:, 0:1]) * _softplus(grow + pr_ref[:, 1:2])
            gcum_row = _cumsum_lanes(g_row, triu_b)
            for h in range(GDN_HEADS):
                q, k, v = qkv[h]
                gc_c = gcum_col[:, h:h + 1]
                gc_r = gcum_row[h:h + 1, :]
                beta = beta_col[:, 4 + h:5 + h]
                decay = jnp.where(incl, jnp.exp(jnp.where(incl, gc_c - gc_r, 0.0)), 0.0)
                ch.append(dict(idx=c * GDN_HEADS + h, q=q, k=k, v=v, gc_c=gc_c, beta=beta,
                               decay=decay, kb=k * beta))
        for d in ch:
            d["a"] = jnp.where(strict, _mm_nt(d["kb"], d["k"]) * d["decay"], 0.0)
        eye, offs = inv_masks
        for d in ch:
            d["t"] = eye - jnp.where(offs[0], d["a"], 0.0)
        for off in offs[1:]:
            for d in ch:
                d["p"] = _mm(d["t"], jnp.where(off, d["a"], 0.0))
            for d in ch:
                d["t"] = d["t"] - _mm(d["p"], d["t"])
        for d in ch:
            d["egc"] = jnp.exp(d["gc_c"])
            d["u"] = _mm(d["t"], d["v"] * d["beta"])
        for d in ch:
            d["w"] = _mm(d["t"], d["kb"] * d["egc"]).astype(BF16)
        for d in ch:
            d["qk"] = (_mm_nt(d["q"], d["k"]) * d["decay"]).astype(BF16)
        for d in ch:
            idx = d["idx"]
            g_last = d["gc_c"][C - 1:C, :]
            u_s[idx] = d["u"]
            w_s[idx] = d["w"]
            qk_s[idx] = d["qk"]
            qd_s[idx] = (d["q"] * d["egc"]).astype(BF16)
            kdt_s[idx] = (d["k"] * jnp.exp(g_last - d["gc_c"])).T.astype(BF16)
            el_s[idx] = jnp.broadcast_to(jnp.exp(g_last), (8, LANES))
        return carry

    lax.fori_loop(0, n_chunks // group, prep, 0)

    def scan(c, carry):
        r0 = pl.multiple_of(c * C, C)
        heads = range(GDN_HEADS)
        dot = functools.partial(jnp.dot, preferred_element_type=F32)
        idx = [c * GDN_HEADS + h for h in heads]
        s_prev = [s_s[h] for h in heads]
        s_b = [s.astype(BF16) for s in s_prev]
        ws = [dot(w_s[idx[h]], s_b[h]) for h in heads]
        qs_ = [dot(qd_s[idx[h]], s_b[h]) for h in heads]
        v_b = [(u_s[idx[h]] - ws[h]).astype(BF16) for h in heads]
        upd = [dot(kdt_s[idx[h]], v_b[h]) for h in heads]
        o_all = [qs_[h] + dot(qk_s[idx[h]], v_b[h]) for h in heads]
        outs = []
        for h in heads:
            s_s[h] = s_prev[h] * el_s[idx[h]][0:1, :] + upd[h]
            o = o_all[h]
            ms = jnp.mean(o * o, axis=-1, keepdims=True)
            outs.append(o * lax.rsqrt(ms + NORM_EPS))
        y = jnp.concatenate(outs, axis=1) * og_ref[...]
        z = z_ref[pl.ds(r0, C), :]
        y = y * (z * _sigmoid(z))
        y_ref[pl.ds(r0, C), :] = y.astype(y_ref.dtype)
        return carry

    lax.fori_loop(0, n_chunks, scan, 0)


def _gdn(proj3, gates_r, conv_w, par_c, par_r, og, *, chunk, tb):
    b, t, _ = proj3.shape
    nc = tb // chunk
    nch = nc * GDN_HEADS
    width = GDN_HEADS * GDN_DK
    group = 2 if nc % 2 == 0 else 1
    kern = functools.partial(_gdn_kernel, chunk=chunk, n_chunks=nc, tb=tb, group=group)
    col = lambda c0: pl.BlockSpec((None, tb, width), lambda i, j, c0=c0: (i, j, c0 // width))
    halo = lambda c0: pl.BlockSpec(
        (None, 8, width), lambda i, j, c0=c0: (i, jnp.maximum(j * (tb // 8) - 1, 0), c0 // width))
    return pl.pallas_call(
        kern,
        grid=(b, t // tb),
        in_specs=[
            col(C_GQ), col(C_GK), col(C_GV), col(C_GZ),
            halo(C_GQ), halo(C_GK), halo(C_GV),
            pl.BlockSpec((None, tb, LANES), lambda i, j: (i, j, C_GATE // LANES)),
            pl.BlockSpec((None, nc, 16, chunk), lambda i, j: (i, j, 0, 0)),
            pl.BlockSpec((GDN_CONV, 3 * width), lambda i, j: (0, 0)),
            pl.BlockSpec((2, LANES), lambda i, j: (0, 0)),
            pl.BlockSpec((16, 2), lambda i, j: (0, 0)),
            pl.BlockSpec((1, width), lambda i, j: (0, 0)),
        ],
        out_specs=pl.BlockSpec((None, tb, width), lambda i, j: (i, j, 0)),
        out_shape=jax.ShapeDtypeStruct((b, t, width), BF16),
        scratch_shapes=[pltpu.VMEM((tb, width), F32), pltpu.VMEM((tb, width), F32),
                        pltpu.VMEM((tb, width), F32),
                        pltpu.VMEM((tb + 8, width), F32),
                        pltpu.VMEM((nch, chunk, GDN_DV), F32),
                        pltpu.VMEM((nch, chunk, GDN_DK), BF16),
                        pltpu.VMEM((nch, chunk, chunk), BF16),
                        pltpu.VMEM((nch, chunk, GDN_DK), BF16),
                        pltpu.VMEM((nch, GDN_DK, chunk), BF16),
                        pltpu.VMEM((nch, 8, LANES), F32),
                        pltpu.VMEM((GDN_HEADS, GDN_DK, GDN_DV), F32)],
        compiler_params=pltpu.CompilerParams(
            dimension_semantics=("arbitrary", "arbitrary"), vmem_limit_bytes=VMEM_LIMIT),
        name="gdn",
    )(proj3, proj3, proj3, proj3, proj3, proj3, proj3, proj3, gates_r, conv_w, par_c, par_r, og)


def _dsa_prep_kernel(cq_ref, ckv_ref, idx_ref, qn_ref, kn_ref, wuq_ref, wqi_ref, wuk_ref,
                     qlat_ref, qidx_ref, ckvn_ref, ckvt_ref, kidx_ref):
    cq = cq_ref[...]
    ms = jnp.mean(cq * cq, axis=-1, keepdims=True)
    cqn = ((cq * lax.rsqrt(ms + NORM_EPS)) * qn_ref[...]).astype(BF16)
    ckv = ckv_ref[...]
    ms = jnp.mean(ckv * ckv, axis=-1, keepdims=True)
    ckvn = (ckv * lax.rsqrt(ms + NORM_EPS)) * kn_ref[...]
    ckvn_ref[...] = ckvn.astype(BF16)
    ckvt_ref[...] = ckvn.T.astype(BF16)
    q = jnp.dot(cqn, wuq_ref[...], preferred_element_type=F32)
    qlat = jnp.dot(q.astype(BF16), wuk_ref[...], preferred_element_type=F32) * (DSA_HEAD_DIM ** -0.5)
    qlat_ref[...] = qlat.astype(BF16)
    qidx_ref[...] = jnp.dot(cqn, wqi_ref[...], preferred_element_type=F32).astype(BF16)
    kidx_ref[...] = idx_ref[:, 0:IDX_DIM].astype(BF16)


def _dsa_prep(proj2, qn, kn, wuq, wqi, wuk_bd):
    m = proj2.shape[0]
    tm = 512
    full = lambda shape: pl.BlockSpec(shape, lambda i: (0, 0))
    return pl.pallas_call(
        _dsa_prep_kernel,
        grid=(m // tm,),
        in_specs=[
            pl.BlockSpec((tm, DSA_Q_RANK), lambda i: (i, C_CQ // DSA_Q_RANK)),
            pl.BlockSpec((tm, DSA_KV_RANK), lambda i: (i, C_CKV // DSA_KV_RANK)),
            pl.BlockSpec((tm, LANES), lambda i: (i, C_IDX // LANES)),
            full((1, DSA_Q_RANK)), full((1, DSA_KV_RANK)),
            full((DSA_Q_RANK, DSA_HEADS * DSA_HEAD_DIM)),
            full((DSA_Q_RANK, IDX_HEADS * IDX_DIM)),
            full((DSA_HEADS * DSA_HEAD_DIM, DSA_HEADS * DSA_KV_RANK)),
        ],
        out_specs=[
            pl.BlockSpec((tm, DSA_HEADS * DSA_KV_RANK), lambda i: (i, 0)),
            pl.BlockSpec((tm, IDX_HEADS * IDX_DIM), lambda i: (i, 0)),
            pl.BlockSpec((tm, DSA_KV_RANK), lambda i: (i, 0)),
            pl.BlockSpec((DSA_KV_RANK, tm), lambda i: (0, i)),
            pl.BlockSpec((tm, IDX_DIM), lambda i: (i, 0)),
        ],
        out_shape=[
            jax.ShapeDtypeStruct((m, DSA_HEADS * DSA_KV_RANK), BF16),
            jax.ShapeDtypeStruct((m, IDX_HEADS * IDX_DIM), BF16),
            jax.ShapeDtypeStruct((m, DSA_KV_RANK), BF16),
            jax.ShapeDtypeStruct((DSA_KV_RANK, m), BF16),
            jax.ShapeDtypeStruct((m, IDX_DIM), BF16),
        ],
        compiler_params=pltpu.CompilerParams(
            dimension_semantics=("arbitrary",), vmem_limit_bytes=VMEM_LIMIT),
        name="dsa_prep",
    )(proj2, proj2, proj2, qn, kn, wuq, wqi, wuk_bd)


def _tree_sum(parts):
    while len(parts) > 1:
        parts = [parts[j] + parts[j + 1] for j in range(0, len(parts) - 1, 2)] + (
            [parts[-1]] if len(parts) % 2 else [])
    return parts[0]


def _dsa_kernel(qidx_ref, wrow_ref, qlat_ref, kidx_ref, ckv_ref, ckvt_ref, wuvt_ref, y_ref,
                key_s, hi_s, lo_s, acc_s, *, qb, tk, n_sel):
    i = pl.program_id(1)
    n_tiles = (i * qb + qb + tk - 1) // tk
    key_pos = lax.broadcasted_iota(jnp.int32, (tk, qb), 0)
    q_pos = i * qb + lax.broadcasted_iota(jnp.int32, (tk, qb), 1)
    w_rows = wrow_ref[...] * (IDX_HEADS ** -0.5 * IDX_DIM ** -0.5)
    qidx = qidx_ref[...]
    q_heads = [qidx[:, h * IDX_DIM:(h + 1) * IDX_DIM] for h in range(IDX_HEADS)]
    i16_min = -(1 << 15)
    i16_max = (1 << 15) - 1
    sub16 = 16
    nt_dims = (((1,), (1,)), ((), ()))

    def score_body(kt, carry):
        k0 = pl.multiple_of(kt * tk, tk)
        kk = kidx_ref[pl.ds(k0, tk), :]
        sh = [lax.dot_general(kk, q_heads[h], nt_dims, preferred_element_type=F32)
              for h in range(IDX_HEADS)]
        acc = jnp.zeros((tk, qb), F32)
        for h in range(IDX_HEADS):
            acc = acc + jnp.maximum(sh[h], 0.0) * w_rows[h:h + 1, :]
        acc = acc + 0.0
        bits = pltpu.bitcast(acc, jnp.int32)
        keys = jnp.where(bits < 0, bits ^ jnp.int32(0x7FFFFFFF), bits)
        keys = jnp.where(k0 + key_pos <= q_pos, keys, jnp.int32(INT_MIN))
        key_s[kt] = keys
        hi_s[kt] = (keys >> 16).astype(jnp.int16)
        return carry

    lax.fori_loop(0, n_tiles, score_body, 0)

    def count_ge(src_s, cand):
        cand16 = jnp.broadcast_to(cand, (sub16, qb)).astype(jnp.int16)
        one, zero = jnp.int16(1), jnp.int16(0)

        def body(kt, acc):
            tile = src_s[kt]
            return acc + _tree_sum([jnp.where(tile[j * sub16:(j + 1) * sub16, :] >= cand16, one, zero)
                                    for j in range(tk // sub16)])

        cnt = lax.fori_loop(0, n_tiles, body, jnp.zeros((sub16, qb), jnp.int16))
        return jnp.sum(cnt.astype(jnp.int32), axis=0, keepdims=True)

    def kth_largest16(src_s, rank):
        thr = jnp.full((1, qb), i16_min, jnp.int32)
        for bit in range(15, -1, -1):
            cand = thr + jnp.int32(1 << bit)
            thr = jnp.where(count_ge(src_s, cand) >= rank, cand, thr)
        return thr

    hi_thr = kth_largest16(hi_s, jnp.int32(n_sel))
    above = count_ge(hi_s, jnp.minimum(hi_thr + 1, jnp.int32(i16_max)))
    rank_lo = jnp.int32(n_sel) - jnp.where(hi_thr < i16_max, above, 0)

    def lo_body(kt, carry):
        keys = key_s[kt]
        lo = (keys & jnp.int32(0xFFFF)) + jnp.int32(i16_min)
        lo_s[kt] = jnp.where((keys >> 16) == hi_thr, lo, jnp.int32(i16_min)).astype(jnp.int16)
        return carry

    lax.fori_loop(0, n_tiles, lo_body, 0)
    lo_thr = kth_largest16(lo_s, rank_lo)
    thr_raw = hi_thr * jnp.int32(1 << 16) + (lo_thr - jnp.int32(i16_min))
    thr = jnp.maximum(thr_raw, jnp.int32(INT_MIN + 1))

    def selected_body(kt, acc):
        return acc + jnp.sum(jnp.where(key_s[kt] >= thr_raw, 1, 0), axis=0, keepdims=True)

    n_selected = lax.fori_loop(0, n_tiles, selected_body, jnp.zeros((1, qb), jnp.int32))
    excess = jnp.where(thr_raw > INT_MIN, n_selected - jnp.int32(n_sel), 0)

    @pl.when(jnp.max(excess) > 0)
    def _():
        last = tk * key_s.shape[0] - 1
        assert last <= i16_max

        def fill(kt, carry):
            rev = jnp.int32(last) - (kt * tk + key_pos)
            lo_s[kt] = jnp.where(key_s[kt] == thr_raw, rev, jnp.int32(-1)).astype(jnp.int16)
            return carry

        lax.fori_loop(0, n_tiles, fill, 0)
        keep = count_ge(lo_s, jnp.zeros((1, qb), jnp.int32)) - excess
        cut = jnp.zeros((1, qb), jnp.int32)
        for bit in range(max(last, 1).bit_length() - 1, -1, -1):
            cand = cut + jnp.int32(1 << bit)
            cut = jnp.where(count_ge(lo_s, cand) >= keep, cand, cut)
        cut = jnp.where(excess > 0, cut, 0)

        def demote(kt, carry):
            keys = key_s[kt]
            rev = jnp.int32(last) - (kt * tk + key_pos)
            demoted = jnp.where(rev < cut, jnp.int32(INT_MIN), keys)
            key_s[kt] = jnp.where(keys == thr_raw, demoted, keys)
            return carry

        lax.fori_loop(0, n_tiles, demote, 0)

    qlat = qlat_ref[...]
    heads = range(DSA_HEADS)
    ql = [qlat[:, h * DSA_KV_RANK:(h + 1) * DSA_KV_RANK] for h in heads]
    acc_s[...] = jnp.zeros_like(acc_s)

    def attn_body(kt, carry):
        m_run, l_run = carry
        k0 = pl.multiple_of(kt * tk, tk)
        ckv = ckv_ref[pl.ds(k0, tk), :]
        ckvt = ckvt_ref[:, pl.ds(k0, tk)]
        bias = jnp.where(key_s[kt] >= thr, 0.0, -jnp.inf)
        lg = [lax.dot_general(ckv, ql[h], nt_dims, preferred_element_type=F32) for h in heads]
        m_out, l_out, p_all, alphas = [], [], [], []
        for h in heads:
            lgm = lg[h] + bias
            m_new = jnp.maximum(m_run[h], jnp.max(lgm, axis=0, keepdims=True))
            p = jnp.exp(lgm - m_new)
            alpha = jnp.exp(m_run[h] - m_new)
            l_out.append(alpha * l_run[h] + jnp.sum(p, axis=0, keepdims=True))
            m_out.append(m_new)
            p_all.append(p.astype(BF16))
            alphas.append(alpha)
        pv = [jnp.dot(ckvt, p_all[h], preferred_element_type=F32) for h in heads]
        for h in heads:
            acc_s[h] = alphas[h] * acc_s[h] + pv[h]
        return tuple(m_out), tuple(l_out)

    init = (tuple(jnp.full((1, qb), -1e30, F32) for _ in heads),
            tuple(jnp.zeros((1, qb), F32) for _ in heads))
    _, l_fin = lax.fori_loop(0, n_tiles, attn_body, init)
    o_lat = [(acc_s[h] / l_fin[h]).astype(BF16) for h in heads]
    outs = [jnp.dot(wuvt_ref[h], o_lat[h], preferred_element_type=F32) for h in heads]
    y_ref[...] = jnp.concatenate(outs, axis=0).T.astype(y_ref.dtype)


def _dsa(qidx, widx_rows, qlat, kidx, ckvn, ckvt, wuvt, *, qb, tk):
    b, t, _ = qidx.shape
    n_sel = min(IDX_TOPK, t // 4)
    kern = functools.partial(_dsa_kernel, qb=qb, tk=tk, n_sel=n_sel)
    return pl.pallas_call(
        kern,
        grid=(b, t // qb),
        in_specs=[
            pl.BlockSpec((None, qb, IDX_HEADS * IDX_DIM), lambda i, j: (i, j, 0)),
            pl.BlockSpec((None, IDX_HEADS, qb), lambda i, j: (i, 0, j)),
            pl.BlockSpec((None, qb, DSA_HEADS * DSA_KV_RANK), lambda i, j: (i, j, 0)),
            pl.BlockSpec((None, t, IDX_DIM), lambda i, j: (i, 0, 0)),
            pl.BlockSpec((None, t, DSA_KV_RANK), lambda i, j: (i, 0, 0)),
            pl.BlockSpec((DSA_KV_RANK, t), lambda i, j: (0, i)),
            pl.BlockSpec((DSA_HEADS, DSA_HEAD_DIM, DSA_KV_RANK), lambda i, j: (0, 0, 0)),
        ],
        out_specs=pl.BlockSpec((None, qb, DSA_HEADS * DSA_HEAD_DIM), lambda i, j: (i, j, 0)),
        out_shape=jax.ShapeDtypeStruct((b, t, DSA_HEADS * DSA_HEAD_DIM), BF16),
        scratch_shapes=[pltpu.VMEM((t // tk, tk, qb), jnp.int32),
                        pltpu.VMEM((t // tk, tk, qb), jnp.int16),
                        pltpu.VMEM((t // tk, tk, qb), jnp.int16),
                        pltpu.VMEM((DSA_HEADS, DSA_KV_RANK, qb), F32)],
        compiler_params=pltpu.CompilerParams(
            dimension_semantics=("arbitrary", "arbitrary"), vmem_limit_bytes=VMEM_LIMIT),
        name="dsa",
    )(qidx, widx_rows, qlat, kidx, ckvn, ckvt, wuvt)


def _pack_w_in(w):
    pts = [0]
    for wd in IN_WIDTHS:
        pts.append(pts[-1] + wd)
    seg = [w[..., pts[i]:pts[i + 1]] for i in range(len(IN_WIDTHS))]
    (gq, gk, gv, gz, ga, gb, cq, ckv, ik, iw, mq, mk, mv, mo, mi, mf) = seg
    zeros = lambda n: jnp.zeros(w.shape[:-1] + (n,), w.dtype)
    idx_blk = jnp.concatenate([ik, iw, zeros(LANES - IDX_DIM - IDX_HEADS)], axis=-1)
    gate_blk = jnp.concatenate([ga, gb, mi, mf, zeros(LANES - 16)], axis=-1)
    out = jnp.concatenate([gq, gk, gv, gz, cq, ckv, idx_blk, mq, mk, mv, mo, gate_blk], axis=-1)
    assert out.shape[-1] == NP
    return out.astype(BF16)


def _lane_rows(vals_at, depth):
    rows = jnp.zeros((depth, LANES), F32)
    for start, v in vals_at:
        rows = rows.at[:, start:start + v.shape[1]].set(v.astype(F32))
    return rows


def kernel(x, attn_norm, w_in, gdn_conv, gdn_a_log, gdn_dt_bias, gdn_out_norm, dsa_q_norm,
           dsa_kv_norm, dsa_w_uq, dsa_w_qidx, dsa_w_uk, dsa_w_uv, mlstm_i_bias, mlstm_f_bias,
           mlstm_out_norm, w_out, ffn_norm, w_gate, w_up, w_down, final_norm):
    b, t, d = x.shape
    depth = w_in.shape[0]
    gdn_chunk, mlstm_chunk = min(128, t), min(64, t)
    tb = min(512, t)
    qb = min(256, t)
    tk = min(512, t)

    w_in_p = _pack_w_in(w_in)
    gdn_par_c = jnp.stack([_lane_rows([(0, gdn_a_log)], depth), _lane_rows([(0, gdn_dt_bias)], depth)],
                          axis=1)
    gdn_par_r = jnp.swapaxes(gdn_par_c[:, :, :16], 1, 2)
    gdn_og = jnp.tile(gdn_out_norm, (1, GDN_HEADS))
    wuk_bd = jnp.zeros((depth, DSA_HEADS * DSA_HEAD_DIM, DSA_HEADS * DSA_KV_RANK), F32)
    for h in range(DSA_HEADS):
        wuk_bd = wuk_bd.at[:, h * DSA_HEAD_DIM:(h + 1) * DSA_HEAD_DIM,
                           h * DSA_KV_RANK:(h + 1) * DSA_KV_RANK].set(jnp.swapaxes(dsa_w_uk[:, h], 1, 2))
    wuk_bd = wuk_bd.astype(BF16)
    wuq_b, wqi_b = dsa_w_uq.astype(BF16), dsa_w_qidx.astype(BF16)
    wuvt_b = jnp.swapaxes(dsa_w_uv, 2, 3).astype(BF16)
    ml_bias_c = _lane_rows([(8, mlstm_i_bias), (12, mlstm_f_bias)], depth)
    ml_og = jnp.tile(mlstm_out_norm, (1, MLSTM_HEADS))
    w_out_b, w_gate_b, w_up_b, w_down_b = (a.astype(BF16) for a in (w_out, w_gate, w_up, w_down))

    x2 = x.reshape(b * t, d)
    for l in range(depth):
        proj2 = _inproj(x2, attn_norm[l].reshape(1, d), w_in_p[l])
        proj3 = proj2.reshape(b, t, NP)
        gates_t = jnp.swapaxes(proj3[:, :, C_GATE:C_GATE + 16], 1, 2)

        def chunk_rows(c):
            return jnp.swapaxes(gates_t.reshape(b, 16, t // c, c), 1, 2)

        y_a = _gdn(proj3, chunk_rows(gdn_chunk), gdn_conv[l], gdn_par_c[l], gdn_par_r[l],
                   gdn_og[l].reshape(1, -1), chunk=gdn_chunk, tb=tb)

        qlat, qidx, ckvn, ckvt, kidx = _dsa_prep(
            proj2, dsa_q_norm[l].reshape(1, -1), dsa_kv_norm[l].reshape(1, -1),
            wuq_b[l], wqi_b[l], wuk_bd[l])
        r3 = lambda a: a.reshape(b, t, a.shape[-1])
        widx_rows = jnp.swapaxes(proj3[:, :, C_IDX + IDX_DIM:C_IDX + IDX_DIM + IDX_HEADS], 1, 2)
        y_b = _dsa(r3(qidx), widx_rows, r3(qlat), r3(kidx), r3(ckvn), ckvt, wuvt_b[l], qb=qb, tk=tk)

        y_c = _mlstm(proj3, chunk_rows(mlstm_chunk), ml_bias_c[l].reshape(1, LANES),
                     ml_bias_c[l, :16].reshape(16, 1), ml_og[l].reshape(1, -1),
                     chunk=mlstm_chunk, tb=tb)

        x2 = _out_ffn(x2, y_a.reshape(b * t, -1), y_b.reshape(b * t, -1), y_c.reshape(b * t, -1),
                      w_out_b[l], ffn_norm[l].reshape(1, d), w_gate_b[l], w_up_b[l], w_down_b[l],
                      final_norm.reshape(1, d), final_norm=(l == depth - 1))
    return x2.reshape(b, t, d)
```

```python
import functools
import math

import jax
import jax.numpy as jnp
from jax import lax
from jax.experimental import pallas as pl
from jax.experimental.pallas import tpu as pltpu

F32 = jnp.float32
BF16 = jnp.bfloat16
HIGHEST = lax.Precision.HIGHEST

D_MODEL = 1024
GDN_HEADS = 4
GDN_DK = 128
GDN_DV = 128
GDN_CONV = 4
DSA_HEADS = 4
DSA_HEAD_DIM = 64
DSA_Q_RANK = 256
DSA_KV_RANK = 128
IDX_HEADS = 8
IDX_DIM = 32
IDX_TOPK = 256
MLSTM_HEADS = 4
MLSTM_DQK = 64
MLSTM_DV = 64
FFN_HIDDEN = 2816
NORM_EPS = 1e-6

IN_WIDTHS = (512, 512, 512, 512, 4, 4, 256, 128, 32, 8, 256, 256, 256, 256, 4, 4)

C_GQ, C_GK, C_GV, C_GZ = 0, 512, 1024, 1536
C_CQ, C_CKV, C_IDX = 2048, 2304, 2432
C_MQ, C_MK, C_MV, C_MO = 2560, 2816, 3072, 3328
C_GATE = 3584
NP = 3712
LANES = 128

VMEM_LIMIT = 56 * 1024 * 1024

INT_MIN = -2 ** 31


def _mm(a, b):
    return jnp.dot(a.astype(BF16), b.astype(BF16), preferred_element_type=F32)


def _mm_nt(a, b):
    return lax.dot_general(a.astype(BF16), b.astype(BF16), (((1,), (1,)), ((), ())),
                           preferred_element_type=F32)


def _sigmoid(x):
    return 1.0 / (1.0 + jnp.exp(-x))


def _softplus(x):
    return jnp.maximum(x, 0.0) + jnp.log1p(jnp.exp(-jnp.abs(x)))


def _tri(n, kind):
    r = lax.broadcasted_iota(jnp.int32, (n, n), 0)
    c = lax.broadcasted_iota(jnp.int32, (n, n), 1)
    if kind == "incl":
        return r >= c
    if kind == "strict":
        return r > c
    if kind == "upper_incl":
        return r <= c
    raise ValueError(kind)


def _inproj_kernel(x_ref, g_ref, w_ref, o_ref):
    x = x_ref[...]
    ms = jnp.mean(x * x, axis=-1, keepdims=True)
    h = (x * lax.rsqrt(ms + NORM_EPS)) * g_ref[...]
    o_ref[...] = jnp.dot(h.astype(BF16), w_ref[...], preferred_element_type=F32)


def _inproj(x2, g, w):
    m = x2.shape[0]
    tm = 256
    return pl.pallas_call(
        _inproj_kernel,
        grid=(m // tm,),
        in_specs=[
            pl.BlockSpec((tm, D_MODEL), lambda i: (i, 0)),
            pl.BlockSpec((1, D_MODEL), lambda i: (0, 0)),
            pl.BlockSpec((D_MODEL, NP), lambda i: (0, 0)),
        ],
        out_specs=pl.BlockSpec((tm, NP), lambda i: (i, 0)),
        out_shape=jax.ShapeDtypeStruct((m, NP), F32),
        compiler_params=pltpu.CompilerParams(
            dimension_semantics=("arbitrary",), vmem_limit_bytes=VMEM_LIMIT),
        name="inproj",
    )(x2, g, w)


def _ffn_kernel(x_ref, ya_ref, yb_ref, yc_ref, wo_ref, g_ref, wg_ref, wu_ref, wd_ref, fg_ref,
                o_ref, acc_ref, h_ref, *, n_hidden_steps, final_norm):
    j = pl.program_id(1)

    @pl.when(j == 0)
    def _():
        xn = x_ref[...]
        xn = xn + jnp.dot(ya_ref[...], wo_ref[0:512, :], preferred_element_type=F32)
        xn = xn + jnp.dot(yb_ref[...], wo_ref[512:768, :], preferred_element_type=F32)
        xn = xn + jnp.dot(yc_ref[...], wo_ref[768:1024, :], preferred_element_type=F32)
        acc_ref[...] = xn
        ms = jnp.mean(xn * xn, axis=-1, keepdims=True)
        h_ref[...] = ((xn * lax.rsqrt(ms + NORM_EPS)) * g_ref[...]).astype(BF16)

    h = h_ref[...]
    gate = jnp.dot(h, wg_ref[...], preferred_element_type=F32)
    up = jnp.dot(h, wu_ref[...], preferred_element_type=F32)
    act = (gate * _sigmoid(gate)) * up
    acc_ref[...] += jnp.dot(act.astype(BF16), wd_ref[...], preferred_element_type=F32)

    @pl.when(j == n_hidden_steps - 1)
    def _():
        y = acc_ref[...]
        if final_norm:
            ms = jnp.mean(y * y, axis=-1, keepdims=True)
            y = (y * lax.rsqrt(ms + NORM_EPS)) * fg_ref[...]
        o_ref[...] = y


def _out_ffn(x2, ya, yb, yc, wo, g, wg, wu, wd, fg, final_norm):
    m = x2.shape[0]
    tm = 512
    th = 2816
    nh = FFN_HIDDEN // th
    kern = functools.partial(_ffn_kernel, n_hidden_steps=nh, final_norm=final_norm)
    return pl.pallas_call(
        kern,
        grid=(m // tm, nh),
        in_specs=[
            pl.BlockSpec((tm, D_MODEL), lambda i, j: (i, 0)),
            pl.BlockSpec((tm, 512), lambda i, j: (i, 0)),
            pl.BlockSpec((tm, 256), lambda i, j: (i, 0)),
            pl.BlockSpec((tm, 256), lambda i, j: (i, 0)),
            pl.BlockSpec((D_MODEL, D_MODEL), lambda i, j: (0, 0), pipeline_mode=pl.Buffered(1)),
            pl.BlockSpec((1, D_MODEL), lambda i, j: (0, 0)),
            pl.BlockSpec((D_MODEL, th), lambda i, j: (0, j), pipeline_mode=pl.Buffered(1)),
            pl.BlockSpec((D_MODEL, th), lambda i, j: (0, j), pipeline_mode=pl.Buffered(1)),
            pl.BlockSpec((th, D_MODEL), lambda i, j: (j, 0), pipeline_mode=pl.Buffered(1)),
            pl.BlockSpec((1, D_MODEL), lambda i, j: (0, 0)),
        ],
        out_specs=pl.BlockSpec((tm, D_MODEL), lambda i, j: (i, 0)),
        out_shape=jax.ShapeDtypeStruct((m, D_MODEL), F32),
        scratch_shapes=[pltpu.VMEM((tm, D_MODEL), F32), pltpu.VMEM((tm, D_MODEL), BF16)],
        compiler_params=pltpu.CompilerParams(
            dimension_semantics=("arbitrary", "arbitrary"), vmem_limit_bytes=VMEM_LIMIT),
        name="out_ffn",
    )(x2, ya, yb, yc, wo, g, wg, wu, wd, fg)


def _cummax_rows(x, n):
    neg = -jnp.inf
    row = lax.broadcasted_iota(jnp.int32, x.shape, 0)
    s = 1
    while s < n:
        if s < 8:
            shifted = jnp.where(row >= s, pltpu.roll(x, s, axis=0), neg)
        else:
            shifted = jnp.concatenate([jnp.full((s, x.shape[1]), neg, x.dtype), x[:n - s]], axis=0)
        x = jnp.maximum(x, shifted)
        s *= 2
    return x


def _mlstm_kernel(q_ref, k_ref, v_ref, o_ref, gc_ref, gr_ref, bc_ref, br_ref, og_ref,
                  y_ref, sv_s, kv_s, b_s, dm_s, c_s, m_s, *, chunk, n_chunks, group):
    L = chunk
    t = pl.program_id(1)
    heads = range(MLSTM_HEADS)
    half = MLSTM_DQK

    @pl.when(t == 0)
    def _():
        c_s[...] = jnp.zeros_like(c_s)
        m_s[...] = jnp.zeros_like(m_s)

    incl = _tri(L, "incl")
    tril_b = incl.astype(BF16)
    triu_b = _tri(L, "upper_incl").astype(BF16)
    lane = lax.broadcasted_iota(jnp.int32, (L, LANES), 1)
    own = [(lane < half) if h % 2 == 0 else (lane >= half) for h in heads]
    row_c = lax.broadcasted_iota(jnp.int32, (LANES, LANES), 0)
    own_rows = [(row_c < half) if h % 2 == 0 else (row_c >= half) for h in heads]
    scale = MLSTM_DQK ** -0.5

    def pair(ref, r0, h):
        return ref[pl.ds(r0, L), (h // 2) * LANES:(h // 2 + 1) * LANES]

    def prep(cg, carry):
        ch = []
        for g in range(group):
            c = cg * group + g
            r0 = pl.multiple_of(c * L, L)
            x_col = gc_ref[pl.ds(r0, L), :] + bc_ref[...]
            x_row = gr_ref[c] + br_ref[...]
            b_row = _cumsum_lanes(-_softplus(-x_row), triu_b)
            f_rep = jnp.concatenate(
                [jnp.broadcast_to(x_col[:, 12 + h:13 + h], (L, LANES)) for h in heads], axis=1)
            b_rep = _cumsum_rows(tril_b, -_softplus(-f_rep))
            for h in heads:
                bx = b_rep[:, h * LANES:(h + 1) * LANES]
                cx = jnp.broadcast_to(x_col[:, 8 + h:9 + h], (L, LANES)) - bx
                dmax = bx + _cummax_rows(cx, L)
                c_row = x_row[8 + h:9 + h, :] - b_row[12 + h:13 + h, :]
                d_log = jnp.where(incl, bx[:, :L] + c_row - dmax[:, :L], -jnp.inf)
                kw = jnp.exp(cx + (bx[L - 1:L, :] - dmax[L - 1:L, :]))
                kp = pair(k_ref, r0, h) * scale
                vp = pair(v_ref, r0, h)
                ch.append(dict(idx=c * MLSTM_HEADS + h, h=h,
                               q=jnp.where(own[h], pair(q_ref, r0, h), 0.0), k=kp,
                               v=jnp.where(own[h], vp, 1.0), dw=jnp.exp(d_log),
                               kxw_t=(kp * kw).T, bx=bx, dmax=dmax))
        for d in ch:
            d["s"] = _mm_nt(d["q"], d["k"]) * d["dw"]
        for d in ch:
            d["sv"] = _mm(d["s"], d["v"])
        for d in ch:
            d["kv"] = jnp.where(own_rows[d["h"]], _mm(d["kxw_t"], d["v"]), 0.0)
        for d in ch:
            sv_s[d["idx"]] = d["sv"]
            kv_s[d["idx"]] = d["kv"]
            b_s[d["idx"]] = d["bx"]
            dm_s[d["idx"]] = d["dmax"]
        return carry

    lax.fori_loop(0, n_chunks // group, prep, 0)

    def scan(c, carry):
        r0 = pl.multiple_of(c * L, L)
        c_prev = [c_s[h] for h in heads]
        qc = [_mm(pair(q_ref, r0, h), c_prev[h]) for h in heads]
        outs = []
        for h in heads:
            idx = c * MLSTM_HEADS + h
            bx = b_s[idx]
            dmax = dm_s[idx]
            m_prev = m_s[h:h + 1, :]
            a_log = bx + m_prev
            m_t = jnp.maximum(a_log, dmax)
            num = jnp.exp(a_log - m_t) * qc[h] + jnp.exp(dmax - m_t) * sv_s[idx]
            den = pltpu.roll(num, half, axis=1)
            hh = jnp.where(own[h], num / jnp.maximum(jnp.abs(den), jnp.exp(-m_t)), 0.0)
            m_new = m_t[L - 1:L, :]
            dec = jnp.exp(bx[L - 1:L, :] + m_prev - m_new)
            c_s[h] = dec * c_prev[h] + jnp.exp(dmax[L - 1:L, :] - m_new) * kv_s[idx]
            m_s[h:h + 1, :] = m_new
            ms = jnp.sum(hh * hh, axis=-1, keepdims=True) * (1.0 / MLSTM_DV)
            outs.append(hh * lax.rsqrt(ms + NORM_EPS))
        y = jnp.concatenate([outs[0] + outs[1], outs[2] + outs[3]], axis=1) * og_ref[...]
        y = y * _sigmoid(o_ref[pl.ds(r0, L), :])
        y_ref[pl.ds(r0, L), :] = y.astype(y_ref.dtype)
        return carry

    lax.fori_loop(0, n_chunks, scan, 0)


def _mlstm(proj3, gates_r, bias_c, bias_r, og, *, chunk, tb):
    b, t, _ = proj3.shape
    nc = tb // chunk
    nch = nc * MLSTM_HEADS
    wq = MLSTM_HEADS * MLSTM_DQK
    group = 2 if nc % 2 == 0 else 1
    kern = functools.partial(_mlstm_kernel, chunk=chunk, n_chunks=nc, group=group)
    col = lambda c0: pl.BlockSpec((None, tb, wq), lambda i, j, c0=c0: (i, j, c0 // wq))
    return pl.pallas_call(
        kern,
        grid=(b, t // tb),
        in_specs=[
            col(C_MQ), col(C_MK), col(C_MV), col(C_MO),
            pl.BlockSpec((None, tb, LANES), lambda i, j: (i, j, C_GATE // LANES)),
            pl.BlockSpec((None, nc, 16, chunk), lambda i, j: (i, j, 0, 0)),
            pl.BlockSpec((1, LANES), lambda i, j: (0, 0)),
            pl.BlockSpec((16, 1), lambda i, j: (0, 0)),
            pl.BlockSpec((1, wq), lambda i, j: (0, 0)),
        ],
        out_specs=pl.BlockSpec((None, tb, wq), lambda i, j: (i, j, 0)),
        out_shape=jax.ShapeDtypeStruct((b, t, wq), BF16),
        scratch_shapes=[pltpu.VMEM((nch, chunk, LANES), F32),
                        pltpu.VMEM((nch, LANES, LANES), F32),
                        pltpu.VMEM((nch, chunk, LANES), F32),
                        pltpu.VMEM((nch, chunk, LANES), F32),
                        pltpu.VMEM((MLSTM_HEADS, LANES, LANES), F32),
                        pltpu.VMEM((8, LANES), F32)],
        compiler_params=pltpu.CompilerParams(
            dimension_semantics=("arbitrary", "arbitrary"), vmem_limit_bytes=VMEM_LIMIT),
        name="mlstm",
    )(proj3, proj3, proj3, proj3, proj3, gates_r, bias_c, bias_r, og)


def _split3(x):
    x1 = x.astype(BF16)
    r1 = x - x1.astype(F32)
    x2 = r1.astype(BF16)
    x3 = (r1 - x2.astype(F32)).astype(BF16)
    return x1, x2, x3


def _cumsum_rows(tril_b, x):
    return sum(jnp.dot(tril_b, p, preferred_element_type=F32) for p in _split3(x))


def _cumsum_lanes(x, triu_b):
    return sum(jnp.dot(p, triu_b, preferred_element_type=F32) for p in _split3(x))


def _inverse_masks(n):
    r = lax.broadcasted_iota(jnp.int32, (n, n), 0)
    c = lax.broadcasted_iota(jnp.int32, (n, n), 1)
    offs = []
    lvl = 0
    while (1 << lvl) < n:
        rb = r >> lvl
        cb = c >> lvl
        offs.append(((rb & 1) == 1) & (cb == rb - 1))
        lvl += 1
    return (r == c).astype(F32), offs


def _gdn_kernel(q_ref, k_ref, v_ref, z_ref, qh_ref, kh_ref, vh_ref, gc_ref, gr_ref, cw_ref,
                pc_ref, pr_ref, og_ref, y_ref, qs, ks, vs, xp_s, u_s, w_s, qk_s, qd_s, kdt_s, el_s, s_s,
                *, chunk, n_chunks, tb, group):
    C = chunk
    t = pl.program_id(1)
    width = GDN_HEADS * GDN_DK

    @pl.when(t == 0)
    def _():
        s_s[...] = jnp.zeros_like(s_s)

    def conv_silu(x_ref, halo_ref, w0):
        xp_s[0:8, :] = jnp.where(t == 0, 0.0, halo_ref[...])
        xp_s[8:, :] = x_ref[...]
        y = xp_s[8:, :] * cw_ref[GDN_CONV - 1:GDN_CONV, w0:w0 + width]
        for j in range(GDN_CONV - 1):
            y = y + xp_s[pl.ds(8 - (GDN_CONV - 1) + j, tb), :] * cw_ref[j:j + 1, w0:w0 + width]
        return y * _sigmoid(y)

    def l2n(x):
        parts = []
        for h in range(GDN_HEADS):
            xh = x[:, h * GDN_DK:(h + 1) * GDN_DK]
            parts.append(xh * lax.rsqrt(jnp.sum(xh * xh, axis=-1, keepdims=True) + NORM_EPS))
        return jnp.concatenate(parts, axis=1)

    qs[...] = l2n(conv_silu(q_ref, qh_ref, 0)) * (GDN_DK ** -0.5)
    ks[...] = l2n(conv_silu(k_ref, kh_ref, width))
    vs[...] = conv_silu(v_ref, vh_ref, 2 * width)

    incl = _tri(C, "incl")
    strict = _tri(C, "strict")
    tril_b = incl.astype(BF16)
    triu_b = _tri(C, "upper_incl").astype(BF16)
    inv_masks = _inverse_masks(C)

    def prep(cg, carry):
        loaded = []
        for g in range(group):
            c = cg * group + g
            r0 = pl.multiple_of(c * C, C)
            gcol = gc_ref[pl.ds(r0, C), :]
            grow = gr_ref[c]
            qkv = [(qs[pl.ds(r0, C), h * GDN_DK:(h + 1) * GDN_DK],
                    ks[pl.ds(r0, C), h * GDN_DK:(h + 1) * GDN_DK],
                    vs[pl.ds(r0, C), h * GDN_DK:(h + 1) * GDN_DK]) for h in range(GDN_HEADS)]
            loaded.append((c, gcol, grow, qkv))
        ch = []
        for c, gcol, grow, qkv in loaded:
            g_col = -jnp.exp(pc_ref[0:1, :]) * _softplus(gcol + pc_ref[1:2, :])
            gcum_col = _cumsum_rows(tril_b, g_col)
            beta_col = _sigmoid(gcol)
            g_row = -jnp.exp(pr_ref[:, 0:1]) * _softplus(grow + pr_ref[:, 1:2])
            gcum_row = _cumsum_lanes(g_row, triu_b)
            for h in range(GDN_HEADS):
                q, k, v = qkv[h]
                gc_c = gcum_col[:, h:h + 1]
                gc_r = gcum_row[h:h + 1, :]
                beta = beta_col[:, 4 + h:5 + h]
                decay = jnp.where(incl, jnp.exp(jnp.where(incl, gc_c - gc_r, 0.0)), 0.0)
                ch.append(dict(idx=c * GDN_HEADS + h, q=q, k=k, v=v, gc_c=gc_c, beta=beta,
                               decay=decay, kb=k * beta))
        for d in ch:
            d["a"] = jnp.where(strict, _mm_nt(d["kb"], d["k"]) * d["decay"], 0.0)
        eye, offs = inv_masks
        for d in ch:
            d["t"] = eye - jnp.where(offs[0], d["a"], 0.0)
        for off in offs[1:]:
            for d in ch:
                d["p"] = _mm(d["t"], jnp.where(off, d["a"], 0.0))
            for d in ch:
                d["t"] = d["t"] - _mm(d["p"], d["t"])
        for d in ch:
            d["egc"] = jnp.exp(d["gc_c"])
            d["u"] = _mm(d["t"], d["v"] * d["beta"])
        for d in ch:
            d["w"] = _mm(d["t"], d["kb"] * d["egc"]).astype(BF16)
        for d in ch:
            d["qk"] = (_mm_nt(d["q"], d["k"]) * d["decay"]).astype(BF16)
        for d in ch:
            idx = d["idx"]
            g_last = d["gc_c"][C - 1:C, :]
            u_s[idx] = d["u"]
            w_s[idx] = d["w"]
            qk_s[idx] = d["qk"]
            qd_s[idx] = (d["q"] * d["egc"]).astype(BF16)
            kdt_s[idx] = (d["k"] * jnp.exp(g_last - d["gc_c"])).T.astype(BF16)
            el_s[idx] = jnp.broadcast_to(jnp.exp(g_last), (8, LANES))
        return carry

    lax.fori_loop(0, n_chunks // group, prep, 0)

    def scan(c, carry):
        r0 = pl.multiple_of(c * C, C)
        heads = range(GDN_HEADS)
        dot = functools.partial(jnp.dot, preferred_element_type=F32)
        idx = [c * GDN_HEADS + h for h in heads]
        s_prev = [s_s[h] for h in heads]
        s_b = [s.astype(BF16) for s in s_prev]
        ws = [dot(w_s[idx[h]], s_b[h]) for h in heads]
        qs_ = [dot(qd_s[idx[h]], s_b[h]) for h in heads]
        v_b = [(u_s[idx[h]] - ws[h]).astype(BF16) for h in heads]
        upd = [dot(kdt_s[idx[h]], v_b[h]) for h in heads]
        o_all = [qs_[h] + dot(qk_s[idx[h]], v_b[h]) for h in heads]
        outs = []
        for h in heads:
            s_s[h] = s_prev[h] * el_s[idx[h]][0:1, :] + upd[h]
            o = o_all[h]
            ms = jnp.mean(o * o, axis=-1, keepdims=True)
            outs.append(o * lax.rsqrt(ms + NORM_EPS))
        y = jnp.concatenate(outs, axis=1) * og_ref[...]
        z = z_ref[pl.ds(r0, C), :]
        y = y * (z * _sigmoid(z))
        y_ref[pl.ds(r0, C), :] = y.astype(y_ref.dtype)
        return carry

    lax.fori_loop(0, n_chunks, scan, 0)


def _gdn(proj3, gates_r, conv_w, par_c, par_r, og, *, chunk, tb):
    b, t, _ = proj3.shape
    nc = tb // chunk
    nch = nc * GDN_HEADS
    width = GDN_HEADS * GDN_DK
    group = 4 if nc % 4 == 0 else (2 if nc % 2 == 0 else 1)
    kern = functools.partial(_gdn_kernel, chunk=chunk, n_chunks=nc, tb=tb, group=group)
    col = lambda c0: pl.BlockSpec((None, tb, width), lambda i, j, c0=c0: (i, j, c0 // width))
    halo = lambda c0: pl.BlockSpec(
        (None, 8, width), lambda i, j, c0=c0: (i, jnp.maximum(j * (tb // 8) - 1, 0), c0 // width))
    return pl.pallas_call(
        kern,
        grid=(b, t // tb),
        in_specs=[
            col(C_GQ), col(C_GK), col(C_GV), col(C_GZ),
            halo(C_GQ), halo(C_GK), halo(C_GV),
            pl.BlockSpec((None, tb, LANES), lambda i, j: (i, j, C_GATE // LANES)),
            pl.BlockSpec((None, nc, 16, chunk), lambda i, j: (i, j, 0, 0)),
            pl.BlockSpec((GDN_CONV, 3 * width), lambda i, j: (0, 0)),
            pl.BlockSpec((2, LANES), lambda i, j: (0, 0)),
            pl.BlockSpec((16, 2), lambda i, j: (0, 0)),
            pl.BlockSpec((1, width), lambda i, j: (0, 0)),
        ],
        out_specs=pl.BlockSpec((None, tb, width), lambda i, j: (i, j, 0)),
        out_shape=jax.ShapeDtypeStruct((b, t, width), BF16),
        scratch_shapes=[pltpu.VMEM((tb, width), F32), pltpu.VMEM((tb, width), F32),
                        pltpu.VMEM((tb, width), F32),
                        pltpu.VMEM((tb + 8, width), F32),
                        pltpu.VMEM((nch, chunk, GDN_DV), F32),
                        pltpu.VMEM((nch, chunk, GDN_DK), BF16),
                        pltpu.VMEM((nch, chunk, chunk), BF16),
                        pltpu.VMEM((nch, chunk, GDN_DK), BF16),
                        pltpu.VMEM((nch, GDN_DK, chunk), BF16),
                        pltpu.VMEM((nch, 8, LANES), F32),
                        pltpu.VMEM((GDN_HEADS, GDN_DK, GDN_DV), F32)],
        compiler_params=pltpu.CompilerParams(
            dimension_semantics=("arbitrary", "arbitrary"), vmem_limit_bytes=VMEM_LIMIT),
        name="gdn",
    )(proj3, proj3, proj3, proj3, proj3, proj3, proj3, proj3, gates_r, conv_w, par_c, par_r, og)


def _dsa_prep_kernel(cq_ref, ckv_ref, idx_ref, qn_ref, kn_ref, wuq_ref, wqi_ref, wuk_ref,
                     qlat_ref, qidx_ref, ckvn_ref, ckvt_ref, kidx_ref):
    cq = cq_ref[...]
    ms = jnp.mean(cq * cq, axis=-1, keepdims=True)
    cqn = ((cq * lax.rsqrt(ms + NORM_EPS)) * qn_ref[...]).astype(BF16)
    ckv = ckv_ref[...]
    ms = jnp.mean(ckv * ckv, axis=-1, keepdims=True)
    ckvn = (ckv * lax.rsqrt(ms + NORM_EPS)) * kn_ref[...]
    ckvn_ref[...] = ckvn.astype(BF16)
    ckvt_ref[...] = ckvn.T.astype(BF16)
    q = jnp.dot(cqn, wuq_ref[...], preferred_element_type=F32)
    qlat = jnp.dot(q.astype(BF16), wuk_ref[...], preferred_element_type=F32) * (DSA_HEAD_DIM ** -0.5)
    qlat_ref[...] = qlat.astype(BF16)
    qidx_ref[...] = jnp.dot(cqn, wqi_ref[...], preferred_element_type=F32).astype(BF16)
    kidx_ref[...] = idx_ref[:, 0:IDX_DIM].astype(BF16)


def _dsa_prep(proj2, qn, kn, wuq, wqi, wuk_bd):
    m = proj2.shape[0]
    tm = 512
    full = lambda shape: pl.BlockSpec(shape, lambda i: (0, 0))
    return pl.pallas_call(
        _dsa_prep_kernel,
        grid=(m // tm,),
        in_specs=[
            pl.BlockSpec((tm, DSA_Q_RANK), lambda i: (i, C_CQ // DSA_Q_RANK)),
            pl.BlockSpec((tm, DSA_KV_RANK), lambda i: (i, C_CKV // DSA_KV_RANK)),
            pl.BlockSpec((tm, LANES), lambda i: (i, C_IDX // LANES)),
            full((1, DSA_Q_RANK)), full((1, DSA_KV_RANK)),
            full((DSA_Q_RANK, DSA_HEADS * DSA_HEAD_DIM)),
            full((DSA_Q_RANK, IDX_HEADS * IDX_DIM)),
            full((DSA_HEADS * DSA_HEAD_DIM, DSA_HEADS * DSA_KV_RANK)),
        ],
        out_specs=[
            pl.BlockSpec((tm, DSA_HEADS * DSA_KV_RANK), lambda i: (i, 0)),
            pl.BlockSpec((tm, IDX_HEADS * IDX_DIM), lambda i: (i, 0)),
            pl.BlockSpec((tm, DSA_KV_RANK), lambda i: (i, 0)),
            pl.BlockSpec((DSA_KV_RANK, tm), lambda i: (0, i)),
            pl.BlockSpec((tm, IDX_DIM), lambda i: (i, 0)),
        ],
        out_shape=[
            jax.ShapeDtypeStruct((m, DSA_HEADS * DSA_KV_RANK), BF16),
            jax.ShapeDtypeStruct((m, IDX_HEADS * IDX_DIM), BF16),
            jax.ShapeDtypeStruct((m, DSA_KV_RANK), BF16),
            jax.ShapeDtypeStruct((DSA_KV_RANK, m), BF16),
            jax.ShapeDtypeStruct((m, IDX_DIM), BF16),
        ],
        compiler_params=pltpu.CompilerParams(
            dimension_semantics=("arbitrary",), vmem_limit_bytes=VMEM_LIMIT),
        name="dsa_prep",
    )(proj2, proj2, proj2, qn, kn, wuq, wqi, wuk_bd)


def _tree_sum(parts):
    while len(parts) > 1:
        parts = [parts[j] + parts[j + 1] for j in range(0, len(parts) - 1, 2)] + (
            [parts[-1]] if len(parts) % 2 else [])
    return parts[0]


def _dsa_kernel(qidx_ref, wrow_ref, qlat_ref, kidx_ref, ckv_ref, ckvt_ref, wuvt_ref, y_ref,
                key_s, hi_s, lo_s, acc_s, *, qb, tk, n_sel):
    i = pl.program_id(1)
    n_tiles = (i * qb + qb + tk - 1) // tk
    key_pos = lax.broadcasted_iota(jnp.int32, (tk, qb), 0)
    q_pos = i * qb + lax.broadcasted_iota(jnp.int32, (tk, qb), 1)
    w_rows = wrow_ref[...] * (IDX_HEADS ** -0.5 * IDX_DIM ** -0.5)
    qidx = qidx_ref[...]
    q_heads = [qidx[:, h * IDX_DIM:(h + 1) * IDX_DIM] for h in range(IDX_HEADS)]
    i16_min = -(1 << 15)
    i16_max = (1 << 15) - 1
    sub16 = 16
    nt_dims = (((1,), (1,)), ((), ()))

    def score_body(kt, carry):
        k0 = pl.multiple_of(kt * tk, tk)
        kk = kidx_ref[pl.ds(k0, tk), :]
        sh = [lax.dot_general(kk, q_heads[h], nt_dims, preferred_element_type=F32)
              for h in range(IDX_HEADS)]
        acc = jnp.zeros((tk, qb), F32)
        for h in range(IDX_HEADS):
            acc = acc + jnp.maximum(sh[h], 0.0) * w_rows[h:h + 1, :]
        acc = acc + 0.0
        bits = pltpu.bitcast(acc, jnp.int32)
        keys = jnp.where(bits < 0, bits ^ jnp.int32(0x7FFFFFFF), bits)
        keys = jnp.where(k0 + key_pos <= q_pos, keys, jnp.int32(INT_MIN))
        key_s[kt] = keys
        hi_s[kt] = (keys >> 16).astype(jnp.int16)
        return carry

    lax.fori_loop(0, n_tiles, score_body, 0)

    def count_ge(src_s, cand):
        cand16 = jnp.broadcast_to(cand, (sub16, qb)).astype(jnp.int16)
        one, zero = jnp.int16(1), jnp.int16(0)

        def body(kt, acc):
            tile = src_s[kt]
            return acc + _tree_sum([jnp.where(tile[j * sub16:(j + 1) * sub16, :] >= cand16, one, zero)
                                    for j in range(tk // sub16)])

        cnt = lax.fori_loop(0, n_tiles, body, jnp.zeros((sub16, qb), jnp.int16))
        return jnp.sum(cnt.astype(jnp.int32), axis=0, keepdims=True)

    def kth_largest16(src_s, rank):
        thr = jnp.full((1, qb), i16_min, jnp.int32)
        for bit in range(15, -1, -1):
            cand = thr + jnp.int32(1 << bit)
            thr = jnp.where(count_ge(src_s, cand) >= rank, cand, thr)
        return thr

    hi_thr = kth_largest16(hi_s, jnp.int32(n_sel))
    above = count_ge(hi_s, jnp.minimum(hi_thr + 1, jnp.int32(i16_max)))
    rank_lo = jnp.int32(n_sel) - jnp.where(hi_thr < i16_max, above, 0)

    def lo_body(kt, carry):
        keys = key_s[kt]
        lo = (keys & jnp.int32(0xFFFF)) + jnp.int32(i16_min)
        lo_s[kt] = jnp.where((keys >> 16) == hi_thr, lo, jnp.int32(i16_min)).astype(jnp.int16)
        return carry

    lax.fori_loop(0, n_tiles, lo_body, 0)
    lo_thr = kth_largest16(lo_s, rank_lo)
    thr_raw = hi_thr * jnp.int32(1 << 16) + (lo_thr - jnp.int32(i16_min))
    thr = jnp.maximum(thr_raw, jnp.int32(INT_MIN + 1))

    def selected_body(kt, acc):
        return acc + jnp.sum(jnp.where(key_s[kt] >= thr_raw, 1, 0), axis=0, keepdims=True)

    n_selected = lax.fori_loop(0, n_tiles, selected_body, jnp.zeros((1, qb), jnp.int32))
    excess = jnp.where(thr_raw > INT_MIN, n_selected - jnp.int32(n_sel), 0)

    @pl.when(jnp.max(excess) > 0)
    def _():
        last = tk * key_s.shape[0] - 1
        assert last <= i16_max

        def fill(kt, carry):
            rev = jnp.int32(last) - (kt * tk + key_pos)
            lo_s[kt] = jnp.where(key_s[kt] == thr_raw, rev, jnp.int32(-1)).astype(jnp.int16)
            return carry

        lax.fori_loop(0, n_tiles, fill, 0)
        keep = count_ge(lo_s, jnp.zeros((1, qb), jnp.int32)) - excess
        cut = jnp.zeros((1, qb), jnp.int32)
        for bit in range(max(last, 1).bit_length() - 1, -1, -1):
            cand = cut + jnp.int32(1 << bit)
            cut = jnp.where(count_ge(lo_s, cand) >= keep, cand, cut)
        cut = jnp.where(excess > 0, cut, 0)

        def demote(kt, carry):
            keys = key_s[kt]
            rev = jnp.int32(last) - (kt * tk + key_pos)
            demoted = jnp.where(rev < cut, jnp.int32(INT_MIN), keys)
            key_s[kt] = jnp.where(keys == thr_raw, demoted, keys)
            return carry

        lax.fori_loop(0, n_tiles, demote, 0)

    qlat = qlat_ref[...]
    heads = range(DSA_HEADS)
    ql = [qlat[:, h * DSA_KV_RANK:(h + 1) * DSA_KV_RANK] for h in heads]
    acc_s[...] = jnp.zeros_like(acc_s)

    def attn_body(kt, carry):
        m_run, l_run = carry
        k0 = pl.multiple_of(kt * tk, tk)
        ckv = ckv_ref[pl.ds(k0, tk), :]
        ckvt = ckvt_ref[:, pl.ds(k0, tk)]
        bias = jnp.where(key_s[kt] >= thr, 0.0, -jnp.inf)
        lg = [lax.dot_general(ckv, ql[h], nt_dims, preferred_element_type=F32) for h in heads]
        m_out, l_out, p_all, alphas = [], [], [], []
        for h in heads:
            lgm = lg[h] + bias
            m_new = jnp.maximum(m_run[h], jnp.max(lgm, axis=0, keepdims=True))
            p = jnp.exp(lgm - m_new)
            alpha = jnp.exp(m_run[h] - m_new)
            l_out.append(alpha * l_run[h] + jnp.sum(p, axis=0, keepdims=True))
            m_out.append(m_new)
            p_all.append(p.astype(BF16))
            alphas.append(alpha)
        pv = [jnp.dot(ckvt, p_all[h], preferred_element_type=F32) for h in heads]
        for h in heads:
            acc_s[h] = alphas[h] * acc_s[h] + pv[h]
        return tuple(m_out), tuple(l_out)

    init = (tuple(jnp.full((1, qb), -1e30, F32) for _ in heads),
            tuple(jnp.zeros((1, qb), F32) for _ in heads))
    _, l_fin = lax.fori_loop(0, n_tiles, attn_body, init)
    o_lat = [(acc_s[h] / l_fin[h]).astype(BF16) for h in heads]
    outs = [jnp.dot(wuvt_ref[h], o_lat[h], preferred_element_type=F32) for h in heads]
    y_ref[...] = jnp.concatenate(outs, axis=0).T.astype(y_ref.dtype)


def _dsa(qidx, widx_rows, qlat, kidx, ckvn, ckvt, wuvt, *, qb, tk):
    b, t, _ = qidx.shape
    n_sel = min(IDX_TOPK, t // 4)
    kern = functools.partial(_dsa_kernel, qb=qb, tk=tk, n_sel=n_sel)
    return pl.pallas_call(
        kern,
        grid=(b, t // qb),
        in_specs=[
            pl.BlockSpec((None, qb, IDX_HEADS * IDX_DIM), lambda i, j: (i, j, 0)),
            pl.BlockSpec((None, IDX_HEADS, qb), lambda i, j: (i, 0, j)),
            pl.BlockSpec((None, qb, DSA_HEADS * DSA_KV_RANK), lambda i, j: (i, j, 0)),
            pl.BlockSpec((None, t, IDX_DIM), lambda i, j: (i, 0, 0)),
            pl.BlockSpec((None, t, DSA_KV_RANK), lambda i, j: (i, 0, 0)),
            pl.BlockSpec((DSA_KV_RANK, t), lambda i, j: (0, i)),
            pl.BlockSpec((DSA_HEADS, DSA_HEAD_DIM, DSA_KV_RANK), lambda i, j: (0, 0, 0)),
        ],
        out_specs=pl.BlockSpec((None, qb, DSA_HEADS * DSA_HEAD_DIM), lambda i, j: (i, j, 0)),
        out_shape=jax.ShapeDtypeStruct((b, t, DSA_HEADS * DSA_HEAD_DIM), BF16),
        scratch_shapes=[pltpu.VMEM((t // tk, tk, qb), jnp.int32),
                        pltpu.VMEM((t // tk, tk, qb), jnp.int16),
                        pltpu.VMEM((t // tk, tk, qb), jnp.int16),
                        pltpu.VMEM((DSA_HEADS, DSA_KV_RANK, qb), F32)],
        compiler_params=pltpu.CompilerParams(
            dimension_semantics=("arbitrary", "arbitrary"), vmem_limit_bytes=VMEM_LIMIT),
        name="dsa",
    )(qidx, widx_rows, qlat, kidx, ckvn, ckvt, wuvt)


def _pack_w_in(w):
    pts = [0]
    for wd in IN_WIDTHS:
        pts.append(pts[-1] + wd)
    seg = [w[..., pts[i]:pts[i + 1]] for i in range(len(IN_WIDTHS))]
    (gq, gk, gv, gz, ga, gb, cq, ckv, ik, iw, mq, mk, mv, mo, mi, mf) = seg
    zeros = lambda n: jnp.zeros(w.shape[:-1] + (n,), w.dtype)
    idx_blk = jnp.concatenate([ik, iw, zeros(LANES - IDX_DIM - IDX_HEADS)], axis=-1)
    gate_blk = jnp.concatenate([ga, gb, mi, mf, zeros(LANES - 16)], axis=-1)
    out = jnp.concatenate([gq, gk, gv, gz, cq, ckv, idx_blk, mq, mk, mv, mo, gate_blk], axis=-1)
    assert out.shape[-1] == NP
    return out.astype(BF16)


def _lane_rows(vals_at, depth):
    rows = jnp.zeros((depth, LANES), F32)
    for start, v in vals_at:
        rows = rows.at[:, start:start + v.shape[1]].set(v.astype(F32))
    return rows


def kernel(x, attn_norm, w_in, gdn_conv, gdn_a_log, gdn_dt_bias, gdn_out_norm, dsa_q_norm,
           dsa_kv_norm, dsa_w_uq, dsa_w_qidx, dsa_w_uk, dsa_w_uv, mlstm_i_bias, mlstm_f_bias,
           mlstm_out_norm, w_out, ffn_norm, w_gate, w_up, w_down, final_norm):
    b, t, d = x.shape
    depth = w_in.shape[0]
    gdn_chunk, mlstm_chunk = min(128, t), min(64, t)
    tb = min(512, t)
    qb = min(256, t)
    tk = min(512, t)

    w_in_p = _pack_w_in(w_in)
    gdn_par_c = jnp.stack([_lane_rows([(0, gdn_a_log)], depth), _lane_rows([(0, gdn_dt_bias)], depth)],
                          axis=1)
    gdn_par_r = jnp.swapaxes(gdn_par_c[:, :, :16], 1, 2)
    gdn_og = jnp.tile(gdn_out_norm, (1, GDN_HEADS))
    wuk_bd = jnp.zeros((depth, DSA_HEADS * DSA_HEAD_DIM, DSA_HEADS * DSA_KV_RANK), F32)
    for h in range(DSA_HEADS):
        wuk_bd = wuk_bd.at[:, h * DSA_HEAD_DIM:(h + 1) * DSA_HEAD_DIM,
                           h * DSA_KV_RANK:(h + 1) * DSA_KV_RANK].set(jnp.swapaxes(dsa_w_uk[:, h], 1, 2))
    wuk_bd = wuk_bd.astype(BF16)
    wuq_b, wqi_b = dsa_w_uq.astype(BF16), dsa_w_qidx.astype(BF16)
    wuvt_b = jnp.swapaxes(dsa_w_uv, 2, 3).astype(BF16)
    ml_bias_c = _lane_rows([(8, mlstm_i_bias), (12, mlstm_f_bias)], depth)
    ml_og = jnp.tile(mlstm_out_norm, (1, MLSTM_HEADS))
    w_out_b, w_gate_b, w_up_b, w_down_b = (a.astype(BF16) for a in (w_out, w_gate, w_up, w_down))

    x2 = x.reshape(b * t, d)
    for l in range(depth):
        proj2 = _inproj(x2, attn_norm[l].reshape(1, d), w_in_p[l])
        proj3 = proj2.reshape(b, t, NP)
        gates_t = jnp.swapaxes(proj3[:, :, C_GATE:C_GATE + 16], 1, 2)

        def chunk_rows(c):
            return jnp.swapaxes(gates_t.reshape(b, 16, t // c, c), 1, 2)

        y_a = _gdn(proj3, chunk_rows(gdn_chunk), gdn_conv[l], gdn_par_c[l], gdn_par_r[l],
                   gdn_og[l].reshape(1, -1), chunk=gdn_chunk, tb=tb)

        qlat, qidx, ckvn, ckvt, kidx = _dsa_prep(
            proj2, dsa_q_norm[l].reshape(1, -1), dsa_kv_norm[l].reshape(1, -1),
            wuq_b[l], wqi_b[l], wuk_bd[l])
        r3 = lambda a: a.reshape(b, t, a.shape[-1])
        widx_rows = jnp.swapaxes(proj3[:, :, C_IDX + IDX_DIM:C_IDX + IDX_DIM + IDX_HEADS], 1, 2)
        y_b = _dsa(r3(qidx), widx_rows, r3(qlat), r3(kidx), r3(ckvn), ckvt, wuvt_b[l], qb=qb, tk=tk)

        y_c = _mlstm(proj3, chunk_rows(mlstm_chunk), ml_bias_c[l].reshape(1, LANES),
                     ml_bias_c[l, :16].reshape(16, 1), ml_og[l].reshape(1, -1),
                     chunk=mlstm_chunk, tb=tb)

        x2 = _out_ffn(x2, y_a.reshape(b * t, -1), y_b.reshape(b * t, -1), y_c.reshape(b * t, -1),
                      w_out_b[l], ffn_norm[l].reshape(1, d), w_gate_b[l], w_up_b[l], w_down_b[l],
                      final_norm.reshape(1, d), final_norm=(l == depth - 1))
    return x2.reshape(b, t, d)
```

```python
import functools
import math

import jax
import jax.numpy as jnp
from jax import lax
from jax.experimental import pallas as pl
from jax.experimental.pallas import tpu as pltpu

F32 = jnp.float32
BF16 = jnp.bfloat16
HIGHEST = lax.Precision.HIGHEST

D_MODEL = 1024
GDN_HEADS = 4
GDN_DK = 128
GDN_DV = 128
GDN_CONV = 4
DSA_HEADS = 4
DSA_HEAD_DIM = 64
DSA_Q_RANK = 256
DSA_KV_RANK = 128
IDX_HEADS = 8
IDX_DIM = 32
IDX_TOPK = 256
MLSTM_HEADS = 4
MLSTM_DQK = 64
MLSTM_DV = 64
FFN_HIDDEN = 2816
NORM_EPS = 1e-6

IN_WIDTHS = (512, 512, 512, 512, 4, 4, 256, 128, 32, 8, 256, 256, 256, 256, 4, 4)

C_GQ, C_GK, C_GV, C_GZ = 0, 512, 1024, 1536
C_CQ, C_CKV, C_IDX = 2048, 2304, 2432
C_MQ, C_MK, C_MV, C_MO = 2560, 2816, 3072, 3328
C_GATE = 3584
NP = 3712
LANES = 128

VMEM_LIMIT = 56 * 1024 * 1024

INT_MIN = -2 ** 31


def _mm(a, b):
    return jnp.dot(a.astype(BF16), b.astype(BF16), preferred_element_type=F32)


def _mm_nt(a, b):
    return lax.dot_general(a.astype(BF16), b.astype(BF16), (((1,), (1,)), ((), ())),
                           preferred_element_type=F32)


def _sigmoid(x):
    return 1.0 / (1.0 + jnp.exp(-x))


def _softplus(x):
    return jnp.maximum(x, 0.0) + jnp.log1p(jnp.exp(-jnp.abs(x)))


def _tri(n, kind):
    r = lax.broadcasted_iota(jnp.int32, (n, n), 0)
    c = lax.broadcasted_iota(jnp.int32, (n, n), 1)
    if kind == "incl":
        return r >= c
    if kind == "strict":
        return r > c
    if kind == "upper_incl":
        return r <= c
    raise ValueError(kind)


def _inproj_kernel(x_ref, g_ref, w_ref, o_ref):
    x = x_ref[...]
    ms = jnp.mean(x * x, axis=-1, keepdims=True)
    h = (x * lax.rsqrt(ms + NORM_EPS)) * g_ref[...]
    o_ref[...] = jnp.dot(h.astype(BF16), w_ref[...], preferred_element_type=F32)


def _inproj(x2, g, w):
    m = x2.shape[0]
    tm = 256
    return pl.pallas_call(
        _inproj_kernel,
        grid=(m // tm,),
        in_specs=[
            pl.BlockSpec((tm, D_MODEL), lambda i: (i, 0)),
            pl.BlockSpec((1, D_MODEL), lambda i: (0, 0)),
            pl.BlockSpec((D_MODEL, NP), lambda i: (0, 0)),
        ],
        out_specs=pl.BlockSpec((tm, NP), lambda i: (i, 0)),
        out_shape=jax.ShapeDtypeStruct((m, NP), F32),
        compiler_params=pltpu.CompilerParams(
            dimension_semantics=("arbitrary",), vmem_limit_bytes=VMEM_LIMIT),
        name="inproj",
    )(x2, g, w)


def _ffn_kernel(x_ref, ya_ref, yb_ref, yc_ref, wo_ref, g_ref, wg_ref, wu_ref, wd_ref, fg_ref,
                o_ref, acc_ref, h_ref, *, n_hidden_steps, final_norm):
    j = pl.program_id(1)

    @pl.when(j == 0)
    def _():
        xn = x_ref[...]
        xn = xn + jnp.dot(ya_ref[...], wo_ref[0:512, :], preferred_element_type=F32)
        xn = xn + jnp.dot(yb_ref[...], wo_ref[512:768, :], preferred_element_type=F32)
        xn = xn + jnp.dot(yc_ref[...], wo_ref[768:1024, :], preferred_element_type=F32)
        acc_ref[...] = xn
        ms = jnp.mean(xn * xn, axis=-1, keepdims=True)
        h_ref[...] = ((xn * lax.rsqrt(ms + NORM_EPS)) * g_ref[...]).astype(BF16)

    h = h_ref[...]
    gate = jnp.dot(h, wg_ref[...], preferred_element_type=F32)
    up = jnp.dot(h, wu_ref[...], preferred_element_type=F32)
    act = (gate * _sigmoid(gate)) * up
    acc_ref[...] += jnp.dot(act.astype(BF16), wd_ref[...], preferred_element_type=F32)

    @pl.when(j == n_hidden_steps - 1)
    def _():
        y = acc_ref[...]
        if final_norm:
            ms = jnp.mean(y * y, axis=-1, keepdims=True)
            y = (y * lax.rsqrt(ms + NORM_EPS)) * fg_ref[...]
        o_ref[...] = y


def _out_ffn(x2, ya, yb, yc, wo, g, wg, wu, wd, fg, final_norm):
    m = x2.shape[0]
    tm = 512
    th = 2816
    nh = FFN_HIDDEN // th
    kern = functools.partial(_ffn_kernel, n_hidden_steps=nh, final_norm=final_norm)
    return pl.pallas_call(
        kern,
        grid=(m // tm, nh),
        in_specs=[
            pl.BlockSpec((tm, D_MODEL), lambda i, j: (i, 0)),
            pl.BlockSpec((tm, 512), lambda i, j: (i, 0)),
            pl.BlockSpec((tm, 256), lambda i, j: (i, 0)),
            pl.BlockSpec((tm, 256), lambda i, j: (i, 0)),
            pl.BlockSpec((D_MODEL, D_MODEL), lambda i, j: (0, 0), pipeline_mode=pl.Buffered(1)),
            pl.BlockSpec((1, D_MODEL), lambda i, j: (0, 0)),
            pl.BlockSpec((D_MODEL, th), lambda i, j: (0, j), pipeline_mode=pl.Buffered(1)),
            pl.BlockSpec((D_MODEL, th), lambda i, j: (0, j), pipeline_mode=pl.Buffered(1)),
            pl.BlockSpec((th, D_MODEL), lambda i, j: (j, 0), pipeline_mode=pl.Buffered(1)),
            pl.BlockSpec((1, D_MODEL), lambda i, j: (0, 0)),
        ],
        out_specs=pl.BlockSpec((tm, D_MODEL), lambda i, j: (i, 0)),
        out_shape=jax.ShapeDtypeStruct((m, D_MODEL), F32),
        scratch_shapes=[pltpu.VMEM((tm, D_MODEL), F32), pltpu.VMEM((tm, D_MODEL), BF16)],
        compiler_params=pltpu.CompilerParams(
            dimension_semantics=("arbitrary", "arbitrary"), vmem_limit_bytes=VMEM_LIMIT),
        name="out_ffn",
    )(x2, ya, yb, yc, wo, g, wg, wu, wd, fg)


def _cummax_rows(x, n):
    neg = -jnp.inf
    row = lax.broadcasted_iota(jnp.int32, x.shape, 0)
    s = 1
    while s < n:
        if s < 8:
            shifted = jnp.where(row >= s, pltpu.roll(x, s, axis=0), neg)
        else:
            shifted = jnp.concatenate([jnp.full((s, x.shape[1]), neg, x.dtype), x[:n - s]], axis=0)
        x = jnp.maximum(x, shifted)
        s *= 2
    return x


def _mlstm_kernel(q_ref, k_ref, v_ref, o_ref, gc_ref, gr_ref, bc_ref, br_ref, og_ref,
                  y_ref, sv_s, kv_s, b_s, dm_s, c_s, m_s, *, chunk, n_chunks, group):
    L = chunk
    t = pl.program_id(1)
    heads = range(MLSTM_HEADS)
    half = MLSTM_DQK

    @pl.when(t == 0)
    def _():
        c_s[...] = jnp.zeros_like(c_s)
        m_s[...] = jnp.zeros_like(m_s)

    incl = _tri(L, "incl")
    tril_b = incl.astype(BF16)
    triu_b = _tri(L, "upper_incl").astype(BF16)
    lane = lax.broadcasted_iota(jnp.int32, (L, LANES), 1)
    own = [(lane < half) if h % 2 == 0 else (lane >= half) for h in heads]
    row_c = lax.broadcasted_iota(jnp.int32, (LANES, LANES), 0)
    own_rows = [(row_c < half) if h % 2 == 0 else (row_c >= half) for h in heads]
    scale = MLSTM_DQK ** -0.5

    def pair(ref, r0, h):
        return ref[pl.ds(r0, L), (h // 2) * LANES:(h // 2 + 1) * LANES]

    def prep(cg, carry):
        ch = []
        for g in range(group):
            c = cg * group + g
            r0 = pl.multiple_of(c * L, L)
            x_col = gc_ref[pl.ds(r0, L), :] + bc_ref[...]
            x_row = gr_ref[c] + br_ref[...]
            b_row = _cumsum_lanes(-_softplus(-x_row), triu_b)
            f_rep = jnp.concatenate(
                [jnp.broadcast_to(x_col[:, 12 + h:13 + h], (L, LANES)) for h in heads], axis=1)
            b_rep = _cumsum_rows(tril_b, -_softplus(-f_rep))
            for h in heads:
                bx = b_rep[:, h * LANES:(h + 1) * LANES]
                cx = jnp.broadcast_to(x_col[:, 8 + h:9 + h], (L, LANES)) - bx
                dmax = bx + _cummax_rows(cx, L)
                c_row = x_row[8 + h:9 + h, :] - b_row[12 + h:13 + h, :]
                d_log = jnp.where(incl, bx[:, :L] + c_row - dmax[:, :L], -jnp.inf)
                kw = jnp.exp(cx + (bx[L - 1:L, :] - dmax[L - 1:L, :]))
                kp = pair(k_ref, r0, h) * scale
                vp = pair(v_ref, r0, h)
                ch.append(dict(idx=c * MLSTM_HEADS + h, h=h,
                               q=jnp.where(own[h], pair(q_ref, r0, h), 0.0), k=kp,
                               v=jnp.where(own[h], vp, 1.0), dw=jnp.exp(d_log),
                               kxw_t=(kp * kw).T, bx=bx, dmax=dmax))
        for d in ch:
            d["s"] = _mm_nt(d["q"], d["k"]) * d["dw"]
        for d in ch:
            d["sv"] = _mm(d["s"], d["v"])
        for d in ch:
            d["kv"] = jnp.where(own_rows[d["h"]], _mm(d["kxw_t"], d["v"]), 0.0)
        for d in ch:
            sv_s[d["idx"]] = d["sv"]
            kv_s[d["idx"]] = d["kv"]
            b_s[d["idx"]] = d["bx"]
            dm_s[d["idx"]] = d["dmax"]
        return carry

    lax.fori_loop(0, n_chunks // group, prep, 0)

    def scan(c, carry):
        r0 = pl.multiple_of(c * L, L)
        c_prev = [c_s[h] for h in heads]
        qc = [_mm(pair(q_ref, r0, h), c_prev[h]) for h in heads]
        outs = []
        for h in heads:
            idx = c * MLSTM_HEADS + h
            bx = b_s[idx]
            dmax = dm_s[idx]
            m_prev = m_s[h:h + 1, :]
            a_log = bx + m_prev
            m_t = jnp.maximum(a_log, dmax)
            num = jnp.exp(a_log - m_t) * qc[h] + jnp.exp(dmax - m_t) * sv_s[idx]
            den = pltpu.roll(num, half, axis=1)
            hh = jnp.where(own[h], num / jnp.maximum(jnp.abs(den), jnp.exp(-m_t)), 0.0)
            m_new = m_t[L - 1:L, :]
            dec = jnp.exp(bx[L - 1:L, :] + m_prev - m_new)
            c_s[h] = dec * c_prev[h] + jnp.exp(dmax[L - 1:L, :] - m_new) * kv_s[idx]
            m_s[h:h + 1, :] = m_new
            ms = jnp.sum(hh * hh, axis=-1, keepdims=True) * (1.0 / MLSTM_DV)
            outs.append(hh * lax.rsqrt(ms + NORM_EPS))
        y = jnp.concatenate([outs[0] + outs[1], outs[2] + outs[3]], axis=1) * og_ref[...]
        y = y * _sigmoid(o_ref[pl.ds(r0, L), :])
        y_ref[pl.ds(r0, L), :] = y.astype(y_ref.dtype)
        return carry

    lax.fori_loop(0, n_chunks, scan, 0)


def _mlstm(proj3, gates_r, bias_c, bias_r, og, *, chunk, tb):
    b, t, _ = proj3.shape
    nc = tb // chunk
    nch = nc * MLSTM_HEADS
    wq = MLSTM_HEADS * MLSTM_DQK
    group = 2 if nc % 2 == 0 else 1
    kern = functools.partial(_mlstm_kernel, chunk=chunk, n_chunks=nc, group=group)
    col = lambda c0: pl.BlockSpec((None, tb, wq), lambda i, j, c0=c0: (i, j, c0 // wq))
    return pl.pallas_call(
        kern,
        grid=(b, t // tb),
        in_specs=[
            col(C_MQ), col(C_MK), col(C_MV), col(C_MO),
            pl.BlockSpec((None, tb, LANES), lambda i, j: (i, j, C_GATE // LANES)),
            pl.BlockSpec((None, nc, 16, chunk), lambda i, j: (i, j, 0, 0)),
            pl.BlockSpec((1, LANES), lambda i, j: (0, 0)),
            pl.BlockSpec((16, 1), lambda i, j: (0, 0)),
            pl.BlockSpec((1, wq), lambda i, j: (0, 0)),
        ],
        out_specs=pl.BlockSpec((None, tb, wq), lambda i, j: (i, j, 0)),
        out_shape=jax.ShapeDtypeStruct((b, t, wq), BF16),
        scratch_shapes=[pltpu.VMEM((nch, chunk, LANES), F32),
                        pltpu.VMEM((nch, LANES, LANES), F32),
                        pltpu.VMEM((nch, chunk, LANES), F32),
                        pltpu.VMEM((nch, chunk, LANES), F32),
                        pltpu.VMEM((MLSTM_HEADS, LANES, LANES), F32),
                        pltpu.VMEM((8, LANES), F32)],
        compiler_params=pltpu.CompilerParams(
            dimension_semantics=("arbitrary", "arbitrary"), vmem_limit_bytes=VMEM_LIMIT),
        name="mlstm",
    )(proj3, proj3, proj3, proj3, proj3, gates_r, bias_c, bias_r, og)


def _split3(x):
    x1 = x.astype(BF16)
    r1 = x - x1.astype(F32)
    x2 = r1.astype(BF16)
    x3 = (r1 - x2.astype(F32)).astype(BF16)
    return x1, x2, x3


def _cumsum_rows(tril_b, x):
    return sum(jnp.dot(tril_b, p, preferred_element_type=F32) for p in _split3(x))


def _cumsum_lanes(x, triu_b):
    return sum(jnp.dot(p, triu_b, preferred_element_type=F32) for p in _split3(x))


def _inverse_masks(n):
    r = lax.broadcasted_iota(jnp.int32, (n, n), 0)
    c = lax.broadcasted_iota(jnp.int32, (n, n), 1)
    offs = []
    lvl = 0
    while (1 << lvl) < n:
        rb = r >> lvl
        cb = c >> lvl
        offs.append(((rb & 1) == 1) & (cb == rb - 1))
        lvl += 1
    return (r == c).astype(F32), offs


def _gdn_kernel(q_ref, k_ref, v_ref, z_ref, qh_ref, kh_ref, vh_ref, gc_ref, gr_ref, cw_ref,
                pc_ref, pr_ref, og_ref, y_ref, qs, ks, vs, xp_s, u_s, w_s, qk_s, qd_s, kdt_s, el_s, s_s,
                *, chunk, n_chunks, tb, group):
    C = chunk
    t = pl.program_id(1)
    width = GDN_HEADS * GDN_DK

    @pl.when(t == 0)
    def _():
        s_s[...] = jnp.zeros_like(s_s)

    def conv_silu(x_ref, halo_ref, w0):
        xp_s[0:8, :] = jnp.where(t == 0, 0.0, halo_ref[...])
        xp_s[8:, :] = x_ref[...]
        y = xp_s[8:, :] * cw_ref[GDN_CONV - 1:GDN_CONV, w0:w0 + width]
        for j in range(GDN_CONV - 1):
            y = y + xp_s[pl.ds(8 - (GDN_CONV - 1) + j, tb), :] * cw_ref[j:j + 1, w0:w0 + width]
        return y * _sigmoid(y)

    def l2n(x):
        parts = []
        for h in range(GDN_HEADS):
            xh = x[:, h * GDN_DK:(h + 1) * GDN_DK]
            parts.append(xh * lax.rsqrt(jnp.sum(xh * xh, axis=-1, keepdims=True) + NORM_EPS))
        return jnp.concatenate(parts, axis=1)

    qs[...] = l2n(conv_silu(q_ref, qh_ref, 0)) * (GDN_DK ** -0.5)
    ks[...] = l2n(conv_silu(k_ref, kh_ref, width))
    vs[...] = conv_silu(v_ref, vh_ref, 2 * width)

    incl = _tri(C, "incl")
    strict = _tri(C, "strict")
    tril_b = incl.astype(BF16)
    triu_b = _tri(C, "upper_incl").astype(BF16)
    inv_masks = _inverse_masks(C)

    def prep(cg, carry):
        loaded = []
        for g in range(group):
            c = cg * group + g
            r0 = pl.multiple_of(c * C, C)
            gcol = gc_ref[pl.ds(r0, C), :]
            grow = gr_ref[c]
            qkv = [(qs[pl.ds(r0, C), h * GDN_DK:(h + 1) * GDN_DK],
                    ks[pl.ds(r0, C), h * GDN_DK:(h + 1) * GDN_DK],
                    vs[pl.ds(r0, C), h * GDN_DK:(h + 1) * GDN_DK]) for h in range(GDN_HEADS)]
            loaded.append((c, gcol, grow, qkv))
        ch = []
        for c, gcol, grow, qkv in loaded:
            g_col = -jnp.exp(pc_ref[0:1, :]) * _softplus(gcol + pc_ref[1:2, :])
            gcum_col = _cumsum_rows(tril_b, g_col)
            beta_col = _sigmoid(gcol)
            g_row = -jnp.exp(pr_ref[:, 0:1]) * _softplus(grow + pr_ref[:, 1:2])
            gcum_row = _cumsum_lanes(g_row, triu_b)
            for h in range(GDN_HEADS):
                q, k, v = qkv[h]
                gc_c = gcum_col[:, h:h + 1]
                gc_r = gcum_row[h:h + 1, :]
                beta = beta_col[:, 4 + h:5 + h]
                decay = jnp.where(incl, jnp.exp(jnp.where(incl, gc_c - gc_r, 0.0)), 0.0)
                ch.append(dict(idx=c * GDN_HEADS + h, q=q, k=k, v=v, gc_c=gc_c, beta=beta,
                               decay=decay, kb=k * beta))
        for d in ch:
            d["a"] = jnp.where(strict, _mm_nt(d["kb"], d["k"]) * d["decay"], 0.0)
        eye, offs = inv_masks
        for d in ch:
            d["t"] = eye - jnp.where(offs[0], d["a"], 0.0)
        for off in offs[1:]:
            for d in ch:
                d["p"] = _mm(d["t"], jnp.where(off, d["a"], 0.0))
            for d in ch:
                d["t"] = d["t"] - _mm(d["p"], d["t"])
        for d in ch:
            d["egc"] = jnp.exp(d["gc_c"])
            d["u"] = _mm(d["t"], d["v"] * d["beta"])
        for d in ch:
            d["w"] = _mm(d["t"], d["kb"] * d["egc"]).astype(BF16)
        for d in ch:
            d["qk"] = (_mm_nt(d["q"], d["k"]) * d["decay"]).astype(BF16)
        for d in ch:
            idx = d["idx"]
            g_last = d["gc_c"][C - 1:C, :]
            u_s[idx] = d["u"]
            w_s[idx] = d["w"]
            qk_s[idx] = d["qk"]
            qd_s[idx] = (d["q"] * d["egc"]).astype(BF16)
            kdt_s[idx] = (d["k"] * jnp.exp(g_last - d["gc_c"])).T.astype(BF16)
            el_s[idx] = jnp.broadcast_to(jnp.exp(g_last), (8, LANES))
        return carry

    lax.fori_loop(0, n_chunks // group, prep, 0)

    def scan(c, carry):
        r0 = pl.multiple_of(c * C, C)
        heads = range(GDN_HEADS)
        dot = functools.partial(jnp.dot, preferred_element_type=F32)
        idx = [c * GDN_HEADS + h for h in heads]
        s_prev = [s_s[h] for h in heads]
        s_b = [s.astype(BF16) for s in s_prev]
        ws = [dot(w_s[idx[h]], s_b[h]) for h in heads]
        qs_ = [dot(qd_s[idx[h]], s_b[h]) for h in heads]
        v_b = [(u_s[idx[h]] - ws[h]).astype(BF16) for h in heads]
        upd = [dot(kdt_s[idx[h]], v_b[h]) for h in heads]
        o_all = [qs_[h] + dot(qk_s[idx[h]], v_b[h]) for h in heads]
        outs = []
        for h in heads:
            s_s[h] = s_prev[h] * el_s[idx[h]][0:1, :] + upd[h]
            o = o_all[h]
            ms = jnp.mean(o * o, axis=-1, keepdims=True)
            outs.append(o * lax.rsqrt(ms + NORM_EPS))
        y = jnp.concatenate(outs, axis=1) * og_ref[...]
        z = z_ref[pl.ds(r0, C), :]
        y = y * (z * _sigmoid(z))
        y_ref[pl.ds(r0, C), :] = y.astype(y_ref.dtype)
        return carry

    lax.fori_loop(0, n_chunks, scan, 0)


def _gdn(proj3, gates_r, conv_w, par_c, par_r, og, *, chunk, tb):
    b, t, _ = proj3.shape
    nc = tb // chunk
    nch = nc * GDN_HEADS
    width = GDN_HEADS * GDN_DK
    group = 4 if nc % 4 == 0 else (2 if nc % 2 == 0 else 1)
    kern = functools.partial(_gdn_kernel, chunk=chunk, n_chunks=nc, tb=tb, group=group)
    col = lambda c0: pl.BlockSpec((None, tb, width), lambda i, j, c0=c0: (i, j, c0 // width))
    halo = lambda c0: pl.BlockSpec(
        (None, 8, width), lambda i, j, c0=c0: (i, jnp.maximum(j * (tb // 8) - 1, 0), c0 // width))
    return pl.pallas_call(
        kern,
        grid=(b, t // tb),
        in_specs=[
            col(C_GQ), col(C_GK), col(C_GV), col(C_GZ),
            halo(C_GQ), halo(C_GK), halo(C_GV),
            pl.BlockSpec((None, tb, LANES), lambda i, j: (i, j, C_GATE // LANES)),
            pl.BlockSpec((None, nc, 16, chunk), lambda i, j: (i, j, 0, 0)),
            pl.BlockSpec((GDN_CONV, 3 * width), lambda i, j: (0, 0)),
            pl.BlockSpec((2, LANES), lambda i, j: (0, 0)),
            pl.BlockSpec((16, 2), lambda i, j: (0, 0)),
            pl.BlockSpec((1, width), lambda i, j: (0, 0)),
        ],
        out_specs=pl.BlockSpec((None, tb, width), lambda i, j: (i, j, 0)),
        out_shape=jax.ShapeDtypeStruct((b, t, width), BF16),
        scratch_shapes=[pltpu.VMEM((tb, width), F32), pltpu.VMEM((tb, width), F32),
                        pltpu.VMEM((tb, width), F32),
                        pltpu.VMEM((tb + 8, width), F32),
                        pltpu.VMEM((nch, chunk, GDN_DV), F32),
                        pltpu.VMEM((nch, chunk, GDN_DK), BF16),
                        pltpu.VMEM((nch, chunk, chunk), BF16),
                        pltpu.VMEM((nch, chunk, GDN_DK), BF16),
                        pltpu.VMEM((nch, GDN_DK, chunk), BF16),
                        pltpu.VMEM((nch, 8, LANES), F32),
                        pltpu.VMEM((GDN_HEADS, GDN_DK, GDN_DV), F32)],
        compiler_params=pltpu.CompilerParams(
            dimension_semantics=("arbitrary", "arbitrary"), vmem_limit_bytes=VMEM_LIMIT),
        name="gdn",
    )(proj3, proj3, proj3, proj3, proj3, proj3, proj3, proj3, gates_r, conv_w, par_c, par_r, og)


def _dsa_prep_kernel(cq_ref, ckv_ref, idx_ref, qn_ref, kn_ref, wuq_ref, wqi_ref, wuk_ref,
                     qlat_ref, qidx_ref, ckvn_ref, ckvt_ref, kidx_ref):
    cq = cq_ref[...]
    ms = jnp.mean(cq * cq, axis=-1, keepdims=True)
    cqn = ((cq * lax.rsqrt(ms + NORM_EPS)) * qn_ref[...]).astype(BF16)
    ckv = ckv_ref[...]
    ms = jnp.mean(ckv * ckv, axis=-1, keepdims=True)
    ckvn = (ckv * lax.rsqrt(ms + NORM_EPS)) * kn_ref[...]
    ckvn_ref[...] = ckvn.astype(BF16)
    ckvt_ref[...] = ckvn.T.astype(BF16)
    q = jnp.dot(cqn, wuq_ref[...], preferred_element_type=F32)
    qlat = jnp.dot(q.astype(BF16), wuk_ref[...], preferred_element_type=F32) * (DSA_HEAD_DIM ** -0.5)
    qlat_ref[...] = qlat.astype(BF16)
    qidx_ref[...] = jnp.dot(cqn, wqi_ref[...], preferred_element_type=F32).astype(BF16)
    kidx_ref[...] = idx_ref[:, 0:IDX_DIM].astype(BF16)


def _dsa_prep(proj2, qn, kn, wuq, wqi, wuk_bd):
    m = proj2.shape[0]
    tm = 512
    full = lambda shape: pl.BlockSpec(shape, lambda i: (0, 0))
    return pl.pallas_call(
        _dsa_prep_kernel,
        grid=(m // tm,),
        in_specs=[
            pl.BlockSpec((tm, DSA_Q_RANK), lambda i: (i, C_CQ // DSA_Q_RANK)),
            pl.BlockSpec((tm, DSA_KV_RANK), lambda i: (i, C_CKV // DSA_KV_RANK)),
            pl.BlockSpec((tm, LANES), lambda i: (i, C_IDX // LANES)),
            full((1, DSA_Q_RANK)), full((1, DSA_KV_RANK)),
            full((DSA_Q_RANK, DSA_HEADS * DSA_HEAD_DIM)),
            full((DSA_Q_RANK, IDX_HEADS * IDX_DIM)),
            full((DSA_HEADS * DSA_HEAD_DIM, DSA_HEADS * DSA_KV_RANK)),
        ],
        out_specs=[
            pl.BlockSpec((tm, DSA_HEADS * DSA_KV_RANK), lambda i: (i, 0)),
            pl.BlockSpec((tm, IDX_HEADS * IDX_DIM), lambda i: (i, 0)),
            pl.BlockSpec((tm, DSA_KV_RANK), lambda i: (i, 0)),
            pl.BlockSpec((DSA_KV_RANK, tm), lambda i: (0, i)),
            pl.BlockSpec((tm, IDX_DIM), lambda i: (i, 0)),
        ],
        out_shape=[
            jax.ShapeDtypeStruct((m, DSA_HEADS * DSA_KV_RANK), BF16),
            jax.ShapeDtypeStruct((m, IDX_HEADS * IDX_DIM), BF16),
            jax.ShapeDtypeStruct((m, DSA_KV_RANK), BF16),
            jax.ShapeDtypeStruct((DSA_KV_RANK, m), BF16),
            jax.ShapeDtypeStruct((m, IDX_DIM), BF16),
        ],
        compiler_params=pltpu.CompilerParams(
            dimension_semantics=("arbitrary",), vmem_limit_bytes=VMEM_LIMIT),
        name="dsa_prep",
    )(proj2, proj2, proj2, qn, kn, wuq, wqi, wuk_bd)


def _tree_sum(parts):
    while len(parts) > 1:
        parts = [parts[j] + parts[j + 1] for j in range(0, len(parts) - 1, 2)] + (
            [parts[-1]] if len(parts) % 2 else [])
    return parts[0]


def _dsa_kernel(qidx_ref, wrow_ref, qlat_ref, kidx_ref, ckv_ref, ckvt_ref, wuvt_ref, y_ref,
                key_s, hi_s, lo_s, cut_s, acc_s, *, qb, tk, n_sel):
    i = pl.program_id(1)
    n_tiles = (i * qb + qb + tk - 1) // tk
    key_pos = lax.broadcasted_iota(jnp.int32, (tk, qb), 0)
    q_pos = i * qb + lax.broadcasted_iota(jnp.int32, (tk, qb), 1)
    w_rows = wrow_ref[...] * (IDX_HEADS ** -0.5 * IDX_DIM ** -0.5)
    qidx = qidx_ref[...]
    q_heads = [qidx[:, h * IDX_DIM:(h + 1) * IDX_DIM] for h in range(IDX_HEADS)]
    i16_min = -(1 << 15)
    i16_max = (1 << 15) - 1
    sub16 = 16
    nt_dims = (((1,), (1,)), ((), ()))

    def score_body(kt, carry):
        k0 = pl.multiple_of(kt * tk, tk)
        kk = kidx_ref[pl.ds(k0, tk), :]
        sh = [lax.dot_general(kk, q_heads[h], nt_dims, preferred_element_type=F32)
              for h in range(IDX_HEADS)]
        acc = jnp.zeros((tk, qb), F32)
        for h in range(IDX_HEADS):
            acc = acc + jnp.maximum(sh[h], 0.0) * w_rows[h:h + 1, :]
        acc = acc + 0.0
        bits = pltpu.bitcast(acc, jnp.int32)
        keys = jnp.where(bits < 0, bits ^ jnp.int32(0x7FFFFFFF), bits)
        keys = jnp.where(k0 + key_pos <= q_pos, keys, jnp.int32(INT_MIN))
        key_s[kt] = keys
        hi_s[kt] = (keys >> 16).astype(jnp.int16)
        return carry

    lax.fori_loop(0, n_tiles, score_body, 0)

    def count_ge(src_s, cand):
        cand16 = jnp.broadcast_to(cand, (sub16, qb)).astype(jnp.int16)
        one, zero = jnp.int16(1), jnp.int16(0)

        def body(kt, acc):
            tile = src_s[kt]
            return acc + _tree_sum([jnp.where(tile[j * sub16:(j + 1) * sub16, :] >= cand16, one, zero)
                                    for j in range(tk // sub16)])

        cnt = lax.fori_loop(0, n_tiles, body, jnp.zeros((sub16, qb), jnp.int16))
        return jnp.sum(cnt.astype(jnp.int32), axis=0, keepdims=True)

    def kth_largest16(src_s, rank):
        thr = jnp.full((1, qb), i16_min, jnp.int32)
        n_ge = jnp.zeros((1, qb), jnp.int32) + n_tiles * tk
        n_gt = jnp.zeros((1, qb), jnp.int32)
        for bit in range(15, -1, -1):
            cand = thr + jnp.int32(1 << bit)
            cnt = count_ge(src_s, cand)
            ok = cnt >= rank
            thr = jnp.where(ok, cand, thr)
            n_ge = jnp.where(ok, cnt, n_ge)
            n_gt = jnp.where(ok, n_gt, cnt)
        return thr, n_ge, n_gt

    hi_thr, hi_ge, above = kth_largest16(hi_s, jnp.int32(n_sel))
    rank_lo = jnp.int32(n_sel) - above

    def lo_body(kt, carry):
        keys = key_s[kt]
        lo = (keys & jnp.int32(0xFFFF)) + jnp.int32(i16_min)
        lo_s[kt] = jnp.where((keys >> 16) == hi_thr, lo, jnp.int32(i16_min)).astype(jnp.int16)
        return carry

    lax.fori_loop(0, n_tiles, lo_body, 0)
    lo_thr, lo_ge, _ = kth_largest16(lo_s, rank_lo)
    thr_raw = hi_thr * jnp.int32(1 << 16) + (lo_thr - jnp.int32(i16_min))
    thr = jnp.maximum(thr_raw, jnp.int32(INT_MIN + 1))

    in_bucket = jnp.where(lo_thr > i16_min, lo_ge, hi_ge - above)
    excess = jnp.where(thr_raw > INT_MIN, above + in_bucket - jnp.int32(n_sel), 0)

    max_excess = jnp.max(excess)
    last = tk * key_s.shape[0] - 1
    assert last <= i16_max
    direct_limit = max(last, 1).bit_length()

    @pl.when(max_excess > 0)
    def _():
        def fill(kt, carry):
            rev = jnp.int32(last) - (kt * tk + key_pos)
            lo_s[kt] = jnp.where(key_s[kt] == thr_raw, rev, jnp.int32(-1)).astype(jnp.int16)
            return carry

        lax.fori_loop(0, n_tiles, fill, 0)

        @pl.when(max_excess <= direct_limit)
        def _():
            def next_tied(prev):
                prev16 = jnp.broadcast_to(prev, (sub16, qb)).astype(jnp.int16)
                big = jnp.int16(i16_max)

                def body(kt, acc):
                    tile = lo_s[kt]
                    parts = [jnp.where(tile[j * sub16:(j + 1) * sub16, :] > prev16,
                                       tile[j * sub16:(j + 1) * sub16, :], big)
                             for j in range(tk // sub16)]
                    while len(parts) > 1:
                        parts = [jnp.where(parts[j] < parts[j + 1], parts[j], parts[j + 1])
                                 for j in range(0, len(parts), 2)]
                    return jnp.where(acc < parts[0], acc, parts[0])

                low = lax.fori_loop(0, n_tiles, body, jnp.full((sub16, qb), i16_max, jnp.int16))
                return jnp.min(low.astype(jnp.int32), axis=0, keepdims=True)

            def step(j, carry):
                prev, cut = carry
                cur = next_tied(prev)
                return cur, jnp.where(j < excess, cur + 1, cut)

            _, cut = lax.fori_loop(0, max_excess, step,
                                   (jnp.full((1, qb), -1, jnp.int32), jnp.zeros((1, qb), jnp.int32)))
            cut_s[0:1, :] = cut

        @pl.when(max_excess > direct_limit)
        def _():
            keep = count_ge(lo_s, jnp.zeros((1, qb), jnp.int32)) - excess
            cut = jnp.zeros((1, qb), jnp.int32)
            for bit in range(direct_limit - 1, -1, -1):
                cand = cut + jnp.int32(1 << bit)
                cut = jnp.where(count_ge(lo_s, cand) >= keep, cand, cut)
            cut_s[0:1, :] = jnp.where(excess > 0, cut, 0)

        cut = cut_s[0:1, :]

        def demote(kt, carry):
            keys = key_s[kt]
            rev = jnp.int32(last) - (kt * tk + key_pos)
            demoted = jnp.where(rev < cut, jnp.int32(INT_MIN), keys)
            key_s[kt] = jnp.where(keys == thr_raw, demoted, keys)
            return carry

        lax.fori_loop(0, n_tiles, demote, 0)

    qlat = qlat_ref[...]
    heads = range(DSA_HEADS)
    ql = [qlat[:, h * DSA_KV_RANK:(h + 1) * DSA_KV_RANK] for h in heads]
    acc_s[...] = jnp.zeros_like(acc_s)

    def attn_body(kt, carry):
        m_run, l_run = carry
        k0 = pl.multiple_of(kt * tk, tk)
        ckv = ckv_ref[pl.ds(k0, tk), :]
        ckvt = ckvt_ref[:, pl.ds(k0, tk)]
        bias = jnp.where(key_s[kt] >= thr, 0.0, -jnp.inf)
        lg = [lax.dot_general(ckv, ql[h], nt_dims, preferred_element_type=F32) for h in heads]
        m_out, l_out, p_all, alphas = [], [], [], []
        for h in heads:
            lgm = lg[h] + bias
            m_new = jnp.maximum(m_run[h], jnp.max(lgm, axis=0, keepdims=True))
            p = jnp.exp(lgm - m_new)
            alpha = jnp.exp(m_run[h] - m_new)
            l_out.append(alpha * l_run[h] + jnp.sum(p, axis=0, keepdims=True))
            m_out.append(m_new)
            p_all.append(p.astype(BF16))
            alphas.append(alpha)
        pv = [jnp.dot(ckvt, p_all[h], preferred_element_type=F32) for h in heads]
        for h in heads:
            acc_s[h] = alphas[h] * acc_s[h] + pv[h]
        return tuple(m_out), tuple(l_out)

    init = (tuple(jnp.full((1, qb), -1e30, F32) for _ in heads),
            tuple(jnp.zeros((1, qb), F32) for _ in heads))
    _, l_fin = lax.fori_loop(0, n_tiles, attn_body, init)
    o_lat = [(acc_s[h] / l_fin[h]).astype(BF16) for h in heads]
    outs = [jnp.dot(wuvt_ref[h], o_lat[h], preferred_element_type=F32) for h in heads]
    y_ref[...] = jnp.concatenate(outs, axis=0).T.astype(y_ref.dtype)


def _dsa(qidx, widx_rows, qlat, kidx, ckvn, ckvt, wuvt, *, qb, tk):
    b, t, _ = qidx.shape
    n_sel = min(IDX_TOPK, t // 4)
    kern = functools.partial(_dsa_kernel, qb=qb, tk=tk, n_sel=n_sel)
    return pl.pallas_call(
        kern,
        grid=(b, t // qb),
        in_specs=[
            pl.BlockSpec((None, qb, IDX_HEADS * IDX_DIM), lambda i, j: (i, j, 0)),
            pl.BlockSpec((None, IDX_HEADS, qb), lambda i, j: (i, 0, j)),
            pl.BlockSpec((None, qb, DSA_HEADS * DSA_KV_RANK), lambda i, j: (i, j, 0)),
            pl.BlockSpec((None, t, IDX_DIM), lambda i, j: (i, 0, 0)),
            pl.BlockSpec((None, t, DSA_KV_RANK), lambda i, j: (i, 0, 0)),
            pl.BlockSpec((DSA_KV_RANK, t), lambda i, j: (0, i)),
            pl.BlockSpec((DSA_HEADS, DSA_HEAD_DIM, DSA_KV_RANK), lambda i, j: (0, 0, 0)),
        ],
        out_specs=pl.BlockSpec((None, qb, DSA_HEADS * DSA_HEAD_DIM), lambda i, j: (i, j, 0)),
        out_shape=jax.ShapeDtypeStruct((b, t, DSA_HEADS * DSA_HEAD_DIM), BF16),
        scratch_shapes=[pltpu.VMEM((t // tk, tk, qb), jnp.int32),
                        pltpu.VMEM((t // tk, tk, qb), jnp.int16),
                        pltpu.VMEM((t // tk, tk, qb), jnp.int16),
                        pltpu.VMEM((8, qb), jnp.int32),
                        pltpu.VMEM((DSA_HEADS, DSA_KV_RANK, qb), F32)],
        compiler_params=pltpu.CompilerParams(
            dimension_semantics=("arbitrary", "arbitrary"), vmem_limit_bytes=VMEM_LIMIT),
        name="dsa",
    )(qidx, widx_rows, qlat, kidx, ckvn, ckvt, wuvt)


def _pack_w_in(w):
    pts = [0]
    for wd in IN_WIDTHS:
        pts.append(pts[-1] + wd)
    seg = [w[..., pts[i]:pts[i + 1]] for i in range(len(IN_WIDTHS))]
    (gq, gk, gv, gz, ga, gb, cq, ckv, ik, iw, mq, mk, mv, mo, mi, mf) = seg
    zeros = lambda n: jnp.zeros(w.shape[:-1] + (n,), w.dtype)
    idx_blk = jnp.concatenate([ik, iw, zeros(LANES - IDX_DIM - IDX_HEADS)], axis=-1)
    gate_blk = jnp.concatenate([ga, gb, mi, mf, zeros(LANES - 16)], axis=-1)
    out = jnp.concatenate([gq, gk, gv, gz, cq, ckv, idx_blk, mq, mk, mv, mo, gate_blk], axis=-1)
    assert out.shape[-1] == NP
    return out.astype(BF16)


def _lane_rows(vals_at, depth):
    rows = jnp.zeros((depth, LANES), F32)
    for start, v in vals_at:
        rows = rows.at[:, start:start + v.shape[1]].set(v.astype(F32))
    return rows


def kernel(x, attn_norm, w_in, gdn_conv, gdn_a_log, gdn_dt_bias, gdn_out_norm, dsa_q_norm,
           dsa_kv_norm, dsa_w_uq, dsa_w_qidx, dsa_w_uk, dsa_w_uv, mlstm_i_bias, mlstm_f_bias,
           mlstm_out_norm, w_out, ffn_norm, w_gate, w_up, w_down, final_norm):
    b, t, d = x.shape
    depth = w_in.shape[0]
    gdn_chunk, mlstm_chunk = min(128, t), min(64, t)
    tb = min(512, t)
    qb = min(256, t)
    tk = min(512, t)

    w_in_p = _pack_w_in(w_in)
    gdn_par_c = jnp.stack([_lane_rows([(0, gdn_a_log)], depth), _lane_rows([(0, gdn_dt_bias)], depth)],
                          axis=1)
    gdn_par_r = jnp.swapaxes(gdn_par_c[:, :, :16], 1, 2)
    gdn_og = jnp.tile(gdn_out_norm, (1, GDN_HEADS))
    wuk_bd = jnp.zeros((depth, DSA_HEADS * DSA_HEAD_DIM, DSA_HEADS * DSA_KV_RANK), F32)
    for h in range(DSA_HEADS):
        wuk_bd = wuk_bd.at[:, h * DSA_HEAD_DIM:(h + 1) * DSA_HEAD_DIM,
                           h * DSA_KV_RANK:(h + 1) * DSA_KV_RANK].set(jnp.swapaxes(dsa_w_uk[:, h], 1, 2))
    wuk_bd = wuk_bd.astype(BF16)
    wuq_b, wqi_b = dsa_w_uq.astype(BF16), dsa_w_qidx.astype(BF16)
    wuvt_b = jnp.swapaxes(dsa_w_uv, 2, 3).astype(BF16)
    ml_bias_c = _lane_rows([(8, mlstm_i_bias), (12, mlstm_f_bias)], depth)
    ml_og = jnp.tile(mlstm_out_norm, (1, MLSTM_HEADS))
    w_out_b, w_gate_b, w_up_b, w_down_b = (a.astype(BF16) for a in (w_out, w_gate, w_up, w_down))

    x2 = x.reshape(b * t, d)
    for l in range(depth):
        proj2 = _inproj(x2, attn_norm[l].reshape(1, d), w_in_p[l])
        proj3 = proj2.reshape(b, t, NP)
        gates_t = jnp.swapaxes(proj3[:, :, C_GATE:C_GATE + 16], 1, 2)

        def chunk_rows(c):
            return jnp.swapaxes(gates_t.reshape(b, 16, t // c, c), 1, 2)

        y_a = _gdn(proj3, chunk_rows(gdn_chunk), gdn_conv[l], gdn_par_c[l], gdn_par_r[l],
                   gdn_og[l].reshape(1, -1), chunk=gdn_chunk, tb=tb)

        qlat, qidx, ckvn, ckvt, kidx = _dsa_prep(
            proj2, dsa_q_norm[l].reshape(1, -1), dsa_kv_norm[l].reshape(1, -1),
            wuq_b[l], wqi_b[l], wuk_bd[l])
        r3 = lambda a: a.reshape(b, t, a.shape[-1])
        widx_rows = jnp.swapaxes(proj3[:, :, C_IDX + IDX_DIM:C_IDX + IDX_DIM + IDX_HEADS], 1, 2)
        y_b = _dsa(r3(qidx), widx_rows, r3(qlat), r3(kidx), r3(ckvn), ckvt, wuvt_b[l], qb=qb, tk=tk)

        y_c = _mlstm(proj3, chunk_rows(mlstm_chunk), ml_bias_c[l].reshape(1, LANES),
                     ml_bias_c[l, :16].reshape(16, 1), ml_og[l].reshape(1, -1),
                     chunk=mlstm_chunk, tb=tb)

        x2 = _out_ffn(x2, y_a.reshape(b * t, -1), y_b.reshape(b * t, -1), y_c.reshape(b * t, -1),
                      w_out_b[l], ffn_norm[l].reshape(1, d), w_gate_b[l], w_up_b[l], w_down_b[l],
                      final_norm.reshape(1, d), final_norm=(l == depth - 1))
    return x2.reshape(b, t, d)
```

```python
import functools
import math

import jax
import jax.numpy as jnp
from jax import lax
from jax.experimental import pallas as pl
from jax.experimental.pallas import tpu as pltpu

F32 = jnp.float32
BF16 = jnp.bfloat16
HIGHEST = lax.Precision.HIGHEST

D_MODEL = 1024
GDN_HEADS = 4
GDN_DK = 128
GDN_DV = 128
GDN_CONV = 4
DSA_HEADS = 4
DSA_HEAD_DIM = 64
DSA_Q_RANK = 256
DSA_KV_RANK = 128
IDX_HEADS = 8
IDX_DIM = 32
IDX_TOPK = 256
MLSTM_HEADS = 4
MLSTM_DQK = 64
MLSTM_DV = 64
FFN_HIDDEN = 2816
NORM_EPS = 1e-6

IN_WIDTHS = (512, 512, 512, 512, 4, 4, 256, 128, 32, 8, 256, 256, 256, 256, 4, 4)

C_GQ, C_GK, C_GV, C_GZ = 0, 512, 1024, 1536
C_CQ, C_CKV, C_IDX = 2048, 2304, 2432
C_MQ, C_MK, C_MV, C_MO = 2560, 2816, 3072, 3328
C_GATE = 3584
NP = 3712
LANES = 128

VMEM_LIMIT = 56 * 1024 * 1024

INT_MIN = -2 ** 31


def _mm(a, b):
    return jnp.dot(a.astype(BF16), b.astype(BF16), preferred_element_type=F32)


def _mm_nt(a, b):
    return lax.dot_general(a.astype(BF16), b.astype(BF16), (((1,), (1,)), ((), ())),
                           preferred_element_type=F32)


def _sigmoid(x):
    return 1.0 / (1.0 + jnp.exp(-x))


def _softplus(x):
    return jnp.maximum(x, 0.0) + jnp.log1p(jnp.exp(-jnp.abs(x)))


def _tri(n, kind):
    r = lax.broadcasted_iota(jnp.int32, (n, n), 0)
    c = lax.broadcasted_iota(jnp.int32, (n, n), 1)
    if kind == "incl":
        return r >= c
    if kind == "strict":
        return r > c
    if kind == "upper_incl":
        return r <= c
    raise ValueError(kind)


def _inproj_kernel(x_ref, g_ref, w_ref, o_ref):
    x = x_ref[...]
    ms = jnp.mean(x * x, axis=-1, keepdims=True)
    h = (x * lax.rsqrt(ms + NORM_EPS)) * g_ref[...]
    o_ref[...] = jnp.dot(h.astype(BF16), w_ref[...], preferred_element_type=F32)


def _inproj(x2, g, w):
    m = x2.shape[0]
    tm = 256
    return pl.pallas_call(
        _inproj_kernel,
        grid=(m // tm,),
        in_specs=[
            pl.BlockSpec((tm, D_MODEL), lambda i: (i, 0)),
            pl.BlockSpec((1, D_MODEL), lambda i: (0, 0)),
            pl.BlockSpec((D_MODEL, NP), lambda i: (0, 0)),
        ],
        out_specs=pl.BlockSpec((tm, NP), lambda i: (i, 0)),
        out_shape=jax.ShapeDtypeStruct((m, NP), F32),
        compiler_params=pltpu.CompilerParams(
            dimension_semantics=("arbitrary",), vmem_limit_bytes=VMEM_LIMIT),
        name="inproj",
    )(x2, g, w)


def _ffn_kernel(x_ref, ya_ref, yb_ref, yc_ref, wo_ref, g_ref, wg_ref, wu_ref, wd_ref, fg_ref,
                o_ref, acc_ref, h_ref, *, n_hidden_steps, final_norm):
    j = pl.program_id(1)

    @pl.when(j == 0)
    def _():
        xn = x_ref[...]
        xn = xn + jnp.dot(ya_ref[...], wo_ref[0:512, :], preferred_element_type=F32)
        xn = xn + jnp.dot(yb_ref[...], wo_ref[512:768, :], preferred_element_type=F32)
        xn = xn + jnp.dot(yc_ref[...], wo_ref[768:1024, :], preferred_element_type=F32)
        acc_ref[...] = xn
        ms = jnp.mean(xn * xn, axis=-1, keepdims=True)
        h_ref[...] = ((xn * lax.rsqrt(ms + NORM_EPS)) * g_ref[...]).astype(BF16)

    h = h_ref[...]
    gate = jnp.dot(h, wg_ref[...], preferred_element_type=F32)
    up = jnp.dot(h, wu_ref[...], preferred_element_type=F32)
    act = (gate * _sigmoid(gate)) * up
    acc_ref[...] += jnp.dot(act.astype(BF16), wd_ref[...], preferred_element_type=F32)

    @pl.when(j == n_hidden_steps - 1)
    def _():
        y = acc_ref[...]
        if final_norm:
            ms = jnp.mean(y * y, axis=-1, keepdims=True)
            y = (y * lax.rsqrt(ms + NORM_EPS)) * fg_ref[...]
        o_ref[...] = y


def _out_ffn(x2, ya, yb, yc, wo, g, wg, wu, wd, fg, final_norm):
    m = x2.shape[0]
    tm = 512
    th = 2816
    nh = FFN_HIDDEN // th
    kern = functools.partial(_ffn_kernel, n_hidden_steps=nh, final_norm=final_norm)
    return pl.pallas_call(
        kern,
        grid=(m // tm, nh),
        in_specs=[
            pl.BlockSpec((tm, D_MODEL), lambda i, j: (i, 0)),
            pl.BlockSpec((tm, 512), lambda i, j: (i, 0)),
            pl.BlockSpec((tm, 256), lambda i, j: (i, 0)),
            pl.BlockSpec((tm, 256), lambda i, j: (i, 0)),
            pl.BlockSpec((D_MODEL, D_MODEL), lambda i, j: (0, 0), pipeline_mode=pl.Buffered(1)),
            pl.BlockSpec((1, D_MODEL), lambda i, j: (0, 0)),
            pl.BlockSpec((D_MODEL, th), lambda i, j: (0, j), pipeline_mode=pl.Buffered(1)),
            pl.BlockSpec((D_MODEL, th), lambda i, j: (0, j), pipeline_mode=pl.Buffered(1)),
            pl.BlockSpec((th, D_MODEL), lambda i, j: (j, 0), pipeline_mode=pl.Buffered(1)),
            pl.BlockSpec((1, D_MODEL), lambda i, j: (0, 0)),
        ],
        out_specs=pl.BlockSpec((tm, D_MODEL), lambda i, j: (i, 0)),
        out_shape=jax.ShapeDtypeStruct((m, D_MODEL), F32),
        scratch_shapes=[pltpu.VMEM((tm, D_MODEL), F32), pltpu.VMEM((tm, D_MODEL), BF16)],
        compiler_params=pltpu.CompilerParams(
            dimension_semantics=("arbitrary", "arbitrary"), vmem_limit_bytes=VMEM_LIMIT),
        name="out_ffn",
    )(x2, ya, yb, yc, wo, g, wg, wu, wd, fg)


def _cummax_rows(x, n):
    neg = -jnp.inf
    row = lax.broadcasted_iota(jnp.int32, x.shape, 0)
    s = 1
    while s < n:
        if s < 8:
            shifted = jnp.where(row >= s, pltpu.roll(x, s, axis=0), neg)
        else:
            shifted = jnp.concatenate([jnp.full((s, x.shape[1]), neg, x.dtype), x[:n - s]], axis=0)
        x = jnp.maximum(x, shifted)
        s *= 2
    return x


def _mlstm_kernel(q_ref, k_ref, v_ref, o_ref, gc_ref, gr_ref, bc_ref, br_ref, og_ref,
                  y_ref, sv_s, kv_s, b_s, dm_s, c_s, m_s, *, chunk, n_chunks, group):
    L = chunk
    t = pl.program_id(1)
    heads = range(MLSTM_HEADS)
    half = MLSTM_DQK

    @pl.when(t == 0)
    def _():
        c_s[...] = jnp.zeros_like(c_s)
        m_s[...] = jnp.zeros_like(m_s)

    incl = _tri(L, "incl")
    tril_b = incl.astype(BF16)
    triu_b = _tri(L, "upper_incl").astype(BF16)
    lane = lax.broadcasted_iota(jnp.int32, (L, LANES), 1)
    own = [(lane < half) if h % 2 == 0 else (lane >= half) for h in heads]
    row_c = lax.broadcasted_iota(jnp.int32, (LANES, LANES), 0)
    own_rows = [(row_c < half) if h % 2 == 0 else (row_c >= half) for h in heads]
    scale = MLSTM_DQK ** -0.5

    def pair(ref, r0, h):
        return ref[pl.ds(r0, L), (h // 2) * LANES:(h // 2 + 1) * LANES]

    def prep(cg, carry):
        ch = []
        for g in range(group):
            c = cg * group + g
            r0 = pl.multiple_of(c * L, L)
            x_col = gc_ref[pl.ds(r0, L), :] + bc_ref[...]
            x_row = gr_ref[c] + br_ref[...]
            b_row = _cumsum_lanes(-_softplus(-x_row), triu_b)
            f_rep = jnp.concatenate(
                [jnp.broadcast_to(x_col[:, 12 + h:13 + h], (L, LANES)) for h in heads], axis=1)
            b_rep = _cumsum_rows(tril_b, -_softplus(-f_rep))
            for h in heads:
                bx = b_rep[:, h * LANES:(h + 1) * LANES]
                cx = jnp.broadcast_to(x_col[:, 8 + h:9 + h], (L, LANES)) - bx
                dmax = bx + _cummax_rows(cx, L)
                c_row = x_row[8 + h:9 + h, :] - b_row[12 + h:13 + h, :]
                d_log = jnp.where(incl, bx[:, :L] + c_row - dmax[:, :L], -jnp.inf)
                kw = jnp.exp(cx + (bx[L - 1:L, :] - dmax[L - 1:L, :]))
                kp = pair(k_ref, r0, h) * scale
                vp = pair(v_ref, r0, h)
                ch.append(dict(idx=c * MLSTM_HEADS + h, h=h,
                               q=jnp.where(own[h], pair(q_ref, r0, h), 0.0), k=kp,
                               v=jnp.where(own[h], vp, 1.0), dw=jnp.exp(d_log),
                               kxw_t=(kp * kw).T, bx=bx, dmax=dmax))
        for d in ch:
            d["s"] = _mm_nt(d["q"], d["k"]) * d["dw"]
        for d in ch:
            d["sv"] = _mm(d["s"], d["v"])
        for d in ch:
            d["kv"] = jnp.where(own_rows[d["h"]], _mm(d["kxw_t"], d["v"]), 0.0)
        for d in ch:
            sv_s[d["idx"]] = d["sv"]
            kv_s[d["idx"]] = d["kv"]
            b_s[d["idx"]] = d["bx"]
            dm_s[d["idx"]] = d["dmax"]
        return carry

    lax.fori_loop(0, n_chunks // group, prep, 0)

    def scan(c, carry):
        r0 = pl.multiple_of(c * L, L)
        c_prev = [c_s[h] for h in heads]
        qc = [_mm(pair(q_ref, r0, h), c_prev[h]) for h in heads]
        outs = []
        for h in heads:
            idx = c * MLSTM_HEADS + h
            bx = b_s[idx]
            dmax = dm_s[idx]
            m_prev = m_s[h:h + 1, :]
            a_log = bx + m_prev
            m_t = jnp.maximum(a_log, dmax)
            num = jnp.exp(a_log - m_t) * qc[h] + jnp.exp(dmax - m_t) * sv_s[idx]
            den = pltpu.roll(num, half, axis=1)
            hh = jnp.where(own[h], num / jnp.maximum(jnp.abs(den), jnp.exp(-m_t)), 0.0)
            m_new = m_t[L - 1:L, :]
            dec = jnp.exp(bx[L - 1:L, :] + m_prev - m_new)
            c_s[h] = dec * c_prev[h] + jnp.exp(dmax[L - 1:L, :] - m_new) * kv_s[idx]
            m_s[h:h + 1, :] = m_new
            ms = jnp.sum(hh * hh, axis=-1, keepdims=True) * (1.0 / MLSTM_DV)
            outs.append(hh * lax.rsqrt(ms + NORM_EPS))
        y = jnp.concatenate([outs[0] + outs[1], outs[2] + outs[3]], axis=1) * og_ref[...]
        y = y * _sigmoid(o_ref[pl.ds(r0, L), :])
        y_ref[pl.ds(r0, L), :] = y.astype(y_ref.dtype)
        return carry

    lax.fori_loop(0, n_chunks, scan, 0)


def _mlstm(proj3, gates_r, bias_c, bias_r, og, *, chunk, tb):
    b, t, _ = proj3.shape
    nc = tb // chunk
    nch = nc * MLSTM_HEADS
    wq = MLSTM_HEADS * MLSTM_DQK
    group = 2 if nc % 2 == 0 else 1
    kern = functools.partial(_mlstm_kernel, chunk=chunk, n_chunks=nc, group=group)
    col = lambda c0: pl.BlockSpec((None, tb, wq), lambda i, j, c0=c0: (i, j, c0 // wq))
    return pl.pallas_call(
        kern,
        grid=(b, t // tb),
        in_specs=[
            col(C_MQ), col(C_MK), col(C_MV), col(C_MO),
            pl.BlockSpec((None, tb, LANES), lambda i, j: (i, j, C_GATE // LANES)),
            pl.BlockSpec((None, nc, 16, chunk), lambda i, j: (i, j, 0, 0)),
            pl.BlockSpec((1, LANES), lambda i, j: (0, 0)),
            pl.BlockSpec((16, 1), lambda i, j: (0, 0)),
            pl.BlockSpec((1, wq), lambda i, j: (0, 0)),
        ],
        out_specs=pl.BlockSpec((None, tb, wq), lambda i, j: (i, j, 0)),
        out_shape=jax.ShapeDtypeStruct((b, t, wq), BF16),
        scratch_shapes=[pltpu.VMEM((nch, chunk, LANES), F32),
                        pltpu.VMEM((nch, LANES, LANES), F32),
                        pltpu.VMEM((nch, chunk, LANES), F32),
                        pltpu.VMEM((nch, chunk, LANES), F32),
                        pltpu.VMEM((MLSTM_HEADS, LANES, LANES), F32),
                        pltpu.VMEM((8, LANES), F32)],
        compiler_params=pltpu.CompilerParams(
            dimension_semantics=("arbitrary", "arbitrary"), vmem_limit_bytes=VMEM_LIMIT),
        name="mlstm",
    )(proj3, proj3, proj3, proj3, proj3, gates_r, bias_c, bias_r, og)


def _split3(x):
    x1 = x.astype(BF16)
    r1 = x - x1.astype(F32)
    x2 = r1.astype(BF16)
    x3 = (r1 - x2.astype(F32)).astype(BF16)
    return x1, x2, x3


def _cumsum_rows(tril_b, x):
    return sum(jnp.dot(tril_b, p, preferred_element_type=F32) for p in _split3(x))


def _cumsum_lanes(x, triu_b):
    return sum(jnp.dot(p, triu_b, preferred_element_type=F32) for p in _split3(x))


def _inverse_masks(n):
    r = lax.broadcasted_iota(jnp.int32, (n, n), 0)
    c = lax.broadcasted_iota(jnp.int32, (n, n), 1)
    offs = []
    lvl = 0
    while (1 << lvl) < n:
        rb = r >> lvl
        cb = c >> lvl
        offs.append(((rb & 1) == 1) & (cb == rb - 1))
        lvl += 1
    return (r == c).astype(F32), offs


def _gdn_kernel(q_ref, k_ref, v_ref, z_ref, qh_ref, kh_ref, vh_ref, gc_ref, gr_ref, cw_ref,
                pc_ref, pr_ref, og_ref, y_ref, qs, ks, vs, xp_s, u_s, w_s, qk_s, qd_s, kdt_s, el_s, s_s,
                *, chunk, n_chunks, tb, group):
    C = chunk
    t = pl.program_id(1)
    width = GDN_HEADS * GDN_DK

    @pl.when(t == 0)
    def _():
        s_s[...] = jnp.zeros_like(s_s)

    def conv_silu(x_ref, halo_ref, w0):
        xp_s[0:8, :] = jnp.where(t == 0, 0.0, halo_ref[...])
        xp_s[8:, :] = x_ref[...]
        y = xp_s[8:, :] * cw_ref[GDN_CONV - 1:GDN_CONV, w0:w0 + width]
        for j in range(GDN_CONV - 1):
            y = y + xp_s[pl.ds(8 - (GDN_CONV - 1) + j, tb), :] * cw_ref[j:j + 1, w0:w0 + width]
        return y * _sigmoid(y)

    def l2n(x):
        parts = []
        for h in range(GDN_HEADS):
            xh = x[:, h * GDN_DK:(h + 1) * GDN_DK]
            parts.append(xh * lax.rsqrt(jnp.sum(xh * xh, axis=-1, keepdims=True) + NORM_EPS))
        return jnp.concatenate(parts, axis=1)

    qs[...] = l2n(conv_silu(q_ref, qh_ref, 0)) * (GDN_DK ** -0.5)
    ks[...] = l2n(conv_silu(k_ref, kh_ref, width))
    vs[...] = conv_silu(v_ref, vh_ref, 2 * width)

    incl = _tri(C, "incl")
    strict = _tri(C, "strict")
    tril_b = incl.astype(BF16)
    triu_b = _tri(C, "upper_incl").astype(BF16)
    inv_masks = _inverse_masks(C)

    def prep(cg, carry):
        loaded = []
        for g in range(group):
            c = cg * group + g
            r0 = pl.multiple_of(c * C, C)
            gcol = gc_ref[pl.ds(r0, C), :]
            grow = gr_ref[c]
            qkv = [(qs[pl.ds(r0, C), h * GDN_DK:(h + 1) * GDN_DK],
                    ks[pl.ds(r0, C), h * GDN_DK:(h + 1) * GDN_DK],
                    vs[pl.ds(r0, C), h * GDN_DK:(h + 1) * GDN_DK]) for h in range(GDN_HEADS)]
            loaded.append((c, gcol, grow, qkv))
        ch = []
        for c, gcol, grow, qkv in loaded:
            g_col = -jnp.exp(pc_ref[0:1, :]) * _softplus(gcol + pc_ref[1:2, :])
            gcum_col = _cumsum_rows(tril_b, g_col)
            beta_col = _sigmoid(gcol)
            g_row = -jnp.exp(pr_ref[:, 0:1]) * _softplus(grow + pr_ref[:, 1:2])
            gcum_row = _cumsum_lanes(g_row, triu_b)
            for h in range(GDN_HEADS):
                q, k, v = qkv[h]
                gc_c = gcum_col[:, h:h + 1]
                gc_r = gcum_row[h:h + 1, :]
                beta = beta_col[:, 4 + h:5 + h]
                decay = jnp.where(incl, jnp.exp(jnp.where(incl, gc_c - gc_r, 0.0)), 0.0)
                ch.append(dict(idx=c * GDN_HEADS + h, q=q, k=k, v=v, gc_c=gc_c, beta=beta,
                               decay=decay, kb=k * beta))
        for d in ch:
            d["a"] = jnp.where(strict, _mm_nt(d["kb"], d["k"]) * d["decay"], 0.0)
        eye, offs = inv_masks
        for d in ch:
            d["t"] = eye - jnp.where(offs[0], d["a"], 0.0)
        for off in offs[1:]:
            for d in ch:
                d["p"] = _mm(d["t"], jnp.where(off, d["a"], 0.0))
            for d in ch:
                d["t"] = d["t"] - _mm(d["p"], d["t"])
        for d in ch:
            d["egc"] = jnp.exp(d["gc_c"])
            d["u"] = _mm(d["t"], d["v"] * d["beta"])
        for d in ch:
            d["w"] = _mm(d["t"], d["kb"] * d["egc"]).astype(BF16)
        for d in ch:
            d["qk"] = (_mm_nt(d["q"], d["k"]) * d["decay"]).astype(BF16)
        for d in ch:
            idx = d["idx"]
            g_last = d["gc_c"][C - 1:C, :]
            u_s[idx] = d["u"]
            w_s[idx] = d["w"]
            qk_s[idx] = d["qk"]
            qd_s[idx] = (d["q"] * d["egc"]).astype(BF16)
            kdt_s[idx] = (d["k"] * jnp.exp(g_last - d["gc_c"])).T.astype(BF16)
            el_s[idx] = jnp.broadcast_to(jnp.exp(g_last), (8, LANES))
        return carry

    lax.fori_loop(0, n_chunks // group, prep, 0)

    def scan(c, carry):
        r0 = pl.multiple_of(c * C, C)
        heads = range(GDN_HEADS)
        dot = functools.partial(jnp.dot, preferred_element_type=F32)
        idx = [c * GDN_HEADS + h for h in heads]
        s_prev = [s_s[h] for h in heads]
        s_b = [s.astype(BF16) for s in s_prev]
        ws = [dot(w_s[idx[h]], s_b[h]) for h in heads]
        qs_ = [dot(qd_s[idx[h]], s_b[h]) for h in heads]
        v_b = [(u_s[idx[h]] - ws[h]).astype(BF16) for h in heads]
        upd = [dot(kdt_s[idx[h]], v_b[h]) for h in heads]
        o_all = [qs_[h] + dot(qk_s[idx[h]], v_b[h]) for h in heads]
        outs = []
        for h in heads:
            s_s[h] = s_prev[h] * el_s[idx[h]][0:1, :] + upd[h]
            o = o_all[h]
            ms = jnp.mean(o * o, axis=-1, keepdims=True)
            outs.append(o * lax.rsqrt(ms + NORM_EPS))
        y = jnp.concatenate(outs, axis=1) * og_ref[...]
        z = z_ref[pl.ds(r0, C), :]
        y = y * (z * _sigmoid(z))
        y_ref[pl.ds(r0, C), :] = y.astype(y_ref.dtype)
        return carry

    lax.fori_loop(0, n_chunks, scan, 0)


def _gdn(proj3, gates_r, conv_w, par_c, par_r, og, *, chunk, tb):
    b, t, _ = proj3.shape
    nc = tb // chunk
    nch = nc * GDN_HEADS
    width = GDN_HEADS * GDN_DK
    group = 4 if nc % 4 == 0 else (2 if nc % 2 == 0 else 1)
    kern = functools.partial(_gdn_kernel, chunk=chunk, n_chunks=nc, tb=tb, group=group)
    col = lambda c0: pl.BlockSpec((None, tb, width), lambda i, j, c0=c0: (i, j, c0 // width))
    halo = lambda c0: pl.BlockSpec(
        (None, 8, width), lambda i, j, c0=c0: (i, jnp.maximum(j * (tb // 8) - 1, 0), c0 // width))
    return pl.pallas_call(
        kern,
        grid=(b, t // tb),
        in_specs=[
            col(C_GQ), col(C_GK), col(C_GV), col(C_GZ),
            halo(C_GQ), halo(C_GK), halo(C_GV),
            pl.BlockSpec((None, tb, LANES), lambda i, j: (i, j, C_GATE // LANES)),
            pl.BlockSpec((None, nc, 16, chunk), lambda i, j: (i, j, 0, 0)),
            pl.BlockSpec((GDN_CONV, 3 * width), lambda i, j: (0, 0)),
            pl.BlockSpec((2, LANES), lambda i, j: (0, 0)),
            pl.BlockSpec((16, 2), lambda i, j: (0, 0)),
            pl.BlockSpec((1, width), lambda i, j: (0, 0)),
        ],
        out_specs=pl.BlockSpec((None, tb, width), lambda i, j: (i, j, 0)),
        out_shape=jax.ShapeDtypeStruct((b, t, width), BF16),
        scratch_shapes=[pltpu.VMEM((tb, width), F32), pltpu.VMEM((tb, width), F32),
                        pltpu.VMEM((tb, width), F32),
                        pltpu.VMEM((tb + 8, width), F32),
                        pltpu.VMEM((nch, chunk, GDN_DV), F32),
                        pltpu.VMEM((nch, chunk, GDN_DK), BF16),
                        pltpu.VMEM((nch, chunk, chunk), BF16),
                        pltpu.VMEM((nch, chunk, GDN_DK), BF16),
                        pltpu.VMEM((nch, GDN_DK, chunk), BF16),
                        pltpu.VMEM((nch, 8, LANES), F32),
                        pltpu.VMEM((GDN_HEADS, GDN_DK, GDN_DV), F32)],
        compiler_params=pltpu.CompilerParams(
            dimension_semantics=("arbitrary", "arbitrary"), vmem_limit_bytes=VMEM_LIMIT),
        name="gdn",
    )(proj3, proj3, proj3, proj3, proj3, proj3, proj3, proj3, gates_r, conv_w, par_c, par_r, og)


def _dsa_prep_kernel(cq_ref, ckv_ref, idx_ref, qn_ref, kn_ref, wuq_ref, wqi_ref, wuk_ref,
                     qlat_ref, qidx_ref, ckvn_ref, ckvt_ref, kidx_ref):
    cq = cq_ref[...]
    ms = jnp.mean(cq * cq, axis=-1, keepdims=True)
    cqn = ((cq * lax.rsqrt(ms + NORM_EPS)) * qn_ref[...]).astype(BF16)
    ckv = ckv_ref[...]
    ms = jnp.mean(ckv * ckv, axis=-1, keepdims=True)
    ckvn = (ckv * lax.rsqrt(ms + NORM_EPS)) * kn_ref[...]
    ckvn_ref[...] = ckvn.astype(BF16)
    ckvt_ref[...] = ckvn.T.astype(BF16)
    q = jnp.dot(cqn, wuq_ref[...], preferred_element_type=F32)
    qlat = jnp.dot(q.astype(BF16), wuk_ref[...], preferred_element_type=F32) * (DSA_HEAD_DIM ** -0.5)
    qlat_ref[...] = qlat.astype(BF16)
    qidx_ref[...] = jnp.dot(cqn, wqi_ref[...], preferred_element_type=F32).astype(BF16)
    kidx_ref[...] = idx_ref[:, 0:IDX_DIM].astype(BF16)


def _dsa_prep(proj2, qn, kn, wuq, wqi, wuk_bd):
    m = proj2.shape[0]
    tm = 512
    full = lambda shape: pl.BlockSpec(shape, lambda i: (0, 0))
    return pl.pallas_call(
        _dsa_prep_kernel,
        grid=(m // tm,),
        in_specs=[
            pl.BlockSpec((tm, DSA_Q_RANK), lambda i: (i, C_CQ // DSA_Q_RANK)),
            pl.BlockSpec((tm, DSA_KV_RANK), lambda i: (i, C_CKV // DSA_KV_RANK)),
            pl.BlockSpec((tm, LANES), lambda i: (i, C_IDX // LANES)),
            full((1, DSA_Q_RANK)), full((1, DSA_KV_RANK)),
            full((DSA_Q_RANK, DSA_HEADS * DSA_HEAD_DIM)),
            full((DSA_Q_RANK, IDX_HEADS * IDX_DIM)),
            full((DSA_HEADS * DSA_HEAD_DIM, DSA_HEADS * DSA_KV_RANK)),
        ],
        out_specs=[
            pl.BlockSpec((tm, DSA_HEADS * DSA_KV_RANK), lambda i: (i, 0)),
            pl.BlockSpec((tm, IDX_HEADS * IDX_DIM), lambda i: (i, 0)),
            pl.BlockSpec((tm, DSA_KV_RANK), lambda i: (i, 0)),
            pl.BlockSpec((DSA_KV_RANK, tm), lambda i: (0, i)),
            pl.BlockSpec((tm, IDX_DIM), lambda i: (i, 0)),
        ],
        out_shape=[
            jax.ShapeDtypeStruct((m, DSA_HEADS * DSA_KV_RANK), BF16),
            jax.ShapeDtypeStruct((m, IDX_HEADS * IDX_DIM), BF16),
            jax.ShapeDtypeStruct((m, DSA_KV_RANK), BF16),
            jax.ShapeDtypeStruct((DSA_KV_RANK, m), BF16),
            jax.ShapeDtypeStruct((m, IDX_DIM), BF16),
        ],
        compiler_params=pltpu.CompilerParams(
            dimension_semantics=("arbitrary",), vmem_limit_bytes=VMEM_LIMIT),
        name="dsa_prep",
    )(proj2, proj2, proj2, qn, kn, wuq, wqi, wuk_bd)


def _tree_sum(parts):
    while len(parts) > 1:
        parts = [parts[j] + parts[j + 1] for j in range(0, len(parts) - 1, 2)] + (
            [parts[-1]] if len(parts) % 2 else [])
    return parts[0]


def _dsa_kernel(qidx_ref, wrow_ref, qlat_ref, kidx_ref, ckv_ref, ckvt_ref, wuvt_ref, y_ref,
                key_s, hi_s, lo_s, cut_s, acc_s, *, qb, tk, n_sel):
    i = pl.program_id(1)
    n_tiles = (i * qb + qb + tk - 1) // tk
    key_pos = lax.broadcasted_iota(jnp.int32, (tk, qb), 0)
    q_pos = i * qb + lax.broadcasted_iota(jnp.int32, (tk, qb), 1)
    w_rows = wrow_ref[...] * (IDX_HEADS ** -0.5 * IDX_DIM ** -0.5)
    qidx = qidx_ref[...]
    q_heads = [qidx[:, h * IDX_DIM:(h + 1) * IDX_DIM] for h in range(IDX_HEADS)]
    i16_min = -(1 << 15)
    i16_max = (1 << 15) - 1
    sub16 = 16
    nt_dims = (((1,), (1,)), ((), ()))

    def score_body(kt, carry):
        k0 = pl.multiple_of(kt * tk, tk)
        kk = kidx_ref[pl.ds(k0, tk), :]
        sh = [lax.dot_general(kk, q_heads[h], nt_dims, preferred_element_type=F32)
              for h in range(IDX_HEADS)]
        acc = jnp.zeros((tk, qb), F32)
        for h in range(IDX_HEADS):
            acc = acc + jnp.maximum(sh[h], 0.0) * w_rows[h:h + 1, :]
        acc = acc + 0.0
        bits = pltpu.bitcast(acc, jnp.int32)
        keys = jnp.where(bits < 0, bits ^ jnp.int32(0x7FFFFFFF), bits)
        keys = jnp.where(k0 + key_pos <= q_pos, keys, jnp.int32(INT_MIN))
        key_s[kt] = keys
        hi_s[kt] = (keys >> 16).astype(jnp.int16)
        return carry

    lax.fori_loop(0, n_tiles, score_body, 0)

    def count_ge(src_s, cand):
        cand16 = jnp.broadcast_to(cand, (sub16, qb)).astype(jnp.int16)
        one, zero = jnp.int16(1), jnp.int16(0)

        def body(kt, acc):
            tile = src_s[kt]
            return acc + _tree_sum([jnp.where(tile[j * sub16:(j + 1) * sub16, :] >= cand16, one, zero)
                                    for j in range(tk // sub16)])

        cnt = lax.fori_loop(0, n_tiles, body, jnp.zeros((sub16, qb), jnp.int16))
        return jnp.sum(cnt.astype(jnp.int32), axis=0, keepdims=True)

    def kth_largest16(src_s, rank):
        thr = jnp.full((1, qb), i16_min, jnp.int32)
        n_ge = jnp.zeros((1, qb), jnp.int32) + n_tiles * tk
        n_gt = jnp.zeros((1, qb), jnp.int32)
        for bit in range(15, -1, -1):
            cand = thr + jnp.int32(1 << bit)
            cnt = count_ge(src_s, cand)
            ok = cnt >= rank
            thr = jnp.where(ok, cand, thr)
            n_ge = jnp.where(ok, cnt, n_ge)
            n_gt = jnp.where(ok, n_gt, cnt)
        return thr, n_ge, n_gt

    hi_thr, hi_ge, above = kth_largest16(hi_s, jnp.int32(n_sel))
    rank_lo = jnp.int32(n_sel) - above

    def lo_body(kt, carry):
        keys = key_s[kt]
        lo = (keys & jnp.int32(0xFFFF)) + jnp.int32(i16_min)
        lo_s[kt] = jnp.where((keys >> 16) == hi_thr, lo, jnp.int32(i16_min)).astype(jnp.int16)
        return carry

    lax.fori_loop(0, n_tiles, lo_body, 0)
    lo_thr, lo_ge, _ = kth_largest16(lo_s, rank_lo)
    thr_raw = hi_thr * jnp.int32(1 << 16) + (lo_thr - jnp.int32(i16_min))
    thr = jnp.maximum(thr_raw, jnp.int32(INT_MIN + 1))

    in_bucket = jnp.where(lo_thr > i16_min, lo_ge, hi_ge - above)
    excess = jnp.where(thr_raw > INT_MIN, above + in_bucket - jnp.int32(n_sel), 0)

    max_excess = jnp.max(excess)
    last = tk * key_s.shape[0] - 1
    assert last <= i16_max
    direct_limit = max(last, 1).bit_length()

    @pl.when(max_excess > 0)
    def _():
        def fill(kt, carry):
            rev = jnp.int32(last) - (kt * tk + key_pos)
            lo_s[kt] = jnp.where(key_s[kt] == thr_raw, rev, jnp.int32(-1)).astype(jnp.int16)
            return carry

        lax.fori_loop(0, n_tiles, fill, 0)

        @pl.when(max_excess <= direct_limit)
        def _():
            def next_tied(prev):
                prev16 = jnp.broadcast_to(prev, (sub16, qb)).astype(jnp.int16)
                big = jnp.int16(i16_max)

                def body(kt, acc):
                    tile = lo_s[kt]
                    parts = [jnp.where(tile[j * sub16:(j + 1) * sub16, :] > prev16,
                                       tile[j * sub16:(j + 1) * sub16, :], big)
                             for j in range(tk // sub16)]
                    while len(parts) > 1:
                        parts = [jnp.where(parts[j] < parts[j + 1], parts[j], parts[j + 1])
                                 for j in range(0, len(parts), 2)]
                    return jnp.where(acc < parts[0], acc, parts[0])

                low = lax.fori_loop(0, n_tiles, body, jnp.full((sub16, qb), i16_max, jnp.int16))
                return jnp.min(low.astype(jnp.int32), axis=0, keepdims=True)

            def step(j, carry):
                prev, cut = carry
                cur = next_tied(prev)
                return cur, jnp.where(j < excess, cur + 1, cut)

            _, cut = lax.fori_loop(0, max_excess, step,
                                   (jnp.full((1, qb), -1, jnp.int32), jnp.zeros((1, qb), jnp.int32)))
            cut_s[0:1, :] = cut

        @pl.when(max_excess > direct_limit)
        def _():
            keep = count_ge(lo_s, jnp.zeros((1, qb), jnp.int32)) - excess
            cut = jnp.zeros((1, qb), jnp.int32)
            for bit in range(direct_limit - 1, -1, -1):
                cand = cut + jnp.int32(1 << bit)
                cut = jnp.where(count_ge(lo_s, cand) >= keep, cand, cut)
            cut_s[0:1, :] = jnp.where(excess > 0, cut, 0)

        cut = cut_s[0:1, :]

        def demote(kt, carry):
            keys = key_s[kt]
            rev = jnp.int32(last) - (kt * tk + key_pos)
            demoted = jnp.where(rev < cut, jnp.int32(INT_MIN), keys)
            key_s[kt] = jnp.where(keys == thr_raw, demoted, keys)
            return carry

        lax.fori_loop(0, n_tiles, demote, 0)

    qlat = qlat_ref[...]
    heads = range(DSA_HEADS)
    ql = [qlat[:, h * DSA_KV_RANK:(h + 1) * DSA_KV_RANK] for h in heads]
    acc_s[...] = jnp.zeros_like(acc_s)

    def attn_body(kt, carry):
        m_run, l_run = carry
        k0 = pl.multiple_of(kt * tk, tk)
        ckv = ckv_ref[pl.ds(k0, tk), :]
        ckvt = ckvt_ref[:, pl.ds(k0, tk)]
        bias = jnp.where(key_s[kt] >= thr, 0.0, -jnp.inf)
        lg = [lax.dot_general(ckv, ql[h], nt_dims, preferred_element_type=F32) for h in heads]
        m_out, l_out, p_all, alphas = [], [], [], []
        for h in heads:
            lgm = lg[h] + bias
            m_new = jnp.maximum(m_run[h], jnp.max(lgm, axis=0, keepdims=True))
            p = jnp.exp(lgm - m_new)
            alpha = jnp.exp(m_run[h] - m_new)
            l_out.append(alpha * l_run[h] + jnp.sum(p, axis=0, keepdims=True))
            m_out.append(m_new)
            p_all.append(p.astype(BF16))
            alphas.append(alpha)
        pv = [jnp.dot(ckvt, p_all[h], preferred_element_type=F32) for h in heads]
        for h in heads:
            acc_s[h] = alphas[h] * acc_s[h] + pv[h]
        return tuple(m_out), tuple(l_out)

    init = (tuple(jnp.full((1, qb), -1e30, F32) for _ in heads),
            tuple(jnp.zeros((1, qb), F32) for _ in heads))
    _, l_fin = lax.fori_loop(0, n_tiles, attn_body, init)
    o_lat = [(acc_s[h] / l_fin[h]).astype(BF16) for h in heads]
    outs = [jnp.dot(wuvt_ref[h], o_lat[h], preferred_element_type=F32) for h in heads]
    y_ref[...] = jnp.concatenate(outs, axis=0).T.astype(y_ref.dtype)


def _dsa(qidx, widx_rows, qlat, kidx, ckvn, ckvt, wuvt, *, qb, tk):
    b, t, _ = qidx.shape
    n_sel = min(IDX_TOPK, t // 4)
    kern = functools.partial(_dsa_kernel, qb=qb, tk=tk, n_sel=n_sel)
    return pl.pallas_call(
        kern,
        grid=(b, t // qb),
        in_specs=[
            pl.BlockSpec((None, qb, IDX_HEADS * IDX_DIM), lambda i, j: (i, j, 0)),
            pl.BlockSpec((None, IDX_HEADS, qb), lambda i, j: (i, 0, j)),
            pl.BlockSpec((None, qb, DSA_HEADS * DSA_KV_RANK), lambda i, j: (i, j, 0)),
            pl.BlockSpec((None, t, IDX_DIM), lambda i, j: (i, 0, 0)),
            pl.BlockSpec((None, t, DSA_KV_RANK), lambda i, j: (i, 0, 0)),
            pl.BlockSpec((DSA_KV_RANK, t), lambda i, j: (0, i)),
            pl.BlockSpec((DSA_HEADS, DSA_HEAD_DIM, DSA_KV_RANK), lambda i, j: (0, 0, 0)),
        ],
        out_specs=pl.BlockSpec((None, qb, DSA_HEADS * DSA_HEAD_DIM), lambda i, j: (i, j, 0)),
        out_shape=jax.ShapeDtypeStruct((b, t, DSA_HEADS * DSA_HEAD_DIM), BF16),
        scratch_shapes=[pltpu.VMEM((t // tk, tk, qb), jnp.int32),
                        pltpu.VMEM((t // tk, tk, qb), jnp.int16),
                        pltpu.VMEM((t // tk, tk, qb), jnp.int16),
                        pltpu.VMEM((8, qb), jnp.int32),
                        pltpu.VMEM((DSA_HEADS, DSA_KV_RANK, qb), F32)],
        compiler_params=pltpu.CompilerParams(
            dimension_semantics=("arbitrary", "arbitrary"), vmem_limit_bytes=VMEM_LIMIT),
        name="dsa",
    )(qidx, widx_rows, qlat, kidx, ckvn, ckvt, wuvt)


def _pack_w_in(w):
    pts = [0]
    for wd in IN_WIDTHS:
        pts.append(pts[-1] + wd)
    seg = [w[..., pts[i]:pts[i + 1]] for i in range(len(IN_WIDTHS))]
    (gq, gk, gv, gz, ga, gb, cq, ckv, ik, iw, mq, mk, mv, mo, mi, mf) = seg
    zeros = lambda n: jnp.zeros(w.shape[:-1] + (n,), w.dtype)
    idx_blk = jnp.concatenate([ik, iw, zeros(LANES - IDX_DIM - IDX_HEADS)], axis=-1)
    gate_blk = jnp.concatenate([ga, gb, mi, mf, zeros(LANES - 16)], axis=-1)
    out = jnp.concatenate([gq, gk, gv, gz, cq, ckv, idx_blk, mq, mk, mv, mo, gate_blk], axis=-1)
    assert out.shape[-1] == NP
    return out.astype(BF16)


def _lane_rows(vals_at, depth):
    rows = jnp.zeros((depth, LANES), F32)
    for start, v in vals_at:
        rows = rows.at[:, start:start + v.shape[1]].set(v.astype(F32))
    return rows


def kernel(x, attn_norm, w_in, gdn_conv, gdn_a_log, gdn_dt_bias, gdn_out_norm, dsa_q_norm,
           dsa_kv_norm, dsa_w_uq, dsa_w_qidx, dsa_w_uk, dsa_w_uv, mlstm_i_bias, mlstm_f_bias,
           mlstm_out_norm, w_out, ffn_norm, w_gate, w_up, w_down, final_norm):
    b, t, d = x.shape
    depth = w_in.shape[0]
    gdn_chunk, mlstm_chunk = min(128, t), min(64, t)
    tb = min(512, t)
    qb = min(512, t)
    tk = min(512, t)

    w_in_p = _pack_w_in(w_in)
    gdn_par_c = jnp.stack([_lane_rows([(0, gdn_a_log)], depth), _lane_rows([(0, gdn_dt_bias)], depth)],
                          axis=1)
    gdn_par_r = jnp.swapaxes(gdn_par_c[:, :, :16], 1, 2)
    gdn_og = jnp.tile(gdn_out_norm, (1, GDN_HEADS))
    wuk_bd = jnp.zeros((depth, DSA_HEADS * DSA_HEAD_DIM, DSA_HEADS * DSA_KV_RANK), F32)
    for h in range(DSA_HEADS):
        wuk_bd = wuk_bd.at[:, h * DSA_HEAD_DIM:(h + 1) * DSA_HEAD_DIM,
                           h * DSA_KV_RANK:(h + 1) * DSA_KV_RANK].set(jnp.swapaxes(dsa_w_uk[:, h], 1, 2))
    wuk_bd = wuk_bd.astype(BF16)
    wuq_b, wqi_b = dsa_w_uq.astype(BF16), dsa_w_qidx.astype(BF16)
    wuvt_b = jnp.swapaxes(dsa_w_uv, 2, 3).astype(BF16)
    ml_bias_c = _lane_rows([(8, mlstm_i_bias), (12, mlstm_f_bias)], depth)
    ml_og = jnp.tile(mlstm_out_norm, (1, MLSTM_HEADS))
    w_out_b, w_gate_b, w_up_b, w_down_b = (a.astype(BF16) for a in (w_out, w_gate, w_up, w_down))

    x2 = x.reshape(b * t, d)
    for l in range(depth):
        proj2 = _inproj(x2, attn_norm[l].reshape(1, d), w_in_p[l])
        proj3 = proj2.reshape(b, t, NP)
        gates_t = jnp.swapaxes(proj3[:, :, C_GATE:C_GATE + 16], 1, 2)

        def chunk_rows(c):
            return jnp.swapaxes(gates_t.reshape(b, 16, t // c, c), 1, 2)

        y_a = _gdn(proj3, chunk_rows(gdn_chunk), gdn_conv[l], gdn_par_c[l], gdn_par_r[l],
                   gdn_og[l].reshape(1, -1), chunk=gdn_chunk, tb=tb)

        qlat, qidx, ckvn, ckvt, kidx = _dsa_prep(
            proj2, dsa_q_norm[l].reshape(1, -1), dsa_kv_norm[l].reshape(1, -1),
            wuq_b[l], wqi_b[l], wuk_bd[l])
        r3 = lambda a: a.reshape(b, t, a.shape[-1])
        widx_rows = jnp.swapaxes(proj3[:, :, C_IDX + IDX_DIM:C_IDX + IDX_DIM + IDX_HEADS], 1, 2)
        y_b = _dsa(r3(qidx), widx_rows, r3(qlat), r3(kidx), r3(ckvn), ckvt, wuvt_b[l], qb=qb, tk=tk)

        y_c = _mlstm(proj3, chunk_rows(mlstm_chunk), ml_bias_c[l].reshape(1, LANES),
                     ml_bias_c[l, :16].reshape(16, 1), ml_og[l].reshape(1, -1),
                     chunk=mlstm_chunk, tb=tb)

        x2 = _out_ffn(x2, y_a.reshape(b * t, -1), y_b.reshape(b * t, -1), y_c.reshape(b * t, -1),
                      w_out_b[l], ffn_norm[l].reshape(1, d), w_gate_b[l], w_up_b[l], w_down_b[l],
                      final_norm.reshape(1, d), final_norm=(l == depth - 1))
    return x2.reshape(b, t, d)
```

```python
import functools
import math

import jax
import jax.numpy as jnp
from jax import lax
from jax.experimental import pallas as pl
from jax.experimental.pallas import tpu as pltpu

F32 = jnp.float32
BF16 = jnp.bfloat16
HIGHEST = lax.Precision.HIGHEST

D_MODEL = 1024
GDN_HEADS = 4
GDN_DK = 128
GDN_DV = 128
GDN_CONV = 4
DSA_HEADS = 4
DSA_HEAD_DIM = 64
DSA_Q_RANK = 256
DSA_KV_RANK = 128
IDX_HEADS = 8
IDX_DIM = 32
IDX_TOPK = 256
MLSTM_HEADS = 4
MLSTM_DQK = 64
MLSTM_DV = 64
FFN_HIDDEN = 2816
NORM_EPS = 1e-6

IN_WIDTHS = (512, 512, 512, 512, 4, 4, 256, 128, 32, 8, 256, 256, 256, 256, 4, 4)

C_GQ, C_GK, C_GV, C_GZ = 0, 512, 1024, 1536
C_CQ, C_CKV, C_IDX = 2048, 2304, 2432
C_MQ, C_MK, C_MV, C_MO = 2560, 2816, 3072, 3328
C_GATE = 3584
NP = 3712
LANES = 128

VMEM_LIMIT = 56 * 1024 * 1024

INT_MIN = -2 ** 31


def _mm(a, b):
    return jnp.dot(a.astype(BF16), b.astype(BF16), preferred_element_type=F32)


def _mm_nt(a, b):
    return lax.dot_general(a.astype(BF16), b.astype(BF16), (((1,), (1,)), ((), ())),
                           preferred_element_type=F32)


def _sigmoid(x):
    return 1.0 / (1.0 + jnp.exp(-x))


def _softplus(x):
    return jnp.maximum(x, 0.0) + jnp.log1p(jnp.exp(-jnp.abs(x)))


def _tri(n, kind):
    r = lax.broadcasted_iota(jnp.int32, (n, n), 0)
    c = lax.broadcasted_iota(jnp.int32, (n, n), 1)
    if kind == "incl":
        return r >= c
    if kind == "strict":
        return r > c
    if kind == "upper_incl":
        return r <= c
    raise ValueError(kind)


def _inproj_kernel(x_ref, g_ref, w_ref, o_ref):
    x = x_ref[...]
    ms = jnp.mean(x * x, axis=-1, keepdims=True)
    h = (x * lax.rsqrt(ms + NORM_EPS)) * g_ref[...]
    o_ref[...] = jnp.dot(h.astype(BF16), w_ref[...], preferred_element_type=F32)


def _inproj(x2, g, w):
    m = x2.shape[0]
    tm = 256
    return pl.pallas_call(
        _inproj_kernel,
        grid=(m // tm,),
        in_specs=[
            pl.BlockSpec((tm, D_MODEL), lambda i: (i, 0)),
            pl.BlockSpec((1, D_MODEL), lambda i: (0, 0)),
            pl.BlockSpec((D_MODEL, NP), lambda i: (0, 0)),
        ],
        out_specs=pl.BlockSpec((tm, NP), lambda i: (i, 0)),
        out_shape=jax.ShapeDtypeStruct((m, NP), F32),
        compiler_params=pltpu.CompilerParams(
            dimension_semantics=("arbitrary",), vmem_limit_bytes=VMEM_LIMIT),
        name="inproj",
    )(x2, g, w)


def _ffn_kernel(x_ref, ya_ref, yb_ref, yc_ref, wo_ref, g_ref, wg_ref, wu_ref, wd_ref, fg_ref,
                o_ref, acc_ref, h_ref, *, n_hidden_steps, final_norm):
    j = pl.program_id(1)

    @pl.when(j == 0)
    def _():
        xn = x_ref[...]
        xn = xn + jnp.dot(ya_ref[...], wo_ref[0:512, :], preferred_element_type=F32)
        xn = xn + jnp.dot(yb_ref[...], wo_ref[512:768, :], preferred_element_type=F32)
        xn = xn + jnp.dot(yc_ref[...], wo_ref[768:1024, :], preferred_element_type=F32)
        acc_ref[...] = xn
        ms = jnp.mean(xn * xn, axis=-1, keepdims=True)
        h_ref[...] = ((xn * lax.rsqrt(ms + NORM_EPS)) * g_ref[...]).astype(BF16)

    h = h_ref[...]
    gate = jnp.dot(h, wg_ref[...], preferred_element_type=F32)
    up = jnp.dot(h, wu_ref[...], preferred_element_type=F32)
    act = (gate * _sigmoid(gate)) * up
    acc_ref[...] += jnp.dot(act.astype(BF16), wd_ref[...], preferred_element_type=F32)

    @pl.when(j == n_hidden_steps - 1)
    def _():
        y = acc_ref[...]
        if final_norm:
            ms = jnp.mean(y * y, axis=-1, keepdims=True)
            y = (y * lax.rsqrt(ms + NORM_EPS)) * fg_ref[...]
        o_ref[...] = y


def _out_ffn(x2, ya, yb, yc, wo, g, wg, wu, wd, fg, final_norm):
    m = x2.shape[0]
    tm = 512
    th = 2816
    nh = FFN_HIDDEN // th
    kern = functools.partial(_ffn_kernel, n_hidden_steps=nh, final_norm=final_norm)
    return pl.pallas_call(
        kern,
        grid=(m // tm, nh),
        in_specs=[
            pl.BlockSpec((tm, D_MODEL), lambda i, j: (i, 0)),
            pl.BlockSpec((tm, 512), lambda i, j: (i, 0)),
            pl.BlockSpec((tm, 256), lambda i, j: (i, 0)),
            pl.BlockSpec((tm, 256), lambda i, j: (i, 0)),
            pl.BlockSpec((D_MODEL, D_MODEL), lambda i, j: (0, 0), pipeline_mode=pl.Buffered(1)),
            pl.BlockSpec((1, D_MODEL), lambda i, j: (0, 0)),
            pl.BlockSpec((D_MODEL, th), lambda i, j: (0, j), pipeline_mode=pl.Buffered(1)),
            pl.BlockSpec((D_MODEL, th), lambda i, j: (0, j), pipeline_mode=pl.Buffered(1)),
            pl.BlockSpec((th, D_MODEL), lambda i, j: (j, 0), pipeline_mode=pl.Buffered(1)),
            pl.BlockSpec((1, D_MODEL), lambda i, j: (0, 0)),
        ],
        out_specs=pl.BlockSpec((tm, D_MODEL), lambda i, j: (i, 0)),
        out_shape=jax.ShapeDtypeStruct((m, D_MODEL), F32),
        scratch_shapes=[pltpu.VMEM((tm, D_MODEL), F32), pltpu.VMEM((tm, D_MODEL), BF16)],
        compiler_params=pltpu.CompilerParams(
            dimension_semantics=("arbitrary", "arbitrary"), vmem_limit_bytes=VMEM_LIMIT),
        name="out_ffn",
    )(x2, ya, yb, yc, wo, g, wg, wu, wd, fg)


def _cummax_rows(x, n):
    neg = -jnp.inf
    row = lax.broadcasted_iota(jnp.int32, x.shape, 0)
    s = 1
    while s < n:
        if s < 8:
            shifted = jnp.where(row >= s, pltpu.roll(x, s, axis=0), neg)
        else:
            shifted = jnp.concatenate([jnp.full((s, x.shape[1]), neg, x.dtype), x[:n - s]], axis=0)
        x = jnp.maximum(x, shifted)
        s *= 2
    return x


def _mlstm_kernel(q_ref, k_ref, v_ref, o_ref, gc_ref, gr_ref, bc_ref, br_ref, og_ref,
                  y_ref, sv_s, kv_s, b_s, dm_s, c_s, m_s, *, chunk, n_chunks, group):
    L = chunk
    t = pl.program_id(1)
    heads = range(MLSTM_HEADS)
    half = MLSTM_DQK

    @pl.when(t == 0)
    def _():
        c_s[...] = jnp.zeros_like(c_s)
        m_s[...] = jnp.zeros_like(m_s)

    incl = _tri(L, "incl")
    tril_b = incl.astype(BF16)
    triu_b = _tri(L, "upper_incl").astype(BF16)
    lane = lax.broadcasted_iota(jnp.int32, (L, LANES), 1)
    own = [(lane < half) if h % 2 == 0 else (lane >= half) for h in heads]
    row_c = lax.broadcasted_iota(jnp.int32, (LANES, LANES), 0)
    own_rows = [(row_c < half) if h % 2 == 0 else (row_c >= half) for h in heads]
    scale = MLSTM_DQK ** -0.5

    def pair(ref, r0, h):
        return ref[pl.ds(r0, L), (h // 2) * LANES:(h // 2 + 1) * LANES]

    def prep(cg, carry):
        ch = []
        for g in range(group):
            c = cg * group + g
            r0 = pl.multiple_of(c * L, L)
            x_col = gc_ref[pl.ds(r0, L), :] + bc_ref[...]
            x_row = gr_ref[c] + br_ref[...]
            b_row = _cumsum_lanes(-_softplus(-x_row), triu_b)
            f_rep = jnp.concatenate(
                [jnp.broadcast_to(x_col[:, 12 + h:13 + h], (L, LANES)) for h in heads], axis=1)
            b_rep = _cumsum_rows(tril_b, -_softplus(-f_rep))
            for h in heads:
                bx = b_rep[:, h * LANES:(h + 1) * LANES]
                cx = jnp.broadcast_to(x_col[:, 8 + h:9 + h], (L, LANES)) - bx
                dmax = bx + _cummax_rows(cx, L)
                c_row = x_row[8 + h:9 + h, :] - b_row[12 + h:13 + h, :]
                d_log = jnp.where(incl, bx[:, :L] + c_row - dmax[:, :L], -jnp.inf)
                kw = jnp.exp(cx + (bx[L - 1:L, :] - dmax[L - 1:L, :]))
                kp = pair(k_ref, r0, h) * scale
                vp = pair(v_ref, r0, h)
                ch.append(dict(idx=c * MLSTM_HEADS + h, h=h,
                               q=jnp.where(own[h], pair(q_ref, r0, h), 0.0), k=kp,
                               v=jnp.where(own[h], vp, 1.0), dw=jnp.exp(d_log),
                               kxw_t=(kp * kw).T, bx=bx, dmax=dmax))
        for d in ch:
            d["s"] = _mm_nt(d["q"], d["k"]) * d["dw"]
        for d in ch:
            d["sv"] = _mm(d["s"], d["v"])
        for d in ch:
            d["kv"] = jnp.where(own_rows[d["h"]], _mm(d["kxw_t"], d["v"]), 0.0)
        for d in ch:
            sv_s[d["idx"]] = d["sv"]
            kv_s[d["idx"]] = d["kv"]
            b_s[d["idx"]] = d["bx"]
            dm_s[d["idx"]] = d["dmax"]
        return carry

    lax.fori_loop(0, n_chunks // group, prep, 0)

    def scan(c, carry):
        r0 = pl.multiple_of(c * L, L)
        c_prev = [c_s[h] for h in heads]
        qc = [_mm(pair(q_ref, r0, h), c_prev[h]) for h in heads]
        outs = []
        for h in heads:
            idx = c * MLSTM_HEADS + h
            bx = b_s[idx]
            dmax = dm_s[idx]
            m_prev = m_s[h:h + 1, :]
            a_log = bx + m_prev
            m_t = jnp.maximum(a_log, dmax)
            num = jnp.exp(a_log - m_t) * qc[h] + jnp.exp(dmax - m_t) * sv_s[idx]
            den = pltpu.roll(num, half, axis=1)
            hh = jnp.where(own[h], num / jnp.maximum(jnp.abs(den), jnp.exp(-m_t)), 0.0)
            m_new = m_t[L - 1:L, :]
            dec = jnp.exp(bx[L - 1:L, :] + m_prev - m_new)
            c_s[h] = dec * c_prev[h] + jnp.exp(dmax[L - 1:L, :] - m_new) * kv_s[idx]
            m_s[h:h + 1, :] = m_new
            ms = jnp.sum(hh * hh, axis=-1, keepdims=True) * (1.0 / MLSTM_DV)
            outs.append(hh * lax.rsqrt(ms + NORM_EPS))
        y = jnp.concatenate([outs[0] + outs[1], outs[2] + outs[3]], axis=1) * og_ref[...]
        y = y * _sigmoid(o_ref[pl.ds(r0, L), :])
        y_ref[pl.ds(r0, L), :] = y.astype(y_ref.dtype)
        return carry

    lax.fori_loop(0, n_chunks, scan, 0, unroll=2 if n_chunks % 2 == 0 else 1)


def _mlstm(proj3, gates_r, bias_c, bias_r, og, *, chunk, tb):
    b, t, _ = proj3.shape
    nc = tb // chunk
    nch = nc * MLSTM_HEADS
    wq = MLSTM_HEADS * MLSTM_DQK
    group = 8 if nc % 8 == 0 else (2 if nc % 2 == 0 else 1)
    kern = functools.partial(_mlstm_kernel, chunk=chunk, n_chunks=nc, group=group)
    col = lambda c0: pl.BlockSpec((None, tb, wq), lambda i, j, c0=c0: (i, j, c0 // wq))
    return pl.pallas_call(
        kern,
        grid=(b, t // tb),
        in_specs=[
            col(C_MQ), col(C_MK), col(C_MV), col(C_MO),
            pl.BlockSpec((None, tb, LANES), lambda i, j: (i, j, C_GATE // LANES)),
            pl.BlockSpec((None, nc, 16, chunk), lambda i, j: (i, j, 0, 0)),
            pl.BlockSpec((1, LANES), lambda i, j: (0, 0)),
            pl.BlockSpec((16, 1), lambda i, j: (0, 0)),
            pl.BlockSpec((1, wq), lambda i, j: (0, 0)),
        ],
        out_specs=pl.BlockSpec((None, tb, wq), lambda i, j: (i, j, 0)),
        out_shape=jax.ShapeDtypeStruct((b, t, wq), BF16),
        scratch_shapes=[pltpu.VMEM((nch, chunk, LANES), F32),
                        pltpu.VMEM((nch, LANES, LANES), F32),
                        pltpu.VMEM((nch, chunk, LANES), F32),
                        pltpu.VMEM((nch, chunk, LANES), F32),
                        pltpu.VMEM((MLSTM_HEADS, LANES, LANES), F32),
                        pltpu.VMEM((8, LANES), F32)],
        compiler_params=pltpu.CompilerParams(
            dimension_semantics=("arbitrary", "arbitrary"), vmem_limit_bytes=VMEM_LIMIT),
        name="mlstm",
    )(proj3, proj3, proj3, proj3, proj3, gates_r, bias_c, bias_r, og)


def _split3(x):
    x1 = x.astype(BF16)
    r1 = x - x1.astype(F32)
    x2 = r1.astype(BF16)
    x3 = (r1 - x2.astype(F32)).astype(BF16)
    return x1, x2, x3


def _cumsum_rows(tril_b, x):
    return sum(jnp.dot(tril_b, p, preferred_element_type=F32) for p in _split3(x))


def _cumsum_lanes(x, triu_b):
    return sum(jnp.dot(p, triu_b, preferred_element_type=F32) for p in _split3(x))


def _inverse_masks(n):
    r = lax.broadcasted_iota(jnp.int32, (n, n), 0)
    c = lax.broadcasted_iota(jnp.int32, (n, n), 1)
    offs = []
    lvl = 0
    while (1 << lvl) < n:
        rb = r >> lvl
        cb = c >> lvl
        offs.append(((rb & 1) == 1) & (cb == rb - 1))
        lvl += 1
    return (r == c).astype(F32), offs


def _gdn_kernel(q_ref, k_ref, v_ref, z_ref, qh_ref, kh_ref, vh_ref, gc_ref, gr_ref, cw_ref,
                pc_ref, pr_ref, og_ref, y_ref, qs, ks, vs, xp_s, u_s, w_s, qk_s, qd_s, kdt_s, el_s, s_s,
                *, chunk, n_chunks, tb, group):
    C = chunk
    t = pl.program_id(1)
    width = GDN_HEADS * GDN_DK

    @pl.when(t == 0)
    def _():
        s_s[...] = jnp.zeros_like(s_s)

    def conv_silu(x_ref, halo_ref, w0):
        xp_s[0:8, :] = jnp.where(t == 0, 0.0, halo_ref[...])
        xp_s[8:, :] = x_ref[...]
        y = xp_s[8:, :] * cw_ref[GDN_CONV - 1:GDN_CONV, w0:w0 + width]
        for j in range(GDN_CONV - 1):
            y = y + xp_s[pl.ds(8 - (GDN_CONV - 1) + j, tb), :] * cw_ref[j:j + 1, w0:w0 + width]
        return y * _sigmoid(y)

    def l2n(x):
        parts = []
        for h in range(GDN_HEADS):
            xh = x[:, h * GDN_DK:(h + 1) * GDN_DK]
            parts.append(xh * lax.rsqrt(jnp.sum(xh * xh, axis=-1, keepdims=True) + NORM_EPS))
        return jnp.concatenate(parts, axis=1)

    qs[...] = l2n(conv_silu(q_ref, qh_ref, 0)) * (GDN_DK ** -0.5)
    ks[...] = l2n(conv_silu(k_ref, kh_ref, width))
    vs[...] = conv_silu(v_ref, vh_ref, 2 * width)

    incl = _tri(C, "incl")
    strict = _tri(C, "strict")
    tril_b = incl.astype(BF16)
    triu_b = _tri(C, "upper_incl").astype(BF16)
    inv_masks = _inverse_masks(C)

    def prep(cg, carry):
        loaded = []
        for g in range(group):
            c = cg * group + g
            r0 = pl.multiple_of(c * C, C)
            gcol = gc_ref[pl.ds(r0, C), :]
            grow = gr_ref[c]
            qkv = [(qs[pl.ds(r0, C), h * GDN_DK:(h + 1) * GDN_DK],
                    ks[pl.ds(r0, C), h * GDN_DK:(h + 1) * GDN_DK],
                    vs[pl.ds(r0, C), h * GDN_DK:(h + 1) * GDN_DK]) for h in range(GDN_HEADS)]
            loaded.append((c, gcol, grow, qkv))
        ch = []
        for c, gcol, grow, qkv in loaded:
            g_col = -jnp.exp(pc_ref[0:1, :]) * _softplus(gcol + pc_ref[1:2, :])
            gcum_col = _cumsum_rows(tril_b, g_col)
            beta_col = _sigmoid(gcol)
            g_row = -jnp.exp(pr_ref[:, 0:1]) * _softplus(grow + pr_ref[:, 1:2])
            gcum_row = _cumsum_lanes(g_row, triu_b)
            for h in range(GDN_HEADS):
                q, k, v = qkv[h]
                gc_c = gcum_col[:, h:h + 1]
                gc_r = gcum_row[h:h + 1, :]
                beta = beta_col[:, 4 + h:5 + h]
                decay = jnp.where(incl, jnp.exp(jnp.where(incl, gc_c - gc_r, 0.0)), 0.0)
                ch.append(dict(idx=c * GDN_HEADS + h, q=q, k=k, v=v, gc_c=gc_c, beta=beta,
                               decay=decay, kb=k * beta))
        for d in ch:
            d["a"] = jnp.where(strict, _mm_nt(d["kb"], d["k"]) * d["decay"], 0.0)
        eye, offs = inv_masks
        for d in ch:
            d["t"] = eye - jnp.where(offs[0], d["a"], 0.0)
        for off in offs[1:]:
            for d in ch:
                d["p"] = _mm(d["t"], jnp.where(off, d["a"], 0.0))
            for d in ch:
                d["t"] = d["t"] - _mm(d["p"], d["t"])
        for d in ch:
            d["egc"] = jnp.exp(d["gc_c"])
            d["u"] = _mm(d["t"], d["v"] * d["beta"])
        for d in ch:
            d["w"] = _mm(d["t"], d["kb"] * d["egc"]).astype(BF16)
        for d in ch:
            d["qk"] = (_mm_nt(d["q"], d["k"]) * d["decay"]).astype(BF16)
        for d in ch:
            idx = d["idx"]
            g_last = d["gc_c"][C - 1:C, :]
            u_s[idx] = d["u"]
            w_s[idx] = d["w"]
            qk_s[idx] = d["qk"]
            qd_s[idx] = (d["q"] * d["egc"]).astype(BF16)
            kdt_s[idx] = (d["k"] * jnp.exp(g_last - d["gc_c"])).T.astype(BF16)
            el_s[idx] = jnp.broadcast_to(jnp.exp(g_last), (8, LANES))
        return carry

    lax.fori_loop(0, n_chunks // group, prep, 0)

    def scan(c, carry):
        r0 = pl.multiple_of(c * C, C)
        heads = range(GDN_HEADS)
        dot = functools.partial(jnp.dot, preferred_element_type=F32)
        idx = [c * GDN_HEADS + h for h in heads]
        s_prev = [s_s[h] for h in heads]
        s_b = [s.astype(BF16) for s in s_prev]
        ws = [dot(w_s[idx[h]], s_b[h]) for h in heads]
        qs_ = [dot(qd_s[idx[h]], s_b[h]) for h in heads]
        v_b = [(u_s[idx[h]] - ws[h]).astype(BF16) for h in heads]
        upd = [dot(kdt_s[idx[h]], v_b[h]) for h in heads]
        o_all = [qs_[h] + dot(qk_s[idx[h]], v_b[h]) for h in heads]
        outs = []
        for h in heads:
            s_s[h] = s_prev[h] * el_s[idx[h]][0:1, :] + upd[h]
            o = o_all[h]
            ms = jnp.mean(o * o, axis=-1, keepdims=True)
            outs.append(o * lax.rsqrt(ms + NORM_EPS))
        y = jnp.concatenate(outs, axis=1) * og_ref[...]
        z = z_ref[pl.ds(r0, C), :]
        y = y * (z * _sigmoid(z))
        y_ref[pl.ds(r0, C), :] = y.astype(y_ref.dtype)
        return carry

    lax.fori_loop(0, n_chunks, scan, 0, unroll=True)


def _gdn(proj3, gates_r, conv_w, par_c, par_r, og, *, chunk, tb):
    b, t, _ = proj3.shape
    nc = tb // chunk
    nch = nc * GDN_HEADS
    width = GDN_HEADS * GDN_DK
    group = 4 if nc % 4 == 0 else (2 if nc % 2 == 0 else 1)
    kern = functools.partial(_gdn_kernel, chunk=chunk, n_chunks=nc, tb=tb, group=group)
    col = lambda c0: pl.BlockSpec((None, tb, width), lambda i, j, c0=c0: (i, j, c0 // width))
    halo = lambda c0: pl.BlockSpec(
        (None, 8, width), lambda i, j, c0=c0: (i, jnp.maximum(j * (tb // 8) - 1, 0), c0 // width))
    return pl.pallas_call(
        kern,
        grid=(b, t // tb),
        in_specs=[
            col(C_GQ), col(C_GK), col(C_GV), col(C_GZ),
            halo(C_GQ), halo(C_GK), halo(C_GV),
            pl.BlockSpec((None, tb, LANES), lambda i, j: (i, j, C_GATE // LANES)),
            pl.BlockSpec((None, nc, 16, chunk), lambda i, j: (i, j, 0, 0)),
            pl.BlockSpec((GDN_CONV, 3 * width), lambda i, j: (0, 0)),
            pl.BlockSpec((2, LANES), lambda i, j: (0, 0)),
            pl.BlockSpec((16, 2), lambda i, j: (0, 0)),
            pl.BlockSpec((1, width), lambda i, j: (0, 0)),
        ],
        out_specs=pl.BlockSpec((None, tb, width), lambda i, j: (i, j, 0)),
        out_shape=jax.ShapeDtypeStruct((b, t, width), BF16),
        scratch_shapes=[pltpu.VMEM((tb, width), F32), pltpu.VMEM((tb, width), F32),
                        pltpu.VMEM((tb, width), F32),
                        pltpu.VMEM((tb + 8, width), F32),
                        pltpu.VMEM((nch, chunk, GDN_DV), F32),
                        pltpu.VMEM((nch, chunk, GDN_DK), BF16),
                        pltpu.VMEM((nch, chunk, chunk), BF16),
                        pltpu.VMEM((nch, chunk, GDN_DK), BF16),
                        pltpu.VMEM((nch, GDN_DK, chunk), BF16),
                        pltpu.VMEM((nch, 8, LANES), F32),
                        pltpu.VMEM((GDN_HEADS, GDN_DK, GDN_DV), F32)],
        compiler_params=pltpu.CompilerParams(
            dimension_semantics=("arbitrary", "arbitrary"), vmem_limit_bytes=VMEM_LIMIT),
        name="gdn",
    )(proj3, proj3, proj3, proj3, proj3, proj3, proj3, proj3, gates_r, conv_w, par_c, par_r, og)


def _dsa_prep_kernel(cq_ref, ckv_ref, idx_ref, qn_ref, kn_ref, wuq_ref, wqi_ref, wuk_ref,
                     qlat_ref, qidx_ref, ckvn_ref, ckvt_ref, kidx_ref):
    cq = cq_ref[...]
    ms = jnp.mean(cq * cq, axis=-1, keepdims=True)
    cqn = ((cq * lax.rsqrt(ms + NORM_EPS)) * qn_ref[...]).astype(BF16)
    ckv = ckv_ref[...]
    ms = jnp.mean(ckv * ckv, axis=-1, keepdims=True)
    ckvn = (ckv * lax.rsqrt(ms + NORM_EPS)) * kn_ref[...]
    ckvn_ref[...] = ckvn.astype(BF16)
    ckvt_ref[...] = ckvn.T.astype(BF16)
    q = jnp.dot(cqn, wuq_ref[...], preferred_element_type=F32)
    qlat = jnp.dot(q.astype(BF16), wuk_ref[...], preferred_element_type=F32) * (DSA_HEAD_DIM ** -0.5)
    qlat_ref[...] = qlat.astype(BF16)
    qidx_ref[...] = jnp.dot(cqn, wqi_ref[...], preferred_element_type=F32).astype(BF16)
    kidx_ref[...] = idx_ref[:, 0:IDX_DIM].astype(BF16)


def _dsa_prep(proj2, qn, kn, wuq, wqi, wuk_bd):
    m = proj2.shape[0]
    tm = 512
    full = lambda shape: pl.BlockSpec(shape, lambda i: (0, 0))
    return pl.pallas_call(
        _dsa_prep_kernel,
        grid=(m // tm,),
        in_specs=[
            pl.BlockSpec((tm, DSA_Q_RANK), lambda i: (i, C_CQ // DSA_Q_RANK)),
            pl.BlockSpec((tm, DSA_KV_RANK), lambda i: (i, C_CKV // DSA_KV_RANK)),
            pl.BlockSpec((tm, LANES), lambda i: (i, C_IDX // LANES)),
            full((1, DSA_Q_RANK)), full((1, DSA_KV_RANK)),
            full((DSA_Q_RANK, DSA_HEADS * DSA_HEAD_DIM)),
            full((DSA_Q_RANK, IDX_HEADS * IDX_DIM)),
            full((DSA_HEADS * DSA_HEAD_DIM, DSA_HEADS * DSA_KV_RANK)),
        ],
        out_specs=[
            pl.BlockSpec((tm, DSA_HEADS * DSA_KV_RANK), lambda i: (i, 0)),
            pl.BlockSpec((tm, IDX_HEADS * IDX_DIM), lambda i: (i, 0)),
            pl.BlockSpec((tm, DSA_KV_RANK), lambda i: (i, 0)),
            pl.BlockSpec((DSA_KV_RANK, tm), lambda i: (0, i)),
            pl.BlockSpec((tm, IDX_DIM), lambda i: (i, 0)),
        ],
        out_shape=[
            jax.ShapeDtypeStruct((m, DSA_HEADS * DSA_KV_RANK), BF16),
            jax.ShapeDtypeStruct((m, IDX_HEADS * IDX_DIM), BF16),
            jax.ShapeDtypeStruct((m, DSA_KV_RANK), BF16),
            jax.ShapeDtypeStruct((DSA_KV_RANK, m), BF16),
            jax.ShapeDtypeStruct((m, IDX_DIM), BF16),
        ],
        compiler_params=pltpu.CompilerParams(
            dimension_semantics=("arbitrary",), vmem_limit_bytes=VMEM_LIMIT),
        name="dsa_prep",
    )(proj2, proj2, proj2, qn, kn, wuq, wqi, wuk_bd)


def _tree_sum(parts):
    while len(parts) > 1:
        parts = [parts[j] + parts[j + 1] for j in range(0, len(parts) - 1, 2)] + (
            [parts[-1]] if len(parts) % 2 else [])
    return parts[0]


def _dsa_kernel(qidx_ref, wrow_ref, qlat_ref, kidx_ref, ckv_ref, ckvt_ref, wuvt_ref, y_ref,
                key_s, hi_s, lo_s, cut_s, acc_s, *, qb, tk, n_sel):
    i = pl.program_id(1)
    n_tiles = (i * qb + qb + tk - 1) // tk
    key_pos = lax.broadcasted_iota(jnp.int32, (tk, qb), 0)
    q_pos = i * qb + lax.broadcasted_iota(jnp.int32, (tk, qb), 1)
    w_rows = wrow_ref[...] * (IDX_HEADS ** -0.5 * IDX_DIM ** -0.5)
    qidx = qidx_ref[...]
    q_heads = [qidx[:, h * IDX_DIM:(h + 1) * IDX_DIM] for h in range(IDX_HEADS)]
    i16_min = -(1 << 15)
    i16_max = (1 << 15) - 1
    sub16 = 16
    nt_dims = (((1,), (1,)), ((), ()))

    def score_body(kt, carry):
        k0 = pl.multiple_of(kt * tk, tk)
        kk = kidx_ref[pl.ds(k0, tk), :]
        sh = [lax.dot_general(kk, q_heads[h], nt_dims, preferred_element_type=F32)
              for h in range(IDX_HEADS)]
        acc = jnp.zeros((tk, qb), F32)
        for h in range(IDX_HEADS):
            acc = acc + jnp.maximum(sh[h], 0.0) * w_rows[h:h + 1, :]
        acc = acc + 0.0
        bits = pltpu.bitcast(acc, jnp.int32)
        keys = jnp.where(bits < 0, bits ^ jnp.int32(0x7FFFFFFF), bits)
        keys = jnp.where(k0 + key_pos <= q_pos, keys, jnp.int32(INT_MIN))
        key_s[kt] = keys
        hi_s[kt] = (keys >> 16).astype(jnp.int16)
        return carry

    lax.fori_loop(0, n_tiles, score_body, 0)

    def count_ge(src_s, cand):
        cand16 = jnp.broadcast_to(cand, (sub16, qb)).astype(jnp.int16)
        one, zero = jnp.int16(1), jnp.int16(0)

        def body(kt, acc):
            tile = src_s[kt]
            return acc + _tree_sum([jnp.where(tile[j * sub16:(j + 1) * sub16, :] >= cand16, one, zero)
                                    for j in range(tk // sub16)])

        cnt = lax.fori_loop(0, n_tiles, body, jnp.zeros((sub16, qb), jnp.int16))
        return jnp.sum(cnt.astype(jnp.int32), axis=0, keepdims=True)

    def kth_largest16(src_s, rank):
        thr = jnp.full((1, qb), i16_min, jnp.int32)
        n_ge = jnp.zeros((1, qb), jnp.int32) + n_tiles * tk
        n_gt = jnp.zeros((1, qb), jnp.int32)
        for bit in range(15, -1, -1):
            cand = thr + jnp.int32(1 << bit)
            cnt = count_ge(src_s, cand)
            ok = cnt >= rank
            thr = jnp.where(ok, cand, thr)
            n_ge = jnp.where(ok, cnt, n_ge)
            n_gt = jnp.where(ok, n_gt, cnt)
        return thr, n_ge, n_gt

    hi_thr, hi_ge, above = kth_largest16(hi_s, jnp.int32(n_sel))
    rank_lo = jnp.int32(n_sel) - above

    def lo_body(kt, carry):
        keys = key_s[kt]
        lo = (keys & jnp.int32(0xFFFF)) + jnp.int32(i16_min)
        lo_s[kt] = jnp.where((keys >> 16) == hi_thr, lo, jnp.int32(i16_min)).astype(jnp.int16)
        return carry

    lax.fori_loop(0, n_tiles, lo_body, 0)
    lo_thr, lo_ge, _ = kth_largest16(lo_s, rank_lo)
    thr_raw = hi_thr * jnp.int32(1 << 16) + (lo_thr - jnp.int32(i16_min))
    thr = jnp.maximum(thr_raw, jnp.int32(INT_MIN + 1))

    in_bucket = jnp.where(lo_thr > i16_min, lo_ge, hi_ge - above)
    excess = jnp.where(thr_raw > INT_MIN, above + in_bucket - jnp.int32(n_sel), 0)

    max_excess = jnp.max(excess)
    last = tk * key_s.shape[0] - 1
    assert last <= i16_max
    direct_limit = max(last, 1).bit_length()

    @pl.when(max_excess > 0)
    def _():
        def fill(kt, carry):
            rev = jnp.int32(last) - (kt * tk + key_pos)
            lo_s[kt] = jnp.where(key_s[kt] == thr_raw, rev, jnp.int32(-1)).astype(jnp.int16)
            return carry

        lax.fori_loop(0, n_tiles, fill, 0)

        @pl.when(max_excess <= direct_limit)
        def _():
            def next_tied(prev):
                prev16 = jnp.broadcast_to(prev, (sub16, qb)).astype(jnp.int16)
                big = jnp.int16(i16_max)

                def body(kt, acc):
                    tile = lo_s[kt]
                    parts = [jnp.where(tile[j * sub16:(j + 1) * sub16, :] > prev16,
                                       tile[j * sub16:(j + 1) * sub16, :], big)
                             for j in range(tk // sub16)]
                    while len(parts) > 1:
                        parts = [jnp.where(parts[j] < parts[j + 1], parts[j], parts[j + 1])
                                 for j in range(0, len(parts), 2)]
                    return jnp.where(acc < parts[0], acc, parts[0])

                low = lax.fori_loop(0, n_tiles, body, jnp.full((sub16, qb), i16_max, jnp.int16))
                return jnp.min(low.astype(jnp.int32), axis=0, keepdims=True)

            def step(j, carry):
                prev, cut = carry
                cur = next_tied(prev)
                return cur, jnp.where(j < excess, cur + 1, cut)

            _, cut = lax.fori_loop(0, max_excess, step,
                                   (jnp.full((1, qb), -1, jnp.int32), jnp.zeros((1, qb), jnp.int32)))
            cut_s[0:1, :] = cut

        @pl.when(max_excess > direct_limit)
        def _():
            keep = count_ge(lo_s, jnp.zeros((1, qb), jnp.int32)) - excess
            cut = jnp.zeros((1, qb), jnp.int32)
            for bit in range(direct_limit - 1, -1, -1):
                cand = cut + jnp.int32(1 << bit)
                cut = jnp.where(count_ge(lo_s, cand) >= keep, cand, cut)
            cut_s[0:1, :] = jnp.where(excess > 0, cut, 0)

        cut = cut_s[0:1, :]

        def demote(kt, carry):
            keys = key_s[kt]
            rev = jnp.int32(last) - (kt * tk + key_pos)
            demoted = jnp.where(rev < cut, jnp.int32(INT_MIN), keys)
            key_s[kt] = jnp.where(keys == thr_raw, demoted, keys)
            return carry

        lax.fori_loop(0, n_tiles, demote, 0)

    qlat = qlat_ref[...]
    heads = range(DSA_HEADS)
    ql = [qlat[:, h * DSA_KV_RANK:(h + 1) * DSA_KV_RANK] for h in heads]
    acc_s[...] = jnp.zeros_like(acc_s)

    def attn_body(kt, carry):
        m_run, l_run = carry
        k0 = pl.multiple_of(kt * tk, tk)
        ckv = ckv_ref[pl.ds(k0, tk), :]
        ckvt = ckvt_ref[:, pl.ds(k0, tk)]
        bias = jnp.where(key_s[kt] >= thr, 0.0, -jnp.inf)
        lg = [lax.dot_general(ckv, ql[h], nt_dims, preferred_element_type=F32) for h in heads]
        m_out, l_out, p_all, alphas = [], [], [], []
        for h in heads:
            lgm = lg[h] + bias
            m_new = jnp.maximum(m_run[h], jnp.max(lgm, axis=0, keepdims=True))
            p = jnp.exp(lgm - m_new)
            alpha = jnp.exp(m_run[h] - m_new)
            l_out.append(alpha * l_run[h] + jnp.sum(p, axis=0, keepdims=True))
            m_out.append(m_new)
            p_all.append(p.astype(BF16))
            alphas.append(alpha)
        pv = [jnp.dot(ckvt, p_all[h], preferred_element_type=F32) for h in heads]
        for h in heads:
            acc_s[h] = alphas[h] * acc_s[h] + pv[h]
        return tuple(m_out), tuple(l_out)

    init = (tuple(jnp.full((1, qb), -1e30, F32) for _ in heads),
            tuple(jnp.zeros((1, qb), F32) for _ in heads))
    _, l_fin = lax.fori_loop(0, n_tiles, attn_body, init)
    o_lat = [(acc_s[h] / l_fin[h]).astype(BF16) for h in heads]
    outs = [jnp.dot(wuvt_ref[h], o_lat[h], preferred_element_type=F32) for h in heads]
    y_ref[...] = jnp.concatenate(outs, axis=0).T.astype(y_ref.dtype)


def _dsa(qidx, widx_rows, qlat, kidx, ckvn, ckvt, wuvt, *, qb, tk):
    b, t, _ = qidx.shape
    n_sel = min(IDX_TOPK, t // 4)
    kern = functools.partial(_dsa_kernel, qb=qb, tk=tk, n_sel=n_sel)
    return pl.pallas_call(
        kern,
        grid=(b, t // qb),
        in_specs=[
            pl.BlockSpec((None, qb, IDX_HEADS * IDX_DIM), lambda i, j: (i, j, 0)),
            pl.BlockSpec((None, IDX_HEADS, qb), lambda i, j: (i, 0, j)),
            pl.BlockSpec((None, qb, DSA_HEADS * DSA_KV_RANK), lambda i, j: (i, j, 0)),
            pl.BlockSpec((None, t, IDX_DIM), lambda i, j: (i, 0, 0)),
            pl.BlockSpec((None, t, DSA_KV_RANK), lambda i, j: (i, 0, 0)),
            pl.BlockSpec((DSA_KV_RANK, t), lambda i, j: (0, i)),
            pl.BlockSpec((DSA_HEADS, DSA_HEAD_DIM, DSA_KV_RANK), lambda i, j: (0, 0, 0)),
        ],
        out_specs=pl.BlockSpec((None, qb, DSA_HEADS * DSA_HEAD_DIM), lambda i, j: (i, j, 0)),
        out_shape=jax.ShapeDtypeStruct((b, t, DSA_HEADS * DSA_HEAD_DIM), BF16),
        scratch_shapes=[pltpu.VMEM((t // tk, tk, qb), jnp.int32),
                        pltpu.VMEM((t // tk, tk, qb), jnp.int16),
                        pltpu.VMEM((t // tk, tk, qb), jnp.int16),
                        pltpu.VMEM((8, qb), jnp.int32),
                        pltpu.VMEM((DSA_HEADS, DSA_KV_RANK, qb), F32)],
        compiler_params=pltpu.CompilerParams(
            dimension_semantics=("arbitrary", "arbitrary"), vmem_limit_bytes=VMEM_LIMIT),
        name="dsa",
    )(qidx, widx_rows, qlat, kidx, ckvn, ckvt, wuvt)


def _pack_w_in(w):
    pts = [0]
    for wd in IN_WIDTHS:
        pts.append(pts[-1] + wd)
    seg = [w[..., pts[i]:pts[i + 1]] for i in range(len(IN_WIDTHS))]
    (gq, gk, gv, gz, ga, gb, cq, ckv, ik, iw, mq, mk, mv, mo, mi, mf) = seg
    zeros = lambda n: jnp.zeros(w.shape[:-1] + (n,), w.dtype)
    idx_blk = jnp.concatenate([ik, iw, zeros(LANES - IDX_DIM - IDX_HEADS)], axis=-1)
    gate_blk = jnp.concatenate([ga, gb, mi, mf, zeros(LANES - 16)], axis=-1)
    out = jnp.concatenate([gq, gk, gv, gz, cq, ckv, idx_blk, mq, mk, mv, mo, gate_blk], axis=-1)
    assert out.shape[-1] == NP
    return out.astype(BF16)


def _lane_rows(vals_at, depth):
    rows = jnp.zeros((depth, LANES), F32)
    for start, v in vals_at:
        rows = rows.at[:, start:start + v.shape[1]].set(v.astype(F32))
    return rows


def kernel(x, attn_norm, w_in, gdn_conv, gdn_a_log, gdn_dt_bias, gdn_out_norm, dsa_q_norm,
           dsa_kv_norm, dsa_w_uq, dsa_w_qidx, dsa_w_uk, dsa_w_uv, mlstm_i_bias, mlstm_f_bias,
           mlstm_out_norm, w_out, ffn_norm, w_gate, w_up, w_down, final_norm):
    b, t, d = x.shape
    depth = w_in.shape[0]
    gdn_chunk, mlstm_chunk = min(128, t), min(64, t)
    tb = min(512, t)
    qb = min(512, t)
    tk = min(512, t)

    w_in_p = _pack_w_in(w_in)
    gdn_par_c = jnp.stack([_lane_rows([(0, gdn_a_log)], depth), _lane_rows([(0, gdn_dt_bias)], depth)],
                          axis=1)
    gdn_par_r = jnp.swapaxes(gdn_par_c[:, :, :16], 1, 2)
    gdn_og = jnp.tile(gdn_out_norm, (1, GDN_HEADS))
    wuk_bd = jnp.zeros((depth, DSA_HEADS * DSA_HEAD_DIM, DSA_HEADS * DSA_KV_RANK), F32)
    for h in range(DSA_HEADS):
        wuk_bd = wuk_bd.at[:, h * DSA_HEAD_DIM:(h + 1) * DSA_HEAD_DIM,
                           h * DSA_KV_RANK:(h + 1) * DSA_KV_RANK].set(jnp.swapaxes(dsa_w_uk[:, h], 1, 2))
    wuk_bd = wuk_bd.astype(BF16)
    wuq_b, wqi_b = dsa_w_uq.astype(BF16), dsa_w_qidx.astype(BF16)
    wuvt_b = jnp.swapaxes(dsa_w_uv, 2, 3).astype(BF16)
    ml_bias_c = _lane_rows([(8, mlstm_i_bias), (12, mlstm_f_bias)], depth)
    ml_og = jnp.tile(mlstm_out_norm, (1, MLSTM_HEADS))
    w_out_b, w_gate_b, w_up_b, w_down_b = (a.astype(BF16) for a in (w_out, w_gate, w_up, w_down))

    x2 = x.reshape(b * t, d)
    for l in range(depth):
        proj2 = _inproj(x2, attn_norm[l].reshape(1, d), w_in_p[l])
        proj3 = proj2.reshape(b, t, NP)
        gates_t = jnp.swapaxes(proj3[:, :, C_GATE:C_GATE + 16], 1, 2)

        def chunk_rows(c):
            return jnp.swapaxes(gates_t.reshape(b, 16, t // c, c), 1, 2)

        y_a = _gdn(proj3, chunk_rows(gdn_chunk), gdn_conv[l], gdn_par_c[l], gdn_par_r[l],
                   gdn_og[l].reshape(1, -1), chunk=gdn_chunk, tb=tb)

        qlat, qidx, ckvn, ckvt, kidx = _dsa_prep(
            proj2, dsa_q_norm[l].reshape(1, -1), dsa_kv_norm[l].reshape(1, -1),
            wuq_b[l], wqi_b[l], wuk_bd[l])
        r3 = lambda a: a.reshape(b, t, a.shape[-1])
        widx_rows = jnp.swapaxes(proj3[:, :, C_IDX + IDX_DIM:C_IDX + IDX_DIM + IDX_HEADS], 1, 2)
        y_b = _dsa(r3(qidx), widx_rows, r3(qlat), r3(kidx), r3(ckvn), ckvt, wuvt_b[l], qb=qb, tk=tk)

        y_c = _mlstm(proj3, chunk_rows(mlstm_chunk), ml_bias_c[l].reshape(1, LANES),
                     ml_bias_c[l, :16].reshape(16, 1), ml_og[l].reshape(1, -1),
                     chunk=mlstm_chunk, tb=tb)

        x2 = _out_ffn(x2, y_a.reshape(b * t, -1), y_b.reshape(b * t, -1), y_c.reshape(b * t, -1),
                      w_out_b[l], ffn_norm[l].reshape(1, d), w_gate_b[l], w_up_b[l], w_down_b[l],
                      final_norm.reshape(1, d), final_norm=(l == depth - 1))
    return x2.reshape(b, t, d)
```

```python
import functools
import math

import jax
import jax.numpy as jnp
from jax import lax
from jax.experimental import pallas as pl
from jax.experimental.pallas import tpu as pltpu

F32 = jnp.float32
BF16 = jnp.bfloat16
HIGHEST = lax.Precision.HIGHEST

D_MODEL = 1024
GDN_HEADS = 4
GDN_DK = 128
GDN_DV = 128
GDN_CONV = 4
DSA_HEADS = 4
DSA_HEAD_DIM = 64
DSA_Q_RANK = 256
DSA_KV_RANK = 128
IDX_HEADS = 8
IDX_DIM = 32
IDX_TOPK = 256
MLSTM_HEADS = 4
MLSTM_DQK = 64
MLSTM_DV = 64
FFN_HIDDEN = 2816
NORM_EPS = 1e-6

IN_WIDTHS = (512, 512, 512, 512, 4, 4, 256, 128, 32, 8, 256, 256, 256, 256, 4, 4)

C_GQ, C_GK, C_GV, C_GZ = 0, 512, 1024, 1536
C_CQ, C_CKV, C_IDX = 2048, 2304, 2432
C_MQ, C_MK, C_MV, C_MO = 2560, 2816, 3072, 3328
C_GATE = 3584
NP = 3712
LANES = 128

VMEM_LIMIT = 56 * 1024 * 1024

INT_MIN = -2 ** 31


def _mm(a, b):
    return jnp.dot(a.astype(BF16), b.astype(BF16), preferred_element_type=F32)


def _mm_nt(a, b):
    return lax.dot_general(a.astype(BF16), b.astype(BF16), (((1,), (1,)), ((), ())),
                           preferred_element_type=F32)


def _sigmoid(x):
    return 1.0 / (1.0 + jnp.exp(-x))


def _softplus(x):
    return jnp.maximum(x, 0.0) + jnp.log1p(jnp.exp(-jnp.abs(x)))


def _tri(n, kind):
    r = lax.broadcasted_iota(jnp.int32, (n, n), 0)
    c = lax.broadcasted_iota(jnp.int32, (n, n), 1)
    if kind == "incl":
        return r >= c
    if kind == "strict":
        return r > c
    if kind == "upper_incl":
        return r <= c
    raise ValueError(kind)


def _inproj_kernel(x_ref, g_ref, w_ref, o_ref, gt_ref, wr_ref):
    x = x_ref[...]
    ms = jnp.mean(x * x, axis=-1, keepdims=True)
    h = (x * lax.rsqrt(ms + NORM_EPS)) * g_ref[...]
    o = jnp.dot(h.astype(BF16), w_ref[...], preferred_element_type=F32)
    o_ref[...] = o
    gt_ref[...] = o[:, C_GATE:C_GATE + LANES].T[0:16, :]
    wr_ref[...] = o[:, C_IDX:C_IDX + LANES].T[IDX_DIM:IDX_DIM + IDX_HEADS, :]


def _inproj(x2, g, w):
    m = x2.shape[0]
    tm = 256
    return pl.pallas_call(
        _inproj_kernel,
        grid=(m // tm,),
        in_specs=[
            pl.BlockSpec((tm, D_MODEL), lambda i: (i, 0)),
            pl.BlockSpec((1, D_MODEL), lambda i: (0, 0)),
            pl.BlockSpec((D_MODEL, NP), lambda i: (0, 0)),
        ],
        out_specs=[pl.BlockSpec((tm, NP), lambda i: (i, 0)),
                   pl.BlockSpec((16, tm), lambda i: (0, i)),
                   pl.BlockSpec((IDX_HEADS, tm), lambda i: (0, i))],
        out_shape=[jax.ShapeDtypeStruct((m, NP), F32),
                   jax.ShapeDtypeStruct((16, m), F32),
                   jax.ShapeDtypeStruct((IDX_HEADS, m), F32)],
        compiler_params=pltpu.CompilerParams(
            dimension_semantics=("arbitrary",), vmem_limit_bytes=VMEM_LIMIT),
        name="inproj",
    )(x2, g, w)


def _ffn_kernel(x_ref, ya_ref, yb_ref, yc_ref, wo_ref, g_ref, wg_ref, wu_ref, wd_ref, fg_ref,
                o_ref, acc_ref, h_ref, *, n_hidden_steps, final_norm):
    j = pl.program_id(1)

    @pl.when(j == 0)
    def _():
        xn = x_ref[...]
        xn = xn + jnp.dot(ya_ref[...], wo_ref[0:512, :], preferred_element_type=F32)
        xn = xn + jnp.dot(yb_ref[...], wo_ref[512:768, :], preferred_element_type=F32)
        xn = xn + jnp.dot(yc_ref[...], wo_ref[768:1024, :], preferred_element_type=F32)
        acc_ref[...] = xn
        ms = jnp.mean(xn * xn, axis=-1, keepdims=True)
        h_ref[...] = ((xn * lax.rsqrt(ms + NORM_EPS)) * g_ref[...]).astype(BF16)

    h = h_ref[...]
    gate = jnp.dot(h, wg_ref[...], preferred_element_type=F32)
    up = jnp.dot(h, wu_ref[...], preferred_element_type=F32)
    act = (gate * _sigmoid(gate)) * up
    acc_ref[...] += jnp.dot(act.astype(BF16), wd_ref[...], preferred_element_type=F32)

    @pl.when(j == n_hidden_steps - 1)
    def _():
        y = acc_ref[...]
        if final_norm:
            ms = jnp.mean(y * y, axis=-1, keepdims=True)
            y = (y * lax.rsqrt(ms + NORM_EPS)) * fg_ref[...]
        o_ref[...] = y


def _out_ffn(x2, ya, yb, yc, wo, g, wg, wu, wd, fg, final_norm):
    m = x2.shape[0]
    tm = 512
    th = 2816
    nh = FFN_HIDDEN // th
    kern = functools.partial(_ffn_kernel, n_hidden_steps=nh, final_norm=final_norm)
    return pl.pallas_call(
        kern,
        grid=(m // tm, nh),
        in_specs=[
            pl.BlockSpec((tm, D_MODEL), lambda i, j: (i, 0)),
            pl.BlockSpec((tm, 512), lambda i, j: (i, 0)),
            pl.BlockSpec((tm, 256), lambda i, j: (i, 0)),
            pl.BlockSpec((tm, 256), lambda i, j: (i, 0)),
            pl.BlockSpec((D_MODEL, D_MODEL), lambda i, j: (0, 0), pipeline_mode=pl.Buffered(1)),
            pl.BlockSpec((1, D_MODEL), lambda i, j: (0, 0)),
            pl.BlockSpec((D_MODEL, th), lambda i, j: (0, j), pipeline_mode=pl.Buffered(1)),
            pl.BlockSpec((D_MODEL, th), lambda i, j: (0, j), pipeline_mode=pl.Buffered(1)),
            pl.BlockSpec((th, D_MODEL), lambda i, j: (j, 0), pipeline_mode=pl.Buffered(1)),
            pl.BlockSpec((1, D_MODEL), lambda i, j: (0, 0)),
        ],
        out_specs=pl.BlockSpec((tm, D_MODEL), lambda i, j: (i, 0)),
        out_shape=jax.ShapeDtypeStruct((m, D_MODEL), F32),
        scratch_shapes=[pltpu.VMEM((tm, D_MODEL), F32), pltpu.VMEM((tm, D_MODEL), BF16)],
        compiler_params=pltpu.CompilerParams(
            dimension_semantics=("arbitrary", "arbitrary"), vmem_limit_bytes=VMEM_LIMIT),
        name="out_ffn",
    )(x2, ya, yb, yc, wo, g, wg, wu, wd, fg)


def _cummax_rows(x, n):
    neg = -jnp.inf
    row = lax.broadcasted_iota(jnp.int32, x.shape, 0)
    s = 1
    while s < n:
        if s < 8:
            shifted = jnp.where(row >= s, pltpu.roll(x, s, axis=0), neg)
        else:
            shifted = jnp.concatenate([jnp.full((s, x.shape[1]), neg, x.dtype), x[:n - s]], axis=0)
        x = jnp.maximum(x, shifted)
        s *= 2
    return x


def _mlstm_kernel(q_ref, k_ref, v_ref, o_ref, gc_ref, gr_ref, bc_ref, br_ref, og_ref,
                  y_ref, sv_s, kv_s, b_s, dm_s, c_s, m_s, *, chunk, n_chunks, group):
    L = chunk
    t = pl.program_id(1)
    heads = range(MLSTM_HEADS)
    half = MLSTM_DQK

    @pl.when(t == 0)
    def _():
        c_s[...] = jnp.zeros_like(c_s)
        m_s[...] = jnp.zeros_like(m_s)

    incl = _tri(L, "incl")
    tril_b = incl.astype(BF16)
    triu_b = _tri(L, "upper_incl").astype(BF16)
    lane = lax.broadcasted_iota(jnp.int32, (L, LANES), 1)
    own = [(lane < half) if h % 2 == 0 else (lane >= half) for h in heads]
    row_c = lax.broadcasted_iota(jnp.int32, (LANES, LANES), 0)
    own_rows = [(row_c < half) if h % 2 == 0 else (row_c >= half) for h in heads]
    scale = MLSTM_DQK ** -0.5

    def pair(ref, r0, h):
        return ref[pl.ds(r0, L), (h // 2) * LANES:(h // 2 + 1) * LANES]

    def prep(cg, carry):
        ch = []
        span = group * L
        rows = gr_ref[:, pl.ds(pl.multiple_of(cg * span, span), span)] + br_ref[...]
        for g in range(group):
            c = cg * group + g
            r0 = pl.multiple_of(c * L, L)
            x_col = gc_ref[pl.ds(r0, L), :] + bc_ref[...]
            x_row = rows[:, g * L:(g + 1) * L]
            b_row = _cumsum_lanes(-_softplus(-x_row), triu_b)
            f_rep = jnp.concatenate(
                [jnp.broadcast_to(x_col[:, 12 + h:13 + h], (L, LANES)) for h in heads], axis=1)
            b_rep = _cumsum_rows(tril_b, -_softplus(-f_rep))
            for h in heads:
                bx = b_rep[:, h * LANES:(h + 1) * LANES]
                cx = jnp.broadcast_to(x_col[:, 8 + h:9 + h], (L, LANES)) - bx
                dmax = bx + _cummax_rows(cx, L)
                c_row = x_row[8 + h:9 + h, :] - b_row[12 + h:13 + h, :]
                d_log = jnp.where(incl, bx[:, :L] + c_row - dmax[:, :L], -jnp.inf)
                kw = jnp.exp(cx + (bx[L - 1:L, :] - dmax[L - 1:L, :]))
                kp = pair(k_ref, r0, h) * scale
                vp = pair(v_ref, r0, h)
                ch.append(dict(idx=c * MLSTM_HEADS + h, h=h,
                               q=jnp.where(own[h], pair(q_ref, r0, h), 0.0), k=kp,
                               v=jnp.where(own[h], vp, 1.0), dw=jnp.exp(d_log),
                               kxw_t=(kp * kw).T, bx=bx, dmax=dmax))
        for d in ch:
            d["s"] = _mm_nt(d["q"], d["k"]) * d["dw"]
        for d in ch:
            d["sv"] = _mm(d["s"], d["v"])
        for d in ch:
            d["kv"] = jnp.where(own_rows[d["h"]], _mm(d["kxw_t"], d["v"]), 0.0)
        for d in ch:
            sv_s[d["idx"]] = d["sv"]
            kv_s[d["idx"]] = d["kv"]
            b_s[d["idx"]] = d["bx"]
            dm_s[d["idx"]] = d["dmax"]
        return carry

    lax.fori_loop(0, n_chunks // group, prep, 0)

    def scan(c, carry):
        r0 = pl.multiple_of(c * L, L)
        c_prev = [c_s[h] for h in heads]
        qc = [_mm(pair(q_ref, r0, h), c_prev[h]) for h in heads]
        outs = []
        for h in heads:
            idx = c * MLSTM_HEADS + h
            bx = b_s[idx]
            dmax = dm_s[idx]
            m_prev = m_s[h:h + 1, :]
            a_log = bx + m_prev
            m_t = jnp.maximum(a_log, dmax)
            num = jnp.exp(a_log - m_t) * qc[h] + jnp.exp(dmax - m_t) * sv_s[idx]
            den = pltpu.roll(num, half, axis=1)
            hh = jnp.where(own[h], num / jnp.maximum(jnp.abs(den), jnp.exp(-m_t)), 0.0)
            m_new = m_t[L - 1:L, :]
            dec = jnp.exp(bx[L - 1:L, :] + m_prev - m_new)
            c_s[h] = dec * c_prev[h] + jnp.exp(dmax[L - 1:L, :] - m_new) * kv_s[idx]
            m_s[h:h + 1, :] = m_new
            ms = jnp.sum(hh * hh, axis=-1, keepdims=True) * (1.0 / MLSTM_DV)
            outs.append(hh * lax.rsqrt(ms + NORM_EPS))
        y = jnp.concatenate([outs[0] + outs[1], outs[2] + outs[3]], axis=1) * og_ref[...]
        y = y * _sigmoid(o_ref[pl.ds(r0, L), :])
        y_ref[pl.ds(r0, L), :] = y.astype(y_ref.dtype)
        return carry

    lax.fori_loop(0, n_chunks, scan, 0, unroll=2 if n_chunks % 2 == 0 else 1)


def _mlstm(proj3, gates_r, bias_c, bias_r, og, *, chunk, tb):
    b, t, _ = proj3.shape
    nc = tb // chunk
    nch = nc * MLSTM_HEADS
    wq = MLSTM_HEADS * MLSTM_DQK
    group = 8 if nc % 8 == 0 else (2 if nc % 2 == 0 else 1)
    assert (group * chunk) % LANES == 0
    kern = functools.partial(_mlstm_kernel, chunk=chunk, n_chunks=nc, group=group)
    col = lambda c0: pl.BlockSpec((None, tb, wq), lambda i, j, c0=c0: (i, j, c0 // wq))
    return pl.pallas_call(
        kern,
        grid=(b, t // tb),
        in_specs=[
            col(C_MQ), col(C_MK), col(C_MV), col(C_MO),
            pl.BlockSpec((None, tb, LANES), lambda i, j: (i, j, C_GATE // LANES)),
            pl.BlockSpec((16, tb), lambda i, j: (0, i * (t // tb) + j)),
            pl.BlockSpec((1, LANES), lambda i, j: (0, 0)),
            pl.BlockSpec((16, 1), lambda i, j: (0, 0)),
            pl.BlockSpec((1, wq), lambda i, j: (0, 0)),
        ],
        out_specs=pl.BlockSpec((None, tb, wq), lambda i, j: (i, j, 0)),
        out_shape=jax.ShapeDtypeStruct((b, t, wq), BF16),
        scratch_shapes=[pltpu.VMEM((nch, chunk, LANES), F32),
                        pltpu.VMEM((nch, LANES, LANES), F32),
                        pltpu.VMEM((nch, chunk, LANES), F32),
                        pltpu.VMEM((nch, chunk, LANES), F32),
                        pltpu.VMEM((MLSTM_HEADS, LANES, LANES), F32),
                        pltpu.VMEM((8, LANES), F32)],
        compiler_params=pltpu.CompilerParams(
            dimension_semantics=("arbitrary", "arbitrary"), vmem_limit_bytes=VMEM_LIMIT),
        name="mlstm",
    )(proj3, proj3, proj3, proj3, proj3, gates_r, bias_c, bias_r, og)


def _split3(x):
    x1 = x.astype(BF16)
    r1 = x - x1.astype(F32)
    x2 = r1.astype(BF16)
    x3 = (r1 - x2.astype(F32)).astype(BF16)
    return x1, x2, x3


def _cumsum_rows(tril_b, x):
    return sum(jnp.dot(tril_b, p, preferred_element_type=F32) for p in _split3(x))


def _cumsum_lanes(x, triu_b):
    return sum(jnp.dot(p, triu_b, preferred_element_type=F32) for p in _split3(x))


def _inverse_masks(n):
    r = lax.broadcasted_iota(jnp.int32, (n, n), 0)
    c = lax.broadcasted_iota(jnp.int32, (n, n), 1)
    offs = []
    lvl = 0
    while (1 << lvl) < n:
        rb = r >> lvl
        cb = c >> lvl
        offs.append(((rb & 1) == 1) & (cb == rb - 1))
        lvl += 1
    return (r == c).astype(F32), offs


def _gdn_kernel(q_ref, k_ref, v_ref, z_ref, qh_ref, kh_ref, vh_ref, gc_ref, gr_ref, cw_ref,
                pc_ref, pr_ref, og_ref, y_ref, qs, ks, vs, xp_s, u_s, w_s, qk_s, qd_s, kdt_s, el_s, s_s,
                *, chunk, n_chunks, tb, group):
    C = chunk
    t = pl.program_id(1)
    width = GDN_HEADS * GDN_DK

    @pl.when(t == 0)
    def _():
        s_s[...] = jnp.zeros_like(s_s)

    def conv_silu(x_ref, halo_ref, w0):
        xp_s[0:8, :] = jnp.where(t == 0, 0.0, halo_ref[...])
        xp_s[8:, :] = x_ref[...]
        y = xp_s[8:, :] * cw_ref[GDN_CONV - 1:GDN_CONV, w0:w0 + width]
        for j in range(GDN_CONV - 1):
            y = y + xp_s[pl.ds(8 - (GDN_CONV - 1) + j, tb), :] * cw_ref[j:j + 1, w0:w0 + width]
        return y * _sigmoid(y)

    def l2n(x):
        parts = []
        for h in range(GDN_HEADS):
            xh = x[:, h * GDN_DK:(h + 1) * GDN_DK]
            parts.append(xh * lax.rsqrt(jnp.sum(xh * xh, axis=-1, keepdims=True) + NORM_EPS))
        return jnp.concatenate(parts, axis=1)

    qs[...] = l2n(conv_silu(q_ref, qh_ref, 0)) * (GDN_DK ** -0.5)
    ks[...] = l2n(conv_silu(k_ref, kh_ref, width))
    vs[...] = conv_silu(v_ref, vh_ref, 2 * width)

    incl = _tri(C, "incl")
    strict = _tri(C, "strict")
    tril_b = incl.astype(BF16)
    triu_b = _tri(C, "upper_incl").astype(BF16)
    inv_masks = _inverse_masks(C)

    def prep(cg, carry):
        loaded = []
        for g in range(group):
            c = cg * group + g
            r0 = pl.multiple_of(c * C, C)
            gcol = gc_ref[pl.ds(r0, C), :]
            grow = gr_ref[:, pl.ds(r0, C)]
            qkv = [(qs[pl.ds(r0, C), h * GDN_DK:(h + 1) * GDN_DK],
                    ks[pl.ds(r0, C), h * GDN_DK:(h + 1) * GDN_DK],
                    vs[pl.ds(r0, C), h * GDN_DK:(h + 1) * GDN_DK]) for h in range(GDN_HEADS)]
            loaded.append((c, gcol, grow, qkv))
        ch = []
        for c, gcol, grow, qkv in loaded:
            g_col = -jnp.exp(pc_ref[0:1, :]) * _softplus(gcol + pc_ref[1:2, :])
            gcum_col = _cumsum_rows(tril_b, g_col)
            beta_col = _sigmoid(gcol)
            g_row = -jnp.exp(pr_ref[:, 0:1]) * _softplus(grow + pr_ref[:, 1:2])
            gcum_row = _cumsum_lanes(g_row, triu_b)
            for h in range(GDN_HEADS):
                q, k, v = qkv[h]
                gc_c = gcum_col[:, h:h + 1]
                gc_r = gcum_row[h:h + 1, :]
                beta = beta_col[:, 4 + h:5 + h]
                decay = jnp.where(incl, jnp.exp(jnp.where(incl, gc_c - gc_r, 0.0)), 0.0)
                ch.append(dict(idx=c * GDN_HEADS + h, q=q, k=k, v=v, gc_c=gc_c, beta=beta,
                               decay=decay, kb=k * beta))
        for d in ch:
            d["a"] = jnp.where(strict, _mm_nt(d["kb"], d["k"]) * d["decay"], 0.0)
        eye, offs = inv_masks
        for d in ch:
            d["t"] = eye - jnp.where(offs[0], d["a"], 0.0)
        for off in offs[1:]:
            for d in ch:
                d["p"] = _mm(d["t"], jnp.where(off, d["a"], 0.0))
            for d in ch:
                d["t"] = d["t"] - _mm(d["p"], d["t"])
        for d in ch:
            d["egc"] = jnp.exp(d["gc_c"])
            d["u"] = _mm(d["t"], d["v"] * d["beta"])
        for d in ch:
            d["w"] = _mm(d["t"], d["kb"] * d["egc"]).astype(BF16)
        for d in ch:
            d["qk"] = (_mm_nt(d["q"], d["k"]) * d["decay"]).astype(BF16)
        for d in ch:
            idx = d["idx"]
            g_last = d["gc_c"][C - 1:C, :]
            u_s[idx] = d["u"]
            w_s[idx] = d["w"]
            qk_s[idx] = d["qk"]
            qd_s[idx] = (d["q"] * d["egc"]).astype(BF16)
            kdt_s[idx] = (d["k"] * jnp.exp(g_last - d["gc_c"])).T.astype(BF16)
            el_s[idx] = jnp.broadcast_to(jnp.exp(g_last), (8, LANES))
        return carry

    lax.fori_loop(0, n_chunks // group, prep, 0)

    def scan(c, carry):
        r0 = pl.multiple_of(c * C, C)
        heads = range(GDN_HEADS)
        dot = functools.partial(jnp.dot, preferred_element_type=F32)
        idx = [c * GDN_HEADS + h for h in heads]
        s_prev = [s_s[h] for h in heads]
        s_b = [s.astype(BF16) for s in s_prev]
        ws = [dot(w_s[idx[h]], s_b[h]) for h in heads]
        qs_ = [dot(qd_s[idx[h]], s_b[h]) for h in heads]
        v_b = [(u_s[idx[h]] - ws[h]).astype(BF16) for h in heads]
        upd = [dot(kdt_s[idx[h]], v_b[h]) for h in heads]
        o_all = [qs_[h] + dot(qk_s[idx[h]], v_b[h]) for h in heads]
        outs = []
        for h in heads:
            s_s[h] = s_prev[h] * el_s[idx[h]][0:1, :] + upd[h]
            o = o_all[h]
            ms = jnp.mean(o * o, axis=-1, keepdims=True)
            outs.append(o * lax.rsqrt(ms + NORM_EPS))
        y = jnp.concatenate(outs, axis=1) * og_ref[...]
        z = z_ref[pl.ds(r0, C), :]
        y = y * (z * _sigmoid(z))
        y_ref[pl.ds(r0, C), :] = y.astype(y_ref.dtype)
        return carry

    lax.fori_loop(0, n_chunks, scan, 0, unroll=True)


def _gdn(proj3, gates_r, conv_w, par_c, par_r, og, *, chunk, tb):
    b, t, _ = proj3.shape
    nc = tb // chunk
    nch = nc * GDN_HEADS
    width = GDN_HEADS * GDN_DK
    group = 4 if nc % 4 == 0 else (2 if nc % 2 == 0 else 1)
    assert chunk % LANES == 0
    kern = functools.partial(_gdn_kernel, chunk=chunk, n_chunks=nc, tb=tb, group=group)
    col = lambda c0: pl.BlockSpec((None, tb, width), lambda i, j, c0=c0: (i, j, c0 // width))
    halo = lambda c0: pl.BlockSpec(
        (None, 8, width), lambda i, j, c0=c0: (i, jnp.maximum(j * (tb // 8) - 1, 0), c0 // width))
    return pl.pallas_call(
        kern,
        grid=(b, t // tb),
        in_specs=[
            col(C_GQ), col(C_GK), col(C_GV), col(C_GZ),
            halo(C_GQ), halo(C_GK), halo(C_GV),
            pl.BlockSpec((None, tb, LANES), lambda i, j: (i, j, C_GATE // LANES)),
            pl.BlockSpec((16, tb), lambda i, j: (0, i * (t // tb) + j)),
            pl.BlockSpec((GDN_CONV, 3 * width), lambda i, j: (0, 0)),
            pl.BlockSpec((2, LANES), lambda i, j: (0, 0)),
            pl.BlockSpec((16, 2), lambda i, j: (0, 0)),
            pl.BlockSpec((1, width), lambda i, j: (0, 0)),
        ],
        out_specs=pl.BlockSpec((None, tb, width), lambda i, j: (i, j, 0)),
        out_shape=jax.ShapeDtypeStruct((b, t, width), BF16),
        scratch_shapes=[pltpu.VMEM((tb, width), F32), pltpu.VMEM((tb, width), F32),
                        pltpu.VMEM((tb, width), F32),
                        pltpu.VMEM((tb + 8, width), F32),
                        pltpu.VMEM((nch, chunk, GDN_DV), F32),
                        pltpu.VMEM((nch, chunk, GDN_DK), BF16),
                        pltpu.VMEM((nch, chunk, chunk), BF16),
                        pltpu.VMEM((nch, chunk, GDN_DK), BF16),
                        pltpu.VMEM((nch, GDN_DK, chunk), BF16),
                        pltpu.VMEM((nch, 8, LANES), F32),
                        pltpu.VMEM((GDN_HEADS, GDN_DK, GDN_DV), F32)],
        compiler_params=pltpu.CompilerParams(
            dimension_semantics=("arbitrary", "arbitrary"), vmem_limit_bytes=VMEM_LIMIT),
        name="gdn",
    )(proj3, proj3, proj3, proj3, proj3, proj3, proj3, proj3, gates_r, conv_w, par_c, par_r, og)


def _dsa_prep_kernel(cq_ref, ckv_ref, idx_ref, qn_ref, kn_ref, wuq_ref, wqi_ref, wuk_ref,
                     qlat_ref, qidx_ref, ckvn_ref, ckvt_ref, kidx_ref):
    cq = cq_ref[...]
    ms = jnp.mean(cq * cq, axis=-1, keepdims=True)
    cqn = ((cq * lax.rsqrt(ms + NORM_EPS)) * qn_ref[...]).astype(BF16)
    ckv = ckv_ref[...]
    ms = jnp.mean(ckv * ckv, axis=-1, keepdims=True)
    ckvn = (ckv * lax.rsqrt(ms + NORM_EPS)) * kn_ref[...]
    ckvn_ref[...] = ckvn.astype(BF16)
    ckvt_ref[...] = ckvn.T.astype(BF16)
    q = jnp.dot(cqn, wuq_ref[...], preferred_element_type=F32)
    qlat = jnp.dot(q.astype(BF16), wuk_ref[...], preferred_element_type=F32) * (DSA_HEAD_DIM ** -0.5)
    qlat_ref[...] = qlat.astype(BF16)
    qidx_ref[...] = jnp.dot(cqn, wqi_ref[...], preferred_element_type=F32).astype(BF16)
    kidx_ref[...] = idx_ref[:, 0:IDX_DIM].astype(BF16)


def _dsa_prep(proj2, qn, kn, wuq, wqi, wuk_bd):
    m = proj2.shape[0]
    tm = 512
    full = lambda shape: pl.BlockSpec(shape, lambda i: (0, 0))
    return pl.pallas_call(
        _dsa_prep_kernel,
        grid=(m // tm,),
        in_specs=[
            pl.BlockSpec((tm, DSA_Q_RANK), lambda i: (i, C_CQ // DSA_Q_RANK)),
            pl.BlockSpec((tm, DSA_KV_RANK), lambda i: (i, C_CKV // DSA_KV_RANK)),
            pl.BlockSpec((tm, LANES), lambda i: (i, C_IDX // LANES)),
            full((1, DSA_Q_RANK)), full((1, DSA_KV_RANK)),
            full((DSA_Q_RANK, DSA_HEADS * DSA_HEAD_DIM)),
            full((DSA_Q_RANK, IDX_HEADS * IDX_DIM)),
            full((DSA_HEADS * DSA_HEAD_DIM, DSA_HEADS * DSA_KV_RANK)),
        ],
        out_specs=[
            pl.BlockSpec((tm, DSA_HEADS * DSA_KV_RANK), lambda i: (i, 0)),
            pl.BlockSpec((tm, IDX_HEADS * IDX_DIM), lambda i: (i, 0)),
            pl.BlockSpec((tm, DSA_KV_RANK), lambda i: (i, 0)),
            pl.BlockSpec((DSA_KV_RANK, tm), lambda i: (0, i)),
            pl.BlockSpec((tm, IDX_DIM), lambda i: (i, 0)),
        ],
        out_shape=[
            jax.ShapeDtypeStruct((m, DSA_HEADS * DSA_KV_RANK), BF16),
            jax.ShapeDtypeStruct((m, IDX_HEADS * IDX_DIM), BF16),
            jax.ShapeDtypeStruct((m, DSA_KV_RANK), BF16),
            jax.ShapeDtypeStruct((DSA_KV_RANK, m), BF16),
            jax.ShapeDtypeStruct((m, IDX_DIM), BF16),
        ],
        compiler_params=pltpu.CompilerParams(
            dimension_semantics=("arbitrary",), vmem_limit_bytes=VMEM_LIMIT),
        name="dsa_prep",
    )(proj2, proj2, proj2, qn, kn, wuq, wqi, wuk_bd)


def _tree_sum(parts):
    while len(parts) > 1:
        parts = [parts[j] + parts[j + 1] for j in range(0, len(parts) - 1, 2)] + (
            [parts[-1]] if len(parts) % 2 else [])
    return parts[0]


def _dsa_kernel(qidx_ref, wrow_ref, qlat_ref, kidx_ref, ckv_ref, ckvt_ref, wuvt_ref, y_ref,
                key_s, hi_s, lo_s, cut_s, acc_s, *, qb, tk, n_sel):
    i = pl.program_id(1)
    n_tiles = (i * qb + qb + tk - 1) // tk
    key_pos = lax.broadcasted_iota(jnp.int32, (tk, qb), 0)
    q_pos = i * qb + lax.broadcasted_iota(jnp.int32, (tk, qb), 1)
    w_rows = wrow_ref[...] * (IDX_HEADS ** -0.5 * IDX_DIM ** -0.5)
    qidx = qidx_ref[...]
    q_heads = [qidx[:, h * IDX_DIM:(h + 1) * IDX_DIM] for h in range(IDX_HEADS)]
    i16_min = -(1 << 15)
    i16_max = (1 << 15) - 1
    sub16 = 16
    nt_dims = (((1,), (1,)), ((), ()))

    def score_body(kt, carry):
        k0 = pl.multiple_of(kt * tk, tk)
        kk = kidx_ref[pl.ds(k0, tk), :]
        sh = [lax.dot_general(kk, q_heads[h], nt_dims, preferred_element_type=F32)
              for h in range(IDX_HEADS)]
        acc = jnp.zeros((tk, qb), F32)
        for h in range(IDX_HEADS):
            acc = acc + jnp.maximum(sh[h], 0.0) * w_rows[h:h + 1, :]
        acc = acc + 0.0
        bits = pltpu.bitcast(acc, jnp.int32)
        keys = jnp.where(bits < 0, bits ^ jnp.int32(0x7FFFFFFF), bits)
        keys = jnp.where(k0 + key_pos <= q_pos, keys, jnp.int32(INT_MIN))
        key_s[kt] = keys
        hi_s[kt] = (keys >> 16).astype(jnp.int16)
        return carry

    lax.fori_loop(0, n_tiles, score_body, 0)

    def count_ge(src_s, cand):
        cand16 = jnp.broadcast_to(cand, (sub16, qb)).astype(jnp.int16)
        one, zero = jnp.int16(1), jnp.int16(0)

        def body(kt, acc):
            tile = src_s[kt]
            return acc + _tree_sum([jnp.where(tile[j * sub16:(j + 1) * sub16, :] >= cand16, one, zero)
                                    for j in range(tk // sub16)])

        cnt = lax.fori_loop(0, n_tiles, body, jnp.zeros((sub16, qb), jnp.int16))
        return jnp.sum(cnt.astype(jnp.int32), axis=0, keepdims=True)

    def kth_largest16(src_s, rank):
        thr = jnp.full((1, qb), i16_min, jnp.int32)
        n_ge = jnp.zeros((1, qb), jnp.int32) + n_tiles * tk
        n_gt = jnp.zeros((1, qb), jnp.int32)
        for bit in range(15, -1, -1):
            cand = thr + jnp.int32(1 << bit)
            cnt = count_ge(src_s, cand)
            ok = cnt >= rank
            thr = jnp.where(ok, cand, thr)
            n_ge = jnp.where(ok, cnt, n_ge)
            n_gt = jnp.where(ok, n_gt, cnt)
        return thr, n_ge, n_gt

    hi_thr, hi_ge, above = kth_largest16(hi_s, jnp.int32(n_sel))
    rank_lo = jnp.int32(n_sel) - above

    def lo_body(kt, carry):
        keys = key_s[kt]
        lo = (keys & jnp.int32(0xFFFF)) + jnp.int32(i16_min)
        lo_s[kt] = jnp.where((keys >> 16) == hi_thr, lo, jnp.int32(i16_min)).astype(jnp.int16)
        return carry

    lax.fori_loop(0, n_tiles, lo_body, 0)
    lo_thr, lo_ge, _ = kth_largest16(lo_s, rank_lo)
    thr_raw = hi_thr * jnp.int32(1 << 16) + (lo_thr - jnp.int32(i16_min))
    thr = jnp.maximum(thr_raw, jnp.int32(INT_MIN + 1))

    in_bucket = jnp.where(lo_thr > i16_min, lo_ge, hi_ge - above)
    excess = jnp.where(thr_raw > INT_MIN, above + in_bucket - jnp.int32(n_sel), 0)

    max_excess = jnp.max(excess)
    last = tk * key_s.shape[0] - 1
    assert last <= i16_max
    direct_limit = max(last, 1).bit_length()

    @pl.when(max_excess > 0)
    def _():
        def fill(kt, carry):
            rev = jnp.int32(last) - (kt * tk + key_pos)
            lo_s[kt] = jnp.where(key_s[kt] == thr_raw, rev, jnp.int32(-1)).astype(jnp.int16)
            return carry

        lax.fori_loop(0, n_tiles, fill, 0)

        @pl.when(max_excess <= direct_limit)
        def _():
            def next_tied(prev):
                prev16 = jnp.broadcast_to(prev, (sub16, qb)).astype(jnp.int16)
                big = jnp.int16(i16_max)

                def body(kt, acc):
                    tile = lo_s[kt]
                    parts = [jnp.where(tile[j * sub16:(j + 1) * sub16, :] > prev16,
                                       tile[j * sub16:(j + 1) * sub16, :], big)
                             for j in range(tk // sub16)]
                    while len(parts) > 1:
                        parts = [jnp.where(parts[j] < parts[j + 1], parts[j], parts[j + 1])
                                 for j in range(0, len(parts), 2)]
                    return jnp.where(acc < parts[0], acc, parts[0])

                low = lax.fori_loop(0, n_tiles, body, jnp.full((sub16, qb), i16_max, jnp.int16))
                return jnp.min(low.astype(jnp.int32), axis=0, keepdims=True)

            def step(j, carry):
                prev, cut = carry
                cur = next_tied(prev)
                return cur, jnp.where(j < excess, cur + 1, cut)

            _, cut = lax.fori_loop(0, max_excess, step,
                                   (jnp.full((1, qb), -1, jnp.int32), jnp.zeros((1, qb), jnp.int32)))
            cut_s[0:1, :] = cut

        @pl.when(max_excess > direct_limit)
        def _():
            keep = count_ge(lo_s, jnp.zeros((1, qb), jnp.int32)) - excess
            cut = jnp.zeros((1, qb), jnp.int32)
            for bit in range(direct_limit - 1, -1, -1):
                cand = cut + jnp.int32(1 << bit)
                cut = jnp.where(count_ge(lo_s, cand) >= keep, cand, cut)
            cut_s[0:1, :] = jnp.where(excess > 0, cut, 0)

        cut = cut_s[0:1, :]

        def demote(kt, carry):
            keys = key_s[kt]
            rev = jnp.int32(last) - (kt * tk + key_pos)
            demoted = jnp.where(rev < cut, jnp.int32(INT_MIN), keys)
            key_s[kt] = jnp.where(keys == thr_raw, demoted, keys)
            return carry

        lax.fori_loop(0, n_tiles, demote, 0)

    qlat = qlat_ref[...]
    heads = range(DSA_HEADS)
    ql = [qlat[:, h * DSA_KV_RANK:(h + 1) * DSA_KV_RANK] for h in heads]
    acc_s[...] = jnp.zeros_like(acc_s)

    def attn_body(kt, carry):
        m_run, l_run = carry
        k0 = pl.multiple_of(kt * tk, tk)
        ckv = ckv_ref[pl.ds(k0, tk), :]
        ckvt = ckvt_ref[:, pl.ds(k0, tk)]
        bias = jnp.where(key_s[kt] >= thr, 0.0, -jnp.inf)
        lg = [lax.dot_general(ckv, ql[h], nt_dims, preferred_element_type=F32) for h in heads]
        m_out, l_out, p_all, alphas = [], [], [], []
        for h in heads:
            lgm = lg[h] + bias
            m_new = jnp.maximum(m_run[h], jnp.max(lgm, axis=0, keepdims=True))
            p = jnp.exp(lgm - m_new)
            alpha = jnp.exp(m_run[h] - m_new)
            l_out.append(alpha * l_run[h] + jnp.sum(p, axis=0, keepdims=True))
            m_out.append(m_new)
            p_all.append(p.astype(BF16))
            alphas.append(alpha)
        pv = [jnp.dot(ckvt, p_all[h], preferred_element_type=F32) for h in heads]
        for h in heads:
            acc_s[h] = alphas[h] * acc_s[h] + pv[h]
        return tuple(m_out), tuple(l_out)

    init = (tuple(jnp.full((1, qb), -1e30, F32) for _ in heads),
            tuple(jnp.zeros((1, qb), F32) for _ in heads))
    _, l_fin = lax.fori_loop(0, n_tiles, attn_body, init)
    o_lat = [(acc_s[h] / l_fin[h]).astype(BF16) for h in heads]
    outs = [jnp.dot(wuvt_ref[h], o_lat[h], preferred_element_type=F32) for h in heads]
    y_ref[...] = jnp.concatenate(outs, axis=0).T.astype(y_ref.dtype)


def _dsa(qidx, widx_rows, qlat, kidx, ckvn, ckvt, wuvt, *, qb, tk):
    b, t, _ = qidx.shape
    n_sel = min(IDX_TOPK, t // 4)
    kern = functools.partial(_dsa_kernel, qb=qb, tk=tk, n_sel=n_sel)
    return pl.pallas_call(
        kern,
        grid=(b, t // qb),
        in_specs=[
            pl.BlockSpec((None, qb, IDX_HEADS * IDX_DIM), lambda i, j: (i, j, 0)),
            pl.BlockSpec((IDX_HEADS, qb), lambda i, j: (0, i * (t // qb) + j)),
            pl.BlockSpec((None, qb, DSA_HEADS * DSA_KV_RANK), lambda i, j: (i, j, 0)),
            pl.BlockSpec((None, t, IDX_DIM), lambda i, j: (i, 0, 0)),
            pl.BlockSpec((None, t, DSA_KV_RANK), lambda i, j: (i, 0, 0)),
            pl.BlockSpec((DSA_KV_RANK, t), lambda i, j: (0, i)),
            pl.BlockSpec((DSA_HEADS, DSA_HEAD_DIM, DSA_KV_RANK), lambda i, j: (0, 0, 0)),
        ],
        out_specs=pl.BlockSpec((None, qb, DSA_HEADS * DSA_HEAD_DIM), lambda i, j: (i, j, 0)),
        out_shape=jax.ShapeDtypeStruct((b, t, DSA_HEADS * DSA_HEAD_DIM), BF16),
        scratch_shapes=[pltpu.VMEM((t // tk, tk, qb), jnp.int32),
                        pltpu.VMEM((t // tk, tk, qb), jnp.int16),
                        pltpu.VMEM((t // tk, tk, qb), jnp.int16),
                        pltpu.VMEM((8, qb), jnp.int32),
                        pltpu.VMEM((DSA_HEADS, DSA_KV_RANK, qb), F32)],
        compiler_params=pltpu.CompilerParams(
            dimension_semantics=("arbitrary", "arbitrary"), vmem_limit_bytes=VMEM_LIMIT),
        name="dsa",
    )(qidx, widx_rows, qlat, kidx, ckvn, ckvt, wuvt)


def _pack_w_in(w):
    pts = [0]
    for wd in IN_WIDTHS:
        pts.append(pts[-1] + wd)
    seg = [w[..., pts[i]:pts[i + 1]] for i in range(len(IN_WIDTHS))]
    (gq, gk, gv, gz, ga, gb, cq, ckv, ik, iw, mq, mk, mv, mo, mi, mf) = seg
    zeros = lambda n: jnp.zeros(w.shape[:-1] + (n,), w.dtype)
    idx_blk = jnp.concatenate([ik, iw, zeros(LANES - IDX_DIM - IDX_HEADS)], axis=-1)
    gate_blk = jnp.concatenate([ga, gb, mi, mf, zeros(LANES - 16)], axis=-1)
    out = jnp.concatenate([gq, gk, gv, gz, cq, ckv, idx_blk, mq, mk, mv, mo, gate_blk], axis=-1)
    assert out.shape[-1] == NP
    return out.astype(BF16)


def _lane_rows(vals_at, depth):
    rows = jnp.zeros((depth, LANES), F32)
    for start, v in vals_at:
        rows = rows.at[:, start:start + v.shape[1]].set(v.astype(F32))
    return rows


def kernel(x, attn_norm, w_in, gdn_conv, gdn_a_log, gdn_dt_bias, gdn_out_norm, dsa_q_norm,
           dsa_kv_norm, dsa_w_uq, dsa_w_qidx, dsa_w_uk, dsa_w_uv, mlstm_i_bias, mlstm_f_bias,
           mlstm_out_norm, w_out, ffn_norm, w_gate, w_up, w_down, final_norm):
    b, t, d = x.shape
    depth = w_in.shape[0]
    gdn_chunk, mlstm_chunk = min(128, t), min(64, t)
    tb = min(512, t)
    qb = min(512, t)
    tk = min(512, t)

    w_in_p = _pack_w_in(w_in)
    gdn_par_c = jnp.stack([_lane_rows([(0, gdn_a_log)], depth), _lane_rows([(0, gdn_dt_bias)], depth)],
                          axis=1)
    gdn_par_r = jnp.swapaxes(gdn_par_c[:, :, :16], 1, 2)
    gdn_og = jnp.tile(gdn_out_norm, (1, GDN_HEADS))
    wuk_bd = jnp.zeros((depth, DSA_HEADS * DSA_HEAD_DIM, DSA_HEADS * DSA_KV_RANK), F32)
    for h in range(DSA_HEADS):
        wuk_bd = wuk_bd.at[:, h * DSA_HEAD_DIM:(h + 1) * DSA_HEAD_DIM,
                           h * DSA_KV_RANK:(h + 1) * DSA_KV_RANK].set(jnp.swapaxes(dsa_w_uk[:, h], 1, 2))
    wuk_bd = wuk_bd.astype(BF16)
    wuq_b, wqi_b = dsa_w_uq.astype(BF16), dsa_w_qidx.astype(BF16)
    wuvt_b = jnp.swapaxes(dsa_w_uv, 2, 3).astype(BF16)
    ml_bias_c = _lane_rows([(8, mlstm_i_bias), (12, mlstm_f_bias)], depth)
    ml_og = jnp.tile(mlstm_out_norm, (1, MLSTM_HEADS))
    w_out_b, w_gate_b, w_up_b, w_down_b = (a.astype(BF16) for a in (w_out, w_gate, w_up, w_down))

    x2 = x.reshape(b * t, d)
    for l in range(depth):
        proj2, gates_t, widx_rows = _inproj(x2, attn_norm[l].reshape(1, d), w_in_p[l])
        proj3 = proj2.reshape(b, t, NP)

        y_a = _gdn(proj3, gates_t, gdn_conv[l], gdn_par_c[l], gdn_par_r[l],
                   gdn_og[l].reshape(1, -1), chunk=gdn_chunk, tb=tb)

        qlat, qidx, ckvn, ckvt, kidx = _dsa_prep(
            proj2, dsa_q_norm[l].reshape(1, -1), dsa_kv_norm[l].reshape(1, -1),
            wuq_b[l], wqi_b[l], wuk_bd[l])
        r3 = lambda a: a.reshape(b, t, a.shape[-1])
        y_b = _dsa(r3(qidx), widx_rows, r3(qlat), r3(kidx), r3(ckvn), ckvt, wuvt_b[l], qb=qb, tk=tk)

        y_c = _mlstm(proj3, gates_t, ml_bias_c[l].reshape(1, LANES),
                     ml_bias_c[l, :16].reshape(16, 1), ml_og[l].reshape(1, -1),
                     chunk=mlstm_chunk, tb=tb)

        x2 = _out_ffn(x2, y_a.reshape(b * t, -1), y_b.reshape(b * t, -1), y_c.reshape(b * t, -1),
                      w_out_b[l], ffn_norm[l].reshape(1, d), w_gate_b[l], w_up_b[l], w_down_b[l],
                      final_norm.reshape(1, d), final_norm=(l == depth - 1))
    return x2.reshape(b, t, d)
```

```python
import functools
import math

import jax
import jax.numpy as jnp
from jax import lax
from jax.experimental import pallas as pl
from jax.experimental.pallas import tpu as pltpu

F32 = jnp.float32
BF16 = jnp.bfloat16
HIGHEST = lax.Precision.HIGHEST

D_MODEL = 1024
GDN_HEADS = 4
GDN_DK = 128
GDN_DV = 128
GDN_CONV = 4
DSA_HEADS = 4
DSA_HEAD_DIM = 64
DSA_Q_RANK = 256
DSA_KV_RANK = 128
IDX_HEADS = 8
IDX_DIM = 32
IDX_TOPK = 256
MLSTM_HEADS = 4
MLSTM_DQK = 64
MLSTM_DV = 64
FFN_HIDDEN = 2816
NORM_EPS = 1e-6

IN_WIDTHS = (512, 512, 512, 512, 4, 4, 256, 128, 32, 8, 256, 256, 256, 256, 4, 4)

C_GQ, C_GK, C_GV, C_GZ = 0, 512, 1024, 1536
C_CQ, C_CKV, C_IDX = 2048, 2304, 2432
C_MQ, C_MK, C_MV, C_MO = 2560, 2816, 3072, 3328
C_GATE = 3584
NP = 3712
LANES = 128

VMEM_LIMIT = 56 * 1024 * 1024

INT_MIN = -2 ** 31


def _mm(a, b):
    return jnp.dot(a.astype(BF16), b.astype(BF16), preferred_element_type=F32)


def _mm_nt(a, b):
    return lax.dot_general(a.astype(BF16), b.astype(BF16), (((1,), (1,)), ((), ())),
                           preferred_element_type=F32)


def _sigmoid(x):
    return 1.0 / (1.0 + jnp.exp(-x))


def _softplus(x):
    return jnp.maximum(x, 0.0) + jnp.log1p(jnp.exp(-jnp.abs(x)))


def _tri(n, kind):
    r = lax.broadcasted_iota(jnp.int32, (n, n), 0)
    c = lax.broadcasted_iota(jnp.int32, (n, n), 1)
    if kind == "incl":
        return r >= c
    if kind == "strict":
        return r > c
    if kind == "upper_incl":
        return r <= c
    raise ValueError(kind)


def _inproj_kernel(x_ref, g_ref, w_ref, o_ref, gt_ref, wr_ref):
    x = x_ref[...]
    ms = jnp.mean(x * x, axis=-1, keepdims=True)
    h = (x * lax.rsqrt(ms + NORM_EPS)) * g_ref[...]
    o = jnp.dot(h.astype(BF16), w_ref[...], preferred_element_type=F32)
    o_ref[...] = o
    gt_ref[...] = o[:, C_GATE:C_GATE + LANES].T[0:16, :]
    wr_ref[...] = o[:, C_IDX:C_IDX + LANES].T[IDX_DIM:IDX_DIM + IDX_HEADS, :]


def _inproj(x2, g, w):
    m = x2.shape[0]
    tm = 512
    return pl.pallas_call(
        _inproj_kernel,
        grid=(m // tm,),
        in_specs=[
            pl.BlockSpec((tm, D_MODEL), lambda i: (i, 0)),
            pl.BlockSpec((1, D_MODEL), lambda i: (0, 0)),
            pl.BlockSpec((D_MODEL, NP), lambda i: (0, 0)),
        ],
        out_specs=[pl.BlockSpec((tm, NP), lambda i: (i, 0)),
                   pl.BlockSpec((16, tm), lambda i: (0, i)),
                   pl.BlockSpec((IDX_HEADS, tm), lambda i: (0, i))],
        out_shape=[jax.ShapeDtypeStruct((m, NP), F32),
                   jax.ShapeDtypeStruct((16, m), F32),
                   jax.ShapeDtypeStruct((IDX_HEADS, m), F32)],
        compiler_params=pltpu.CompilerParams(
            dimension_semantics=("arbitrary",), vmem_limit_bytes=VMEM_LIMIT),
        name="inproj",
    )(x2, g, w)


def _ffn_kernel(x_ref, ya_ref, yb_ref, yc_ref, wo_ref, g_ref, wg_ref, wu_ref, wd_ref, fg_ref,
                o_ref, acc_ref, h_ref, *, n_hidden_steps, final_norm):
    j = pl.program_id(1)

    @pl.when(j == 0)
    def _():
        xn = x_ref[...]
        xn = xn + jnp.dot(ya_ref[...], wo_ref[0:512, :], preferred_element_type=F32)
        xn = xn + jnp.dot(yb_ref[...], wo_ref[512:768, :], preferred_element_type=F32)
        xn = xn + jnp.dot(yc_ref[...], wo_ref[768:1024, :], preferred_element_type=F32)
        acc_ref[...] = xn
        ms = jnp.mean(xn * xn, axis=-1, keepdims=True)
        h_ref[...] = ((xn * lax.rsqrt(ms + NORM_EPS)) * g_ref[...]).astype(BF16)

    h = h_ref[...]
    gate = jnp.dot(h, wg_ref[...], preferred_element_type=F32)
    up = jnp.dot(h, wu_ref[...], preferred_element_type=F32)
    act = (gate * _sigmoid(gate)) * up
    acc_ref[...] += jnp.dot(act.astype(BF16), wd_ref[...], preferred_element_type=F32)

    @pl.when(j == n_hidden_steps - 1)
    def _():
        y = acc_ref[...]
        if final_norm:
            ms = jnp.mean(y * y, axis=-1, keepdims=True)
            y = (y * lax.rsqrt(ms + NORM_EPS)) * fg_ref[...]
        o_ref[...] = y


def _out_ffn(x2, ya, yb, yc, wo, g, wg, wu, wd, fg, final_norm):
    m = x2.shape[0]
    tm = 512
    th = 2816
    nh = FFN_HIDDEN // th
    kern = functools.partial(_ffn_kernel, n_hidden_steps=nh, final_norm=final_norm)
    return pl.pallas_call(
        kern,
        grid=(m // tm, nh),
        in_specs=[
            pl.BlockSpec((tm, D_MODEL), lambda i, j: (i, 0)),
            pl.BlockSpec((tm, 512), lambda i, j: (i, 0)),
            pl.BlockSpec((tm, 256), lambda i, j: (i, 0)),
            pl.BlockSpec((tm, 256), lambda i, j: (i, 0)),
            pl.BlockSpec((D_MODEL, D_MODEL), lambda i, j: (0, 0), pipeline_mode=pl.Buffered(1)),
            pl.BlockSpec((1, D_MODEL), lambda i, j: (0, 0)),
            pl.BlockSpec((D_MODEL, th), lambda i, j: (0, j), pipeline_mode=pl.Buffered(1)),
            pl.BlockSpec((D_MODEL, th), lambda i, j: (0, j), pipeline_mode=pl.Buffered(1)),
            pl.BlockSpec((th, D_MODEL), lambda i, j: (j, 0), pipeline_mode=pl.Buffered(1)),
            pl.BlockSpec((1, D_MODEL), lambda i, j: (0, 0)),
        ],
        out_specs=pl.BlockSpec((tm, D_MODEL), lambda i, j: (i, 0)),
        out_shape=jax.ShapeDtypeStruct((m, D_MODEL), F32),
        scratch_shapes=[pltpu.VMEM((tm, D_MODEL), F32), pltpu.VMEM((tm, D_MODEL), BF16)],
        compiler_params=pltpu.CompilerParams(
            dimension_semantics=("arbitrary", "arbitrary"), vmem_limit_bytes=VMEM_LIMIT),
        name="out_ffn",
    )(x2, ya, yb, yc, wo, g, wg, wu, wd, fg)


def _cummax_rows(x, n):
    neg = -jnp.inf
    row = lax.broadcasted_iota(jnp.int32, x.shape, 0)
    s = 1
    while s < n:
        if s < 8:
            shifted = jnp.where(row >= s, pltpu.roll(x, s, axis=0), neg)
        else:
            shifted = jnp.concatenate([jnp.full((s, x.shape[1]), neg, x.dtype), x[:n - s]], axis=0)
        x = jnp.maximum(x, shifted)
        s *= 2
    return x


def _mlstm_kernel(q_ref, k_ref, v_ref, o_ref, gc_ref, gr_ref, bc_ref, br_ref, og_ref,
                  y_ref, sv_s, kv_s, b_s, dm_s, c_s, m_s, *, chunk, n_chunks, group):
    L = chunk
    t = pl.program_id(1)
    heads = range(MLSTM_HEADS)
    half = MLSTM_DQK

    @pl.when(t == 0)
    def _():
        c_s[...] = jnp.zeros_like(c_s)
        m_s[...] = jnp.zeros_like(m_s)

    incl = _tri(L, "incl")
    tril_b = incl.astype(BF16)
    triu_b = _tri(L, "upper_incl").astype(BF16)
    lane = lax.broadcasted_iota(jnp.int32, (L, LANES), 1)
    own = [(lane < half) if h % 2 == 0 else (lane >= half) for h in heads]
    row_c = lax.broadcasted_iota(jnp.int32, (LANES, LANES), 0)
    own_rows = [(row_c < half) if h % 2 == 0 else (row_c >= half) for h in heads]
    scale = MLSTM_DQK ** -0.5

    def pair(ref, r0, h):
        return ref[pl.ds(r0, L), (h // 2) * LANES:(h // 2 + 1) * LANES]

    def prep(cg, carry):
        ch = []
        span = group * L
        rows = gr_ref[:, pl.ds(pl.multiple_of(cg * span, span), span)] + br_ref[...]
        for g in range(group):
            c = cg * group + g
            r0 = pl.multiple_of(c * L, L)
            x_col = gc_ref[pl.ds(r0, L), :] + bc_ref[...]
            x_row = rows[:, g * L:(g + 1) * L]
            b_row = _cumsum_lanes(-_softplus(-x_row), triu_b)
            f_rep = jnp.concatenate(
                [jnp.broadcast_to(x_col[:, 12 + h:13 + h], (L, LANES)) for h in heads], axis=1)
            b_rep = _cumsum_rows(tril_b, -_softplus(-f_rep))
            for h in heads:
                bx = b_rep[:, h * LANES:(h + 1) * LANES]
                cx = jnp.broadcast_to(x_col[:, 8 + h:9 + h], (L, LANES)) - bx
                dmax = bx + _cummax_rows(cx, L)
                c_row = x_row[8 + h:9 + h, :] - b_row[12 + h:13 + h, :]
                d_log = jnp.where(incl, bx[:, :L] + c_row - dmax[:, :L], -jnp.inf)
                kw = jnp.exp(cx + (bx[L - 1:L, :] - dmax[L - 1:L, :]))
                kp = pair(k_ref, r0, h) * scale
                vp = pair(v_ref, r0, h)
                ch.append(dict(idx=c * MLSTM_HEADS + h, h=h,
                               q=jnp.where(own[h], pair(q_ref, r0, h), 0.0), k=kp,
                               v=jnp.where(own[h], vp, 1.0), dw=jnp.exp(d_log),
                               kxw_t=(kp * kw).T, bx=bx, dmax=dmax))
        for d in ch:
            d["s"] = _mm_nt(d["q"], d["k"]) * d["dw"]
        for d in ch:
            d["sv"] = _mm(d["s"], d["v"])
        for d in ch:
            d["kv"] = jnp.where(own_rows[d["h"]], _mm(d["kxw_t"], d["v"]), 0.0)
        for d in ch:
            sv_s[d["idx"]] = d["sv"]
            kv_s[d["idx"]] = d["kv"]
            b_s[d["idx"]] = d["bx"]
            dm_s[d["idx"]] = d["dmax"]
        return carry

    lax.fori_loop(0, n_chunks // group, prep, 0)

    def scan(c, carry):
        r0 = pl.multiple_of(c * L, L)
        c_prev = [c_s[h] for h in heads]
        qc = [_mm(pair(q_ref, r0, h), c_prev[h]) for h in heads]
        outs = []
        for h in heads:
            idx = c * MLSTM_HEADS + h
            bx = b_s[idx]
            dmax = dm_s[idx]
            m_prev = m_s[h:h + 1, :]
            a_log = bx + m_prev
            m_t = jnp.maximum(a_log, dmax)
            num = jnp.exp(a_log - m_t) * qc[h] + jnp.exp(dmax - m_t) * sv_s[idx]
            den = pltpu.roll(num, half, axis=1)
            hh = jnp.where(own[h], num / jnp.maximum(jnp.abs(den), jnp.exp(-m_t)), 0.0)
            m_new = m_t[L - 1:L, :]
            dec = jnp.exp(bx[L - 1:L, :] + m_prev - m_new)
            c_s[h] = dec * c_prev[h] + jnp.exp(dmax[L - 1:L, :] - m_new) * kv_s[idx]
            m_s[h:h + 1, :] = m_new
            ms = jnp.sum(hh * hh, axis=-1, keepdims=True) * (1.0 / MLSTM_DV)
            outs.append(hh * lax.rsqrt(ms + NORM_EPS))
        y = jnp.concatenate([outs[0] + outs[1], outs[2] + outs[3]], axis=1) * og_ref[...]
        y = y * _sigmoid(o_ref[pl.ds(r0, L), :])
        y_ref[pl.ds(r0, L), :] = y.astype(y_ref.dtype)
        return carry

    lax.fori_loop(0, n_chunks, scan, 0, unroll=2 if n_chunks % 2 == 0 else 1)


def _mlstm(proj3, gates_r, bias_c, bias_r, og, *, chunk, tb):
    b, t, _ = proj3.shape
    nc = tb // chunk
    nch = nc * MLSTM_HEADS
    wq = MLSTM_HEADS * MLSTM_DQK
    group = 8 if nc % 8 == 0 else (2 if nc % 2 == 0 else 1)
    assert (group * chunk) % LANES == 0
    kern = functools.partial(_mlstm_kernel, chunk=chunk, n_chunks=nc, group=group)
    col = lambda c0: pl.BlockSpec((None, tb, wq), lambda i, j, c0=c0: (i, j, c0 // wq))
    return pl.pallas_call(
        kern,
        grid=(b, t // tb),
        in_specs=[
            col(C_MQ), col(C_MK), col(C_MV), col(C_MO),
            pl.BlockSpec((None, tb, LANES), lambda i, j: (i, j, C_GATE // LANES)),
            pl.BlockSpec((16, tb), lambda i, j: (0, i * (t // tb) + j)),
            pl.BlockSpec((1, LANES), lambda i, j: (0, 0)),
            pl.BlockSpec((16, 1), lambda i, j: (0, 0)),
            pl.BlockSpec((1, wq), lambda i, j: (0, 0)),
        ],
        out_specs=pl.BlockSpec((None, tb, wq), lambda i, j: (i, j, 0)),
        out_shape=jax.ShapeDtypeStruct((b, t, wq), BF16),
        scratch_shapes=[pltpu.VMEM((nch, chunk, LANES), F32),
                        pltpu.VMEM((nch, LANES, LANES), F32),
                        pltpu.VMEM((nch, chunk, LANES), F32),
                        pltpu.VMEM((nch, chunk, LANES), F32),
                        pltpu.VMEM((MLSTM_HEADS, LANES, LANES), F32),
                        pltpu.VMEM((8, LANES), F32)],
        compiler_params=pltpu.CompilerParams(
            dimension_semantics=("arbitrary", "arbitrary"), vmem_limit_bytes=VMEM_LIMIT),
        name="mlstm",
    )(proj3, proj3, proj3, proj3, proj3, gates_r, bias_c, bias_r, og)


def _split3(x):
    x1 = x.astype(BF16)
    r1 = x - x1.astype(F32)
    x2 = r1.astype(BF16)
    x3 = (r1 - x2.astype(F32)).astype(BF16)
    return x1, x2, x3


def _cumsum_rows(tril_b, x):
    return sum(jnp.dot(tril_b, p, preferred_element_type=F32) for p in _split3(x))


def _cumsum_lanes(x, triu_b):
    return sum(jnp.dot(p, triu_b, preferred_element_type=F32) for p in _split3(x))


def _inverse_masks(n):
    r = lax.broadcasted_iota(jnp.int32, (n, n), 0)
    c = lax.broadcasted_iota(jnp.int32, (n, n), 1)
    offs = []
    lvl = 0
    while (1 << lvl) < n:
        rb = r >> lvl
        cb = c >> lvl
        offs.append(((rb & 1) == 1) & (cb == rb - 1))
        lvl += 1
    return (r == c).astype(F32), offs


def _gdn_kernel(q_ref, k_ref, v_ref, z_ref, qh_ref, kh_ref, vh_ref, gc_ref, gr_ref, cw_ref,
                pc_ref, pr_ref, og_ref, y_ref, qs, ks, vs, xp_s, u_s, w_s, qk_s, qd_s, kdt_s, el_s, s_s,
                *, chunk, n_chunks, tb, group):
    C = chunk
    t = pl.program_id(1)
    width = GDN_HEADS * GDN_DK

    @pl.when(t == 0)
    def _():
        s_s[...] = jnp.zeros_like(s_s)

    def conv_silu(x_ref, halo_ref, w0):
        xp_s[0:8, :] = jnp.where(t == 0, 0.0, halo_ref[...])
        xp_s[8:, :] = x_ref[...]
        y = xp_s[8:, :] * cw_ref[GDN_CONV - 1:GDN_CONV, w0:w0 + width]
        for j in range(GDN_CONV - 1):
            y = y + xp_s[pl.ds(8 - (GDN_CONV - 1) + j, tb), :] * cw_ref[j:j + 1, w0:w0 + width]
        return y * _sigmoid(y)

    def l2n(x):
        parts = []
        for h in range(GDN_HEADS):
            xh = x[:, h * GDN_DK:(h + 1) * GDN_DK]
            parts.append(xh * lax.rsqrt(jnp.sum(xh * xh, axis=-1, keepdims=True) + NORM_EPS))
        return jnp.concatenate(parts, axis=1)

    qs[...] = l2n(conv_silu(q_ref, qh_ref, 0)) * (GDN_DK ** -0.5)
    ks[...] = l2n(conv_silu(k_ref, kh_ref, width))
    vs[...] = conv_silu(v_ref, vh_ref, 2 * width)

    incl = _tri(C, "incl")
    strict = _tri(C, "strict")
    tril_b = incl.astype(BF16)
    triu_b = _tri(C, "upper_incl").astype(BF16)
    inv_masks = _inverse_masks(C)

    def prep(cg, carry):
        loaded = []
        for g in range(group):
            c = cg * group + g
            r0 = pl.multiple_of(c * C, C)
            gcol = gc_ref[pl.ds(r0, C), :]
            grow = gr_ref[:, pl.ds(r0, C)]
            qkv = [(qs[pl.ds(r0, C), h * GDN_DK:(h + 1) * GDN_DK],
                    ks[pl.ds(r0, C), h * GDN_DK:(h + 1) * GDN_DK],
                    vs[pl.ds(r0, C), h * GDN_DK:(h + 1) * GDN_DK]) for h in range(GDN_HEADS)]
            loaded.append((c, gcol, grow, qkv))
        ch = []
        for c, gcol, grow, qkv in loaded:
            g_col = -jnp.exp(pc_ref[0:1, :]) * _softplus(gcol + pc_ref[1:2, :])
            gcum_col = _cumsum_rows(tril_b, g_col)
            beta_col = _sigmoid(gcol)
            g_row = -jnp.exp(pr_ref[:, 0:1]) * _softplus(grow + pr_ref[:, 1:2])
            gcum_row = _cumsum_lanes(g_row, triu_b)
            for h in range(GDN_HEADS):
                q, k, v = qkv[h]
                gc_c = gcum_col[:, h:h + 1]
                gc_r = gcum_row[h:h + 1, :]
                beta = beta_col[:, 4 + h:5 + h]
                decay = jnp.where(incl, jnp.exp(jnp.where(incl, gc_c - gc_r, 0.0)), 0.0)
                ch.append(dict(idx=c * GDN_HEADS + h, q=q, k=k, v=v, gc_c=gc_c, beta=beta,
                               decay=decay, kb=k * beta))
        for d in ch:
            d["a"] = jnp.where(strict, _mm_nt(d["kb"], d["k"]) * d["decay"], 0.0)
        eye, offs = inv_masks
        for d in ch:
            d["t"] = eye - jnp.where(offs[0], d["a"], 0.0)
        for off in offs[1:]:
            for d in ch:
                d["p"] = _mm(d["t"], jnp.where(off, d["a"], 0.0))
            for d in ch:
                d["t"] = d["t"] - _mm(d["p"], d["t"])
        for d in ch:
            d["egc"] = jnp.exp(d["gc_c"])
            d["u"] = _mm(d["t"], d["v"] * d["beta"])
        for d in ch:
            d["w"] = _mm(d["t"], d["kb"] * d["egc"]).astype(BF16)
        for d in ch:
            d["qk"] = (_mm_nt(d["q"], d["k"]) * d["decay"]).astype(BF16)
        for d in ch:
            idx = d["idx"]
            g_last = d["gc_c"][C - 1:C, :]
            u_s[idx] = d["u"]
            w_s[idx] = d["w"]
            qk_s[idx] = d["qk"]
            qd_s[idx] = (d["q"] * d["egc"]).astype(BF16)
            kdt_s[idx] = (d["k"] * jnp.exp(g_last - d["gc_c"])).T.astype(BF16)
            el_s[idx] = jnp.broadcast_to(jnp.exp(g_last), (8, LANES))
        return carry

    lax.fori_loop(0, n_chunks // group, prep, 0)

    def scan(c, carry):
        r0 = pl.multiple_of(c * C, C)
        heads = range(GDN_HEADS)
        dot = functools.partial(jnp.dot, preferred_element_type=F32)
        idx = [c * GDN_HEADS + h for h in heads]
        s_prev = [s_s[h] for h in heads]
        s_b = [s.astype(BF16) for s in s_prev]
        ws = [dot(w_s[idx[h]], s_b[h]) for h in heads]
        qs_ = [dot(qd_s[idx[h]], s_b[h]) for h in heads]
        v_b = [(u_s[idx[h]] - ws[h]).astype(BF16) for h in heads]
        upd = [dot(kdt_s[idx[h]], v_b[h]) for h in heads]
        o_all = [qs_[h] + dot(qk_s[idx[h]], v_b[h]) for h in heads]
        outs = []
        for h in heads:
            s_s[h] = s_prev[h] * el_s[idx[h]][0:1, :] + upd[h]
            o = o_all[h]
            ms = jnp.mean(o * o, axis=-1, keepdims=True)
            outs.append(o * lax.rsqrt(ms + NORM_EPS))
        y = jnp.concatenate(outs, axis=1) * og_ref[...]
        z = z_ref[pl.ds(r0, C), :]
        y = y * (z * _sigmoid(z))
        y_ref[pl.ds(r0, C), :] = y.astype(y_ref.dtype)
        return carry

    lax.fori_loop(0, n_chunks, scan, 0, unroll=True)


def _gdn(proj3, gates_r, conv_w, par_c, par_r, og, *, chunk, tb):
    b, t, _ = proj3.shape
    nc = tb // chunk
    nch = nc * GDN_HEADS
    width = GDN_HEADS * GDN_DK
    group = 4 if nc % 4 == 0 else (2 if nc % 2 == 0 else 1)
    assert chunk % LANES == 0
    kern = functools.partial(_gdn_kernel, chunk=chunk, n_chunks=nc, tb=tb, group=group)
    col = lambda c0: pl.BlockSpec((None, tb, width), lambda i, j, c0=c0: (i, j, c0 // width))
    halo = lambda c0: pl.BlockSpec(
        (None, 8, width), lambda i, j, c0=c0: (i, jnp.maximum(j * (tb // 8) - 1, 0), c0 // width))
    return pl.pallas_call(
        kern,
        grid=(b, t // tb),
        in_specs=[
            col(C_GQ), col(C_GK), col(C_GV), col(C_GZ),
            halo(C_GQ), halo(C_GK), halo(C_GV),
            pl.BlockSpec((None, tb, LANES), lambda i, j: (i, j, C_GATE // LANES)),
            pl.BlockSpec((16, tb), lambda i, j: (0, i * (t // tb) + j)),
            pl.BlockSpec((GDN_CONV, 3 * width), lambda i, j: (0, 0)),
            pl.BlockSpec((2, LANES), lambda i, j: (0, 0)),
            pl.BlockSpec((16, 2), lambda i, j: (0, 0)),
            pl.BlockSpec((1, width), lambda i, j: (0, 0)),
        ],
        out_specs=pl.BlockSpec((None, tb, width), lambda i, j: (i, j, 0)),
        out_shape=jax.ShapeDtypeStruct((b, t, width), BF16),
        scratch_shapes=[pltpu.VMEM((tb, width), F32), pltpu.VMEM((tb, width), F32),
                        pltpu.VMEM((tb, width), F32),
                        pltpu.VMEM((tb + 8, width), F32),
                        pltpu.VMEM((nch, chunk, GDN_DV), F32),
                        pltpu.VMEM((nch, chunk, GDN_DK), BF16),
                        pltpu.VMEM((nch, chunk, chunk), BF16),
                        pltpu.VMEM((nch, chunk, GDN_DK), BF16),
                        pltpu.VMEM((nch, GDN_DK, chunk), BF16),
                        pltpu.VMEM((nch, 8, LANES), F32),
                        pltpu.VMEM((GDN_HEADS, GDN_DK, GDN_DV), F32)],
        compiler_params=pltpu.CompilerParams(
            dimension_semantics=("arbitrary", "arbitrary"), vmem_limit_bytes=VMEM_LIMIT),
        name="gdn",
    )(proj3, proj3, proj3, proj3, proj3, proj3, proj3, proj3, gates_r, conv_w, par_c, par_r, og)


def _dsa_prep_kernel(cq_ref, ckv_ref, idx_ref, qn_ref, kn_ref, wuq_ref, wqi_ref, wuk_ref,
                     qlat_ref, qidx_ref, ckvn_ref, ckvt_ref, kidx_ref):
    cq = cq_ref[...]
    ms = jnp.mean(cq * cq, axis=-1, keepdims=True)
    cqn = ((cq * lax.rsqrt(ms + NORM_EPS)) * qn_ref[...]).astype(BF16)
    ckv = ckv_ref[...]
    ms = jnp.mean(ckv * ckv, axis=-1, keepdims=True)
    ckvn = (ckv * lax.rsqrt(ms + NORM_EPS)) * kn_ref[...]
    ckvn_ref[...] = ckvn.astype(BF16)
    ckvt_ref[...] = ckvn.T.astype(BF16)
    q = jnp.dot(cqn, wuq_ref[...], preferred_element_type=F32)
    qlat = jnp.dot(q.astype(BF16), wuk_ref[...], preferred_element_type=F32) * (DSA_HEAD_DIM ** -0.5)
    qlat_ref[...] = qlat.astype(BF16)
    qidx_ref[...] = jnp.dot(cqn, wqi_ref[...], preferred_element_type=F32).astype(BF16)
    kidx_ref[...] = idx_ref[:, 0:IDX_DIM].astype(BF16)


def _dsa_prep(proj2, qn, kn, wuq, wqi, wuk_bd):
    m = proj2.shape[0]
    tm = min(1024, m)
    full = lambda shape: pl.BlockSpec(shape, lambda i: (0, 0))
    return pl.pallas_call(
        _dsa_prep_kernel,
        grid=(m // tm,),
        in_specs=[
            pl.BlockSpec((tm, DSA_Q_RANK), lambda i: (i, C_CQ // DSA_Q_RANK)),
            pl.BlockSpec((tm, DSA_KV_RANK), lambda i: (i, C_CKV // DSA_KV_RANK)),
            pl.BlockSpec((tm, LANES), lambda i: (i, C_IDX // LANES)),
            full((1, DSA_Q_RANK)), full((1, DSA_KV_RANK)),
            full((DSA_Q_RANK, DSA_HEADS * DSA_HEAD_DIM)),
            full((DSA_Q_RANK, IDX_HEADS * IDX_DIM)),
            full((DSA_HEADS * DSA_HEAD_DIM, DSA_HEADS * DSA_KV_RANK)),
        ],
        out_specs=[
            pl.BlockSpec((tm, DSA_HEADS * DSA_KV_RANK), lambda i: (i, 0)),
            pl.BlockSpec((tm, IDX_HEADS * IDX_DIM), lambda i: (i, 0)),
            pl.BlockSpec((tm, DSA_KV_RANK), lambda i: (i, 0)),
            pl.BlockSpec((DSA_KV_RANK, tm), lambda i: (0, i)),
            pl.BlockSpec((tm, IDX_DIM), lambda i: (i, 0)),
        ],
        out_shape=[
            jax.ShapeDtypeStruct((m, DSA_HEADS * DSA_KV_RANK), BF16),
            jax.ShapeDtypeStruct((m, IDX_HEADS * IDX_DIM), BF16),
            jax.ShapeDtypeStruct((m, DSA_KV_RANK), BF16),
            jax.ShapeDtypeStruct((DSA_KV_RANK, m), BF16),
            jax.ShapeDtypeStruct((m, IDX_DIM), BF16),
        ],
        compiler_params=pltpu.CompilerParams(
            dimension_semantics=("arbitrary",), vmem_limit_bytes=VMEM_LIMIT),
        name="dsa_prep",
    )(proj2, proj2, proj2, qn, kn, wuq, wqi, wuk_bd)


def _tree_sum(parts):
    while len(parts) > 1:
        parts = [parts[j] + parts[j + 1] for j in range(0, len(parts) - 1, 2)] + (
            [parts[-1]] if len(parts) % 2 else [])
    return parts[0]


def _dsa_kernel(qidx_ref, wrow_ref, qlat_ref, kidx_ref, ckv_ref, ckvt_ref, wuvt_ref, y_ref,
                key_s, hi_s, lo_s, cut_s, acc_s, *, qb, tk, n_sel):
    i = pl.program_id(1)
    n_tiles = (i * qb + qb + tk - 1) // tk
    key_pos = lax.broadcasted_iota(jnp.int32, (tk, qb), 0)
    q_pos = i * qb + lax.broadcasted_iota(jnp.int32, (tk, qb), 1)
    w_rows = wrow_ref[...] * (IDX_HEADS ** -0.5 * IDX_DIM ** -0.5)
    qidx = qidx_ref[...]
    q_heads = [qidx[:, h * IDX_DIM:(h + 1) * IDX_DIM] for h in range(IDX_HEADS)]
    i16_min = -(1 << 15)
    i16_max = (1 << 15) - 1
    sub16 = 16
    nt_dims = (((1,), (1,)), ((), ()))

    def score_body(kt, carry):
        k0 = pl.multiple_of(kt * tk, tk)
        kk = kidx_ref[pl.ds(k0, tk), :]
        sh = [lax.dot_general(kk, q_heads[h], nt_dims, preferred_element_type=F32)
              for h in range(IDX_HEADS)]
        acc = jnp.zeros((tk, qb), F32)
        for h in range(IDX_HEADS):
            acc = acc + jnp.maximum(sh[h], 0.0) * w_rows[h:h + 1, :]
        acc = acc + 0.0
        bits = pltpu.bitcast(acc, jnp.int32)
        keys = jnp.where(bits < 0, bits ^ jnp.int32(0x7FFFFFFF), bits)
        keys = jnp.where(k0 + key_pos <= q_pos, keys, jnp.int32(INT_MIN))
        key_s[kt] = keys
        hi_s[kt] = (keys >> 16).astype(jnp.int16)
        return carry

    lax.fori_loop(0, n_tiles, score_body, 0)

    def count_ge(src_s, cand):
        cand16 = jnp.broadcast_to(cand, (sub16, qb)).astype(jnp.int16)
        one, zero = jnp.int16(1), jnp.int16(0)

        def body(kt, acc):
            tile = src_s[kt]
            return acc + _tree_sum([jnp.where(tile[j * sub16:(j + 1) * sub16, :] >= cand16, one, zero)
                                    for j in range(tk // sub16)])

        cnt = lax.fori_loop(0, n_tiles, body, jnp.zeros((sub16, qb), jnp.int16))
        return jnp.sum(cnt.astype(jnp.int32), axis=0, keepdims=True)

    def kth_largest16(src_s, rank):
        thr = jnp.full((1, qb), i16_min, jnp.int32)
        n_ge = jnp.zeros((1, qb), jnp.int32) + n_tiles * tk
        n_gt = jnp.zeros((1, qb), jnp.int32)
        for bit in range(15, -1, -1):
            cand = thr + jnp.int32(1 << bit)
            cnt = count_ge(src_s, cand)
            ok = cnt >= rank
            thr = jnp.where(ok, cand, thr)
            n_ge = jnp.where(ok, cnt, n_ge)
            n_gt = jnp.where(ok, n_gt, cnt)
        return thr, n_ge, n_gt

    hi_thr, hi_ge, above = kth_largest16(hi_s, jnp.int32(n_sel))
    rank_lo = jnp.int32(n_sel) - above

    def lo_body(kt, carry):
        keys = key_s[kt]
        lo = (keys & jnp.int32(0xFFFF)) + jnp.int32(i16_min)
        lo_s[kt] = jnp.where((keys >> 16) == hi_thr, lo, jnp.int32(i16_min)).astype(jnp.int16)
        return carry

    lax.fori_loop(0, n_tiles, lo_body, 0)
    lo_thr, lo_ge, _ = kth_largest16(lo_s, rank_lo)
    thr_raw = hi_thr * jnp.int32(1 << 16) + (lo_thr - jnp.int32(i16_min))
    thr = jnp.maximum(thr_raw, jnp.int32(INT_MIN + 1))

    in_bucket = jnp.where(lo_thr > i16_min, lo_ge, hi_ge - above)
    excess = jnp.where(thr_raw > INT_MIN, above + in_bucket - jnp.int32(n_sel), 0)

    max_excess = jnp.max(excess)
    last = tk * key_s.shape[0] - 1
    assert last <= i16_max
    direct_limit = max(last, 1).bit_length()

    @pl.when(max_excess > 0)
    def _():
        def fill(kt, carry):
            rev = jnp.int32(last) - (kt * tk + key_pos)
            lo_s[kt] = jnp.where(key_s[kt] == thr_raw, rev, jnp.int32(-1)).astype(jnp.int16)
            return carry

        lax.fori_loop(0, n_tiles, fill, 0)

        @pl.when(max_excess <= direct_limit)
        def _():
            def next_tied(prev):
                prev16 = jnp.broadcast_to(prev, (sub16, qb)).astype(jnp.int16)
                big = jnp.int16(i16_max)

                def body(kt, acc):
                    tile = lo_s[kt]
                    parts = [jnp.where(tile[j * sub16:(j + 1) * sub16, :] > prev16,
                                       tile[j * sub16:(j + 1) * sub16, :], big)
                             for j in range(tk // sub16)]
                    while len(parts) > 1:
                        parts = [jnp.where(parts[j] < parts[j + 1], parts[j], parts[j + 1])
                                 for j in range(0, len(parts), 2)]
                    return jnp.where(acc < parts[0], acc, parts[0])

                low = lax.fori_loop(0, n_tiles, body, jnp.full((sub16, qb), i16_max, jnp.int16))
                return jnp.min(low.astype(jnp.int32), axis=0, keepdims=True)

            def step(j, carry):
                prev, cut = carry
                cur = next_tied(prev)
                return cur, jnp.where(j < excess, cur + 1, cut)

            _, cut = lax.fori_loop(0, max_excess, step,
                                   (jnp.full((1, qb), -1, jnp.int32), jnp.zeros((1, qb), jnp.int32)))
            cut_s[0:1, :] = cut

        @pl.when(max_excess > direct_limit)
        def _():
            keep = count_ge(lo_s, jnp.zeros((1, qb), jnp.int32)) - excess
            cut = jnp.zeros((1, qb), jnp.int32)
            for bit in range(direct_limit - 1, -1, -1):
                cand = cut + jnp.int32(1 << bit)
                cut = jnp.where(count_ge(lo_s, cand) >= keep, cand, cut)
            cut_s[0:1, :] = jnp.where(excess > 0, cut, 0)

        cut = cut_s[0:1, :]

        def demote(kt, carry):
            keys = key_s[kt]
            rev = jnp.int32(last) - (kt * tk + key_pos)
            demoted = jnp.where(rev < cut, jnp.int32(INT_MIN), keys)
            key_s[kt] = jnp.where(keys == thr_raw, demoted, keys)
            return carry

        lax.fori_loop(0, n_tiles, demote, 0)

    qlat = qlat_ref[...]
    heads = range(DSA_HEADS)
    ql = [qlat[:, h * DSA_KV_RANK:(h + 1) * DSA_KV_RANK] for h in heads]
    acc_s[...] = jnp.zeros_like(acc_s)

    def attn_body(kt, carry):
        m_run, l_run = carry
        k0 = pl.multiple_of(kt * tk, tk)
        ckv = ckv_ref[pl.ds(k0, tk), :]
        ckvt = ckvt_ref[:, pl.ds(k0, tk)]
        bias = jnp.where(key_s[kt] >= thr, 0.0, -jnp.inf)
        lg = [lax.dot_general(ckv, ql[h], nt_dims, preferred_element_type=F32) for h in heads]
        m_out, l_out, p_all, alphas = [], [], [], []
        for h in heads:
            lgm = lg[h] + bias
            m_new = jnp.maximum(m_run[h], jnp.max(lgm, axis=0, keepdims=True))
            p = jnp.exp(lgm - m_new)
            alpha = jnp.exp(m_run[h] - m_new)
            l_out.append(alpha * l_run[h] + jnp.sum(p, axis=0, keepdims=True))
            m_out.append(m_new)
            p_all.append(p.astype(BF16))
            alphas.append(alpha)
        pv = [jnp.dot(ckvt, p_all[h], preferred_element_type=F32) for h in heads]
        for h in heads:
            acc_s[h] = alphas[h] * acc_s[h] + pv[h]
        return tuple(m_out), tuple(l_out)

    init = (tuple(jnp.full((1, qb), -1e30, F32) for _ in heads),
            tuple(jnp.zeros((1, qb), F32) for _ in heads))
    _, l_fin = lax.fori_loop(0, n_tiles, attn_body, init)
    o_lat = [(acc_s[h] / l_fin[h]).astype(BF16) for h in heads]
    outs = [jnp.dot(wuvt_ref[h], o_lat[h], preferred_element_type=F32) for h in heads]
    y_ref[...] = jnp.concatenate(outs, axis=0).T.astype(y_ref.dtype)


def _dsa(qidx, widx_rows, qlat, kidx, ckvn, ckvt, wuvt, *, qb, tk):
    b, t, _ = qidx.shape
    n_sel = min(IDX_TOPK, t // 4)
    kern = functools.partial(_dsa_kernel, qb=qb, tk=tk, n_sel=n_sel)
    return pl.pallas_call(
        kern,
        grid=(b, t // qb),
        in_specs=[
            pl.BlockSpec((None, qb, IDX_HEADS * IDX_DIM), lambda i, j: (i, j, 0)),
            pl.BlockSpec((IDX_HEADS, qb), lambda i, j: (0, i * (t // qb) + j)),
            pl.BlockSpec((None, qb, DSA_HEADS * DSA_KV_RANK), lambda i, j: (i, j, 0)),
            pl.BlockSpec((None, t, IDX_DIM), lambda i, j: (i, 0, 0)),
            pl.BlockSpec((None, t, DSA_KV_RANK), lambda i, j: (i, 0, 0)),
            pl.BlockSpec((DSA_KV_RANK, t), lambda i, j: (0, i)),
            pl.BlockSpec((DSA_HEADS, DSA_HEAD_DIM, DSA_KV_RANK), lambda i, j: (0, 0, 0)),
        ],
        out_specs=pl.BlockSpec((None, qb, DSA_HEADS * DSA_HEAD_DIM), lambda i, j: (i, j, 0)),
        out_shape=jax.ShapeDtypeStruct((b, t, DSA_HEADS * DSA_HEAD_DIM), BF16),
        scratch_shapes=[pltpu.VMEM((t // tk, tk, qb), jnp.int32),
                        pltpu.VMEM((t // tk, tk, qb), jnp.int16),
                        pltpu.VMEM((t // tk, tk, qb), jnp.int16),
                        pltpu.VMEM((8, qb), jnp.int32),
                        pltpu.VMEM((DSA_HEADS, DSA_KV_RANK, qb), F32)],
        compiler_params=pltpu.CompilerParams(
            dimension_semantics=("arbitrary", "arbitrary"), vmem_limit_bytes=VMEM_LIMIT),
        name="dsa",
    )(qidx, widx_rows, qlat, kidx, ckvn, ckvt, wuvt)


def _pack_w_in(w):
    pts = [0]
    for wd in IN_WIDTHS:
        pts.append(pts[-1] + wd)
    seg = [w[..., pts[i]:pts[i + 1]] for i in range(len(IN_WIDTHS))]
    (gq, gk, gv, gz, ga, gb, cq, ckv, ik, iw, mq, mk, mv, mo, mi, mf) = seg
    zeros = lambda n: jnp.zeros(w.shape[:-1] + (n,), w.dtype)
    idx_blk = jnp.concatenate([ik, iw, zeros(LANES - IDX_DIM - IDX_HEADS)], axis=-1)
    gate_blk = jnp.concatenate([ga, gb, mi, mf, zeros(LANES - 16)], axis=-1)
    out = jnp.concatenate([gq, gk, gv, gz, cq, ckv, idx_blk, mq, mk, mv, mo, gate_blk], axis=-1)
    assert out.shape[-1] == NP
    return out.astype(BF16)


def _lane_rows(vals_at, depth):
    rows = jnp.zeros((depth, LANES), F32)
    for start, v in vals_at:
        rows = rows.at[:, start:start + v.shape[1]].set(v.astype(F32))
    return rows


def kernel(x, attn_norm, w_in, gdn_conv, gdn_a_log, gdn_dt_bias, gdn_out_norm, dsa_q_norm,
           dsa_kv_norm, dsa_w_uq, dsa_w_qidx, dsa_w_uk, dsa_w_uv, mlstm_i_bias, mlstm_f_bias,
           mlstm_out_norm, w_out, ffn_norm, w_gate, w_up, w_down, final_norm):
    b, t, d = x.shape
    depth = w_in.shape[0]
    gdn_chunk, mlstm_chunk = min(128, t), min(64, t)
    tb = min(512, t)
    qb = min(512, t)
    tk = min(512, t)

    w_in_p = _pack_w_in(w_in)
    gdn_par_c = jnp.stack([_lane_rows([(0, gdn_a_log)], depth), _lane_rows([(0, gdn_dt_bias)], depth)],
                          axis=1)
    gdn_par_r = jnp.swapaxes(gdn_par_c[:, :, :16], 1, 2)
    gdn_og = jnp.tile(gdn_out_norm, (1, GDN_HEADS))
    wuk_bd = jnp.zeros((depth, DSA_HEADS * DSA_HEAD_DIM, DSA_HEADS * DSA_KV_RANK), F32)
    for h in range(DSA_HEADS):
        wuk_bd = wuk_bd.at[:, h * DSA_HEAD_DIM:(h + 1) * DSA_HEAD_DIM,
                           h * DSA_KV_RANK:(h + 1) * DSA_KV_RANK].set(jnp.swapaxes(dsa_w_uk[:, h], 1, 2))
    wuk_bd = wuk_bd.astype(BF16)
    wuq_b, wqi_b = dsa_w_uq.astype(BF16), dsa_w_qidx.astype(BF16)
    wuvt_b = jnp.swapaxes(dsa_w_uv, 2, 3).astype(BF16)
    ml_bias_c = _lane_rows([(8, mlstm_i_bias), (12, mlstm_f_bias)], depth)
    ml_og = jnp.tile(mlstm_out_norm, (1, MLSTM_HEADS))
    w_out_b, w_gate_b, w_up_b, w_down_b = (a.astype(BF16) for a in (w_out, w_gate, w_up, w_down))

    x2 = x.reshape(b * t, d)
    for l in range(depth):
        proj2, gates_t, widx_rows = _inproj(x2, attn_norm[l].reshape(1, d), w_in_p[l])
        proj3 = proj2.reshape(b, t, NP)

        y_a = _gdn(proj3, gates_t, gdn_conv[l], gdn_par_c[l], gdn_par_r[l],
                   gdn_og[l].reshape(1, -1), chunk=gdn_chunk, tb=tb)

        qlat, qidx, ckvn, ckvt, kidx = _dsa_prep(
            proj2, dsa_q_norm[l].reshape(1, -1), dsa_kv_norm[l].reshape(1, -1),
            wuq_b[l], wqi_b[l], wuk_bd[l])
        r3 = lambda a: a.reshape(b, t, a.shape[-1])
        y_b = _dsa(r3(qidx), widx_rows, r3(qlat), r3(kidx), r3(ckvn), ckvt, wuvt_b[l], qb=qb, tk=tk)

        y_c = _mlstm(proj3, gates_t, ml_bias_c[l].reshape(1, LANES),
                     ml_bias_c[l, :16].reshape(16, 1), ml_og[l].reshape(1, -1),
                     chunk=mlstm_chunk, tb=tb)

        x2 = _out_ffn(x2, y_a.reshape(b * t, -1), y_b.reshape(b * t, -1), y_c.reshape(b * t, -1),
                      w_out_b[l], ffn_norm[l].reshape(1, d), w_gate_b[l], w_up_b[l], w_down_b[l],
                      final_norm.reshape(1, d), final_norm=(l == depth - 1))
    return x2.reshape(b, t, d)
```

```python
import functools
import math

import jax
import jax.numpy as jnp
from jax import lax
from jax.experimental import pallas as pl
from jax.experimental.pallas import tpu as pltpu

F32 = jnp.float32
BF16 = jnp.bfloat16
HIGHEST = lax.Precision.HIGHEST

D_MODEL = 1024
GDN_HEADS = 4
GDN_DK = 128
GDN_DV = 128
GDN_CONV = 4
DSA_HEADS = 4
DSA_HEAD_DIM = 64
DSA_Q_RANK = 256
DSA_KV_RANK = 128
IDX_HEADS = 8
IDX_DIM = 32
IDX_TOPK = 256
MLSTM_HEADS = 4
MLSTM_DQK = 64
MLSTM_DV = 64
FFN_HIDDEN = 2816
NORM_EPS = 1e-6

IN_WIDTHS = (512, 512, 512, 512, 4, 4, 256, 128, 32, 8, 256, 256, 256, 256, 4, 4)

C_GQ, C_GK, C_GV, C_GZ = 0, 512, 1024, 1536
C_CQ, C_CKV, C_IDX = 2048, 2304, 2432
C_MQ, C_MK, C_MV, C_MO = 2560, 2816, 3072, 3328
C_GATE = 3584
NP = 3712
LANES = 128

VMEM_LIMIT = 56 * 1024 * 1024

INT_MIN = -2 ** 31


def _mm(a, b):
    return jnp.dot(a.astype(BF16), b.astype(BF16), preferred_element_type=F32)


def _mm_nt(a, b):
    return lax.dot_general(a.astype(BF16), b.astype(BF16), (((1,), (1,)), ((), ())),
                           preferred_element_type=F32)


def _sigmoid(x):
    return 1.0 / (1.0 + jnp.exp(-x))


def _softplus(x):
    return jnp.maximum(x, 0.0) + jnp.log1p(jnp.exp(-jnp.abs(x)))


def _tri(n, kind):
    r = lax.broadcasted_iota(jnp.int32, (n, n), 0)
    c = lax.broadcasted_iota(jnp.int32, (n, n), 1)
    if kind == "incl":
        return r >= c
    if kind == "strict":
        return r > c
    if kind == "upper_incl":
        return r <= c
    raise ValueError(kind)


def _inproj_kernel(x_ref, g_ref, w_ref, o_ref, gt_ref, wr_ref):
    x = x_ref[...]
    ms = jnp.mean(x * x, axis=-1, keepdims=True)
    h = (x * lax.rsqrt(ms + NORM_EPS)) * g_ref[...]
    o = jnp.dot(h.astype(BF16), w_ref[...], preferred_element_type=F32)
    o_ref[...] = o
    gt_ref[...] = o[:, C_GATE:C_GATE + LANES].T[0:16, :]
    wr_ref[...] = o[:, C_IDX:C_IDX + LANES].T[IDX_DIM:IDX_DIM + IDX_HEADS, :]


def _inproj(x2, g, w):
    m = x2.shape[0]
    tm = 512
    return pl.pallas_call(
        _inproj_kernel,
        grid=(m // tm,),
        in_specs=[
            pl.BlockSpec((tm, D_MODEL), lambda i: (i, 0)),
            pl.BlockSpec((1, D_MODEL), lambda i: (0, 0)),
            pl.BlockSpec((D_MODEL, NP), lambda i: (0, 0)),
        ],
        out_specs=[pl.BlockSpec((tm, NP), lambda i: (i, 0)),
                   pl.BlockSpec((16, tm), lambda i: (0, i)),
                   pl.BlockSpec((IDX_HEADS, tm), lambda i: (0, i))],
        out_shape=[jax.ShapeDtypeStruct((m, NP), F32),
                   jax.ShapeDtypeStruct((16, m), F32),
                   jax.ShapeDtypeStruct((IDX_HEADS, m), F32)],
        compiler_params=pltpu.CompilerParams(
            dimension_semantics=("arbitrary",), vmem_limit_bytes=VMEM_LIMIT),
        name="inproj",
    )(x2, g, w)


def _ffn_kernel(x_ref, ya_ref, yb_ref, yc_ref, wo_ref, g_ref, wg_ref, wu_ref, wd_ref, fg_ref,
                o_ref, acc_ref, h_ref, *, n_hidden_steps, final_norm):
    j = pl.program_id(1)

    @pl.when(j == 0)
    def _():
        xn = x_ref[...]
        xn = xn + jnp.dot(ya_ref[...], wo_ref[0:512, :], preferred_element_type=F32)
        xn = xn + jnp.dot(yb_ref[...], wo_ref[512:768, :], preferred_element_type=F32)
        xn = xn + jnp.dot(yc_ref[...], wo_ref[768:1024, :], preferred_element_type=F32)
        acc_ref[...] = xn
        ms = jnp.mean(xn * xn, axis=-1, keepdims=True)
        h_ref[...] = ((xn * lax.rsqrt(ms + NORM_EPS)) * g_ref[...]).astype(BF16)

    h = h_ref[...]
    gate = jnp.dot(h, wg_ref[...], preferred_element_type=F32)
    up = jnp.dot(h, wu_ref[...], preferred_element_type=F32)
    act = (gate * _sigmoid(gate)) * up
    acc_ref[...] += jnp.dot(act.astype(BF16), wd_ref[...], preferred_element_type=F32)

    @pl.when(j == n_hidden_steps - 1)
    def _():
        y = acc_ref[...]
        if final_norm:
            ms = jnp.mean(y * y, axis=-1, keepdims=True)
            y = (y * lax.rsqrt(ms + NORM_EPS)) * fg_ref[...]
        o_ref[...] = y


def _out_ffn(x2, ya, yb, yc, wo, g, wg, wu, wd, fg, final_norm):
    m = x2.shape[0]
    tm = 512
    th = 2816
    nh = FFN_HIDDEN // th
    kern = functools.partial(_ffn_kernel, n_hidden_steps=nh, final_norm=final_norm)
    return pl.pallas_call(
        kern,
        grid=(m // tm, nh),
        in_specs=[
            pl.BlockSpec((tm, D_MODEL), lambda i, j: (i, 0)),
            pl.BlockSpec((tm, 512), lambda i, j: (i, 0)),
            pl.BlockSpec((tm, 256), lambda i, j: (i, 0)),
            pl.BlockSpec((tm, 256), lambda i, j: (i, 0)),
            pl.BlockSpec((D_MODEL, D_MODEL), lambda i, j: (0, 0), pipeline_mode=pl.Buffered(1)),
            pl.BlockSpec((1, D_MODEL), lambda i, j: (0, 0)),
            pl.BlockSpec((D_MODEL, th), lambda i, j: (0, j), pipeline_mode=pl.Buffered(1)),
            pl.BlockSpec((D_MODEL, th), lambda i, j: (0, j), pipeline_mode=pl.Buffered(1)),
            pl.BlockSpec((th, D_MODEL), lambda i, j: (j, 0), pipeline_mode=pl.Buffered(1)),
            pl.BlockSpec((1, D_MODEL), lambda i, j: (0, 0)),
        ],
        out_specs=pl.BlockSpec((tm, D_MODEL), lambda i, j: (i, 0)),
        out_shape=jax.ShapeDtypeStruct((m, D_MODEL), F32),
        scratch_shapes=[pltpu.VMEM((tm, D_MODEL), F32), pltpu.VMEM((tm, D_MODEL), BF16)],
        compiler_params=pltpu.CompilerParams(
            dimension_semantics=("arbitrary", "arbitrary"), vmem_limit_bytes=VMEM_LIMIT),
        name="out_ffn",
    )(x2, ya, yb, yc, wo, g, wg, wu, wd, fg)


def _cummax_rows(x, n):
    neg = -jnp.inf
    row = lax.broadcasted_iota(jnp.int32, x.shape, 0)
    s = 1
    while s < n:
        if s < 8:
            shifted = jnp.where(row >= s, pltpu.roll(x, s, axis=0), neg)
        else:
            shifted = jnp.concatenate([jnp.full((s, x.shape[1]), neg, x.dtype), x[:n - s]], axis=0)
        x = jnp.maximum(x, shifted)
        s *= 2
    return x


def _mlstm_kernel(q_ref, k_ref, v_ref, o_ref, gc_ref, gr_ref, bc_ref, br_ref, og_ref,
                  y_ref, sv_s, kv_s, b_s, dm_s, c_s, m_s, *, chunk, n_chunks, group):
    L = chunk
    t = pl.program_id(1)
    heads = range(MLSTM_HEADS)
    half = MLSTM_DQK

    @pl.when(t == 0)
    def _():
        c_s[...] = jnp.zeros_like(c_s)
        m_s[...] = jnp.zeros_like(m_s)

    incl = _tri(L, "incl")
    tril_b = incl.astype(BF16)
    triu_b = _tri(L, "upper_incl").astype(BF16)
    lane = lax.broadcasted_iota(jnp.int32, (L, LANES), 1)
    own = [(lane < half) if h % 2 == 0 else (lane >= half) for h in heads]
    row_c = lax.broadcasted_iota(jnp.int32, (LANES, LANES), 0)
    own_rows = [(row_c < half) if h % 2 == 0 else (row_c >= half) for h in heads]
    scale = MLSTM_DQK ** -0.5

    def pair(ref, r0, h):
        return ref[pl.ds(r0, L), (h // 2) * LANES:(h // 2 + 1) * LANES]

    def prep(cg, carry):
        ch = []
        span = group * L
        rows = gr_ref[:, pl.ds(pl.multiple_of(cg * span, span), span)] + br_ref[...]
        for g in range(group):
            c = cg * group + g
            r0 = pl.multiple_of(c * L, L)
            x_col = gc_ref[pl.ds(r0, L), :] + bc_ref[...]
            x_row = rows[:, g * L:(g + 1) * L]
            b_row = _cumsum_lanes(-_softplus(-x_row), triu_b)
            f_rep = jnp.concatenate(
                [jnp.broadcast_to(x_col[:, 12 + h:13 + h], (L, LANES)) for h in heads], axis=1)
            b_rep = _cumsum_rows(tril_b, -_softplus(-f_rep))
            for h in heads:
                bx = b_rep[:, h * LANES:(h + 1) * LANES]
                cx = jnp.broadcast_to(x_col[:, 8 + h:9 + h], (L, LANES)) - bx
                dmax = bx + _cummax_rows(cx, L)
                c_row = x_row[8 + h:9 + h, :] - b_row[12 + h:13 + h, :]
                d_log = jnp.where(incl, bx[:, :L] + c_row - dmax[:, :L], -jnp.inf)
                kw = jnp.exp(cx + (bx[L - 1:L, :] - dmax[L - 1:L, :]))
                kp = pair(k_ref, r0, h) * scale
                vp = pair(v_ref, r0, h)
                ch.append(dict(idx=c * MLSTM_HEADS + h, h=h,
                               q=jnp.where(own[h], pair(q_ref, r0, h), 0.0), k=kp,
                               v=jnp.where(own[h], vp, 1.0), dw=jnp.exp(d_log),
                               kxw_t=(kp * kw).T, bx=bx, dmax=dmax))
        for d in ch:
            d["s"] = _mm_nt(d["q"], d["k"]) * d["dw"]
        for d in ch:
            d["sv"] = _mm(d["s"], d["v"])
        for d in ch:
            d["kv"] = jnp.where(own_rows[d["h"]], _mm(d["kxw_t"], d["v"]), 0.0)
        for d in ch:
            sv_s[d["idx"]] = d["sv"]
            kv_s[d["idx"]] = d["kv"]
            b_s[d["idx"]] = d["bx"]
            dm_s[d["idx"]] = d["dmax"]
        return carry

    lax.fori_loop(0, n_chunks // group, prep, 0)

    def scan(c, carry):
        r0 = pl.multiple_of(c * L, L)
        c_prev = [c_s[h] for h in heads]
        qc = [_mm(pair(q_ref, r0, h), c_prev[h]) for h in heads]
        outs = []
        for h in heads:
            idx = c * MLSTM_HEADS + h
            bx = b_s[idx]
            dmax = dm_s[idx]
            m_prev = m_s[h:h + 1, :]
            a_log = bx + m_prev
            m_t = jnp.maximum(a_log, dmax)
            num = jnp.exp(a_log - m_t) * qc[h] + jnp.exp(dmax - m_t) * sv_s[idx]
            den = pltpu.roll(num, half, axis=1)
            hh = jnp.where(own[h], num / jnp.maximum(jnp.abs(den), jnp.exp(-m_t)), 0.0)
            m_new = m_t[L - 1:L, :]
            dec = jnp.exp(bx[L - 1:L, :] + m_prev - m_new)
            c_s[h] = dec * c_prev[h] + jnp.exp(dmax[L - 1:L, :] - m_new) * kv_s[idx]
            m_s[h:h + 1, :] = m_new
            ms = jnp.sum(hh * hh, axis=-1, keepdims=True) * (1.0 / MLSTM_DV)
            outs.append(hh * lax.rsqrt(ms + NORM_EPS))
        y = jnp.concatenate([outs[0] + outs[1], outs[2] + outs[3]], axis=1) * og_ref[...]
        y = y * _sigmoid(o_ref[pl.ds(r0, L), :])
        y_ref[pl.ds(r0, L), :] = y.astype(y_ref.dtype)
        return carry

    lax.fori_loop(0, n_chunks, scan, 0, unroll=2 if n_chunks % 2 == 0 else 1)


def _mlstm(proj3, gates_r, bias_c, bias_r, og, *, chunk, tb):
    b, t, _ = proj3.shape
    nc = tb // chunk
    nch = nc * MLSTM_HEADS
    wq = MLSTM_HEADS * MLSTM_DQK
    group = 8 if nc % 8 == 0 else (4 if nc % 4 == 0 else (2 if nc % 2 == 0 else 1))
    assert (group * chunk) % LANES == 0
    kern = functools.partial(_mlstm_kernel, chunk=chunk, n_chunks=nc, group=group)
    col = lambda c0: pl.BlockSpec((None, tb, wq), lambda i, j, c0=c0: (i, j, c0 // wq))
    return pl.pallas_call(
        kern,
        grid=(b, t // tb),
        in_specs=[
            col(C_MQ), col(C_MK), col(C_MV), col(C_MO),
            pl.BlockSpec((None, tb, LANES), lambda i, j: (i, j, C_GATE // LANES)),
            pl.BlockSpec((16, tb), lambda i, j: (0, i * (t // tb) + j)),
            pl.BlockSpec((1, LANES), lambda i, j: (0, 0)),
            pl.BlockSpec((16, 1), lambda i, j: (0, 0)),
            pl.BlockSpec((1, wq), lambda i, j: (0, 0)),
        ],
        out_specs=pl.BlockSpec((None, tb, wq), lambda i, j: (i, j, 0)),
        out_shape=jax.ShapeDtypeStruct((b, t, wq), BF16),
        scratch_shapes=[pltpu.VMEM((nch, chunk, LANES), F32),
                        pltpu.VMEM((nch, LANES, LANES), F32),
                        pltpu.VMEM((nch, chunk, LANES), F32),
                        pltpu.VMEM((nch, chunk, LANES), F32),
                        pltpu.VMEM((MLSTM_HEADS, LANES, LANES), F32),
                        pltpu.VMEM((8, LANES), F32)],
        compiler_params=pltpu.CompilerParams(
            dimension_semantics=("arbitrary", "arbitrary"), vmem_limit_bytes=VMEM_LIMIT),
        name="mlstm",
    )(proj3, proj3, proj3, proj3, proj3, gates_r, bias_c, bias_r, og)


def _split3(x):
    x1 = x.astype(BF16)
    r1 = x - x1.astype(F32)
    x2 = r1.astype(BF16)
    x3 = (r1 - x2.astype(F32)).astype(BF16)
    return x1, x2, x3


def _cumsum_rows(tril_b, x):
    return sum(jnp.dot(tril_b, p, preferred_element_type=F32) for p in _split3(x))


def _cumsum_lanes(x, triu_b):
    return sum(jnp.dot(p, triu_b, preferred_element_type=F32) for p in _split3(x))


def _inverse_masks(n):
    r = lax.broadcasted_iota(jnp.int32, (n, n), 0)
    c = lax.broadcasted_iota(jnp.int32, (n, n), 1)
    offs = []
    lvl = 0
    while (1 << lvl) < n:
        rb = r >> lvl
        cb = c >> lvl
        offs.append(((rb & 1) == 1) & (cb == rb - 1))
        lvl += 1
    return (r == c).astype(F32), offs


def _gdn_kernel(q_ref, k_ref, v_ref, z_ref, qh_ref, kh_ref, vh_ref, gc_ref, gr_ref, cw_ref,
                pc_ref, pr_ref, og_ref, y_ref, qs, ks, vs, xp_s, u_s, w_s, qk_s, qd_s, kdt_s, el_s, s_s,
                *, chunk, n_chunks, tb, group):
    C = chunk
    t = pl.program_id(1)
    width = GDN_HEADS * GDN_DK

    @pl.when(t == 0)
    def _():
        s_s[...] = jnp.zeros_like(s_s)

    def conv_silu(x_ref, halo_ref, w0):
        xp_s[0:8, :] = jnp.where(t == 0, 0.0, halo_ref[...])
        xp_s[8:, :] = x_ref[...]
        y = xp_s[8:, :] * cw_ref[GDN_CONV - 1:GDN_CONV, w0:w0 + width]
        for j in range(GDN_CONV - 1):
            y = y + xp_s[pl.ds(8 - (GDN_CONV - 1) + j, tb), :] * cw_ref[j:j + 1, w0:w0 + width]
        return y * _sigmoid(y)

    def l2n(x):
        parts = []
        for h in range(GDN_HEADS):
            xh = x[:, h * GDN_DK:(h + 1) * GDN_DK]
            parts.append(xh * lax.rsqrt(jnp.sum(xh * xh, axis=-1, keepdims=True) + NORM_EPS))
        return jnp.concatenate(parts, axis=1)

    qs[...] = l2n(conv_silu(q_ref, qh_ref, 0)) * (GDN_DK ** -0.5)
    ks[...] = l2n(conv_silu(k_ref, kh_ref, width))
    vs[...] = conv_silu(v_ref, vh_ref, 2 * width)

    incl = _tri(C, "incl")
    strict = _tri(C, "strict")
    tril_b = incl.astype(BF16)
    triu_b = _tri(C, "upper_incl").astype(BF16)
    inv_masks = _inverse_masks(C)

    def prep(cg, carry):
        loaded = []
        for g in range(group):
            c = cg * group + g
            r0 = pl.multiple_of(c * C, C)
            gcol = gc_ref[pl.ds(r0, C), :]
            grow = gr_ref[:, pl.ds(r0, C)]
            qkv = [(qs[pl.ds(r0, C), h * GDN_DK:(h + 1) * GDN_DK],
                    ks[pl.ds(r0, C), h * GDN_DK:(h + 1) * GDN_DK],
                    vs[pl.ds(r0, C), h * GDN_DK:(h + 1) * GDN_DK]) for h in range(GDN_HEADS)]
            loaded.append((c, gcol, grow, qkv))
        ch = []
        for c, gcol, grow, qkv in loaded:
            g_col = -jnp.exp(pc_ref[0:1, :]) * _softplus(gcol + pc_ref[1:2, :])
            gcum_col = _cumsum_rows(tril_b, g_col)
            beta_col = _sigmoid(gcol)
            g_row = -jnp.exp(pr_ref[:, 0:1]) * _softplus(grow + pr_ref[:, 1:2])
            gcum_row = _cumsum_lanes(g_row, triu_b)
            for h in range(GDN_HEADS):
                q, k, v = qkv[h]
                gc_c = gcum_col[:, h:h + 1]
                gc_r = gcum_row[h:h + 1, :]
                beta = beta_col[:, 4 + h:5 + h]
                decay = jnp.where(incl, jnp.exp(jnp.where(incl, gc_c - gc_r, 0.0)), 0.0)
                ch.append(dict(idx=c * GDN_HEADS + h, q=q, k=k, v=v, gc_c=gc_c, beta=beta,
                               decay=decay, kb=k * beta))
        for d in ch:
            d["a"] = jnp.where(strict, _mm_nt(d["kb"], d["k"]) * d["decay"], 0.0)
        eye, offs = inv_masks
        for d in ch:
            d["t"] = eye - jnp.where(offs[0], d["a"], 0.0)
        for off in offs[1:]:
            for d in ch:
                d["p"] = _mm(d["t"], jnp.where(off, d["a"], 0.0))
            for d in ch:
                d["t"] = d["t"] - _mm(d["p"], d["t"])
        for d in ch:
            d["egc"] = jnp.exp(d["gc_c"])
            d["u"] = _mm(d["t"], d["v"] * d["beta"])
        for d in ch:
            d["w"] = _mm(d["t"], d["kb"] * d["egc"]).astype(BF16)
        for d in ch:
            d["qk"] = (_mm_nt(d["q"], d["k"]) * d["decay"]).astype(BF16)
        for d in ch:
            idx = d["idx"]
            g_last = d["gc_c"][C - 1:C, :]
            u_s[idx] = d["u"]
            w_s[idx] = d["w"]
            qk_s[idx] = d["qk"]
            qd_s[idx] = (d["q"] * d["egc"]).astype(BF16)
            kdt_s[idx] = (d["k"] * jnp.exp(g_last - d["gc_c"])).T.astype(BF16)
            el_s[idx] = jnp.broadcast_to(jnp.exp(g_last), (8, LANES))
        return carry

    lax.fori_loop(0, n_chunks // group, prep, 0)

    def scan(c, carry):
        r0 = pl.multiple_of(c * C, C)
        heads = range(GDN_HEADS)
        dot = functools.partial(jnp.dot, preferred_element_type=F32)
        idx = [c * GDN_HEADS + h for h in heads]
        s_prev = [s_s[h] for h in heads]
        s_b = [s.astype(BF16) for s in s_prev]
        ws = [dot(w_s[idx[h]], s_b[h]) for h in heads]
        qs_ = [dot(qd_s[idx[h]], s_b[h]) for h in heads]
        v_b = [(u_s[idx[h]] - ws[h]).astype(BF16) for h in heads]
        upd = [dot(kdt_s[idx[h]], v_b[h]) for h in heads]
        o_all = [qs_[h] + dot(qk_s[idx[h]], v_b[h]) for h in heads]
        outs = []
        for h in heads:
            s_s[h] = s_prev[h] * el_s[idx[h]][0:1, :] + upd[h]
            o = o_all[h]
            ms = jnp.mean(o * o, axis=-1, keepdims=True)
            outs.append(o * lax.rsqrt(ms + NORM_EPS))
        y = jnp.concatenate(outs, axis=1) * og_ref[...]
        z = z_ref[pl.ds(r0, C), :]
        y = y * (z * _sigmoid(z))
        y_ref[pl.ds(r0, C), :] = y.astype(y_ref.dtype)
        return carry

    lax.fori_loop(0, n_chunks, scan, 0, unroll=True)


def _gdn(proj3, gates_r, conv_w, par_c, par_r, og, *, chunk, tb):
    b, t, _ = proj3.shape
    nc = tb // chunk
    nch = nc * GDN_HEADS
    width = GDN_HEADS * GDN_DK
    group = 4 if nc % 4 == 0 else (2 if nc % 2 == 0 else 1)
    assert chunk % LANES == 0
    kern = functools.partial(_gdn_kernel, chunk=chunk, n_chunks=nc, tb=tb, group=group)
    col = lambda c0: pl.BlockSpec((None, tb, width), lambda i, j, c0=c0: (i, j, c0 // width))
    halo = lambda c0: pl.BlockSpec(
        (None, 8, width), lambda i, j, c0=c0: (i, jnp.maximum(j * (tb // 8) - 1, 0), c0 // width))
    return pl.pallas_call(
        kern,
        grid=(b, t // tb),
        in_specs=[
            col(C_GQ), col(C_GK), col(C_GV), col(C_GZ),
            halo(C_GQ), halo(C_GK), halo(C_GV),
            pl.BlockSpec((None, tb, LANES), lambda i, j: (i, j, C_GATE // LANES)),
            pl.BlockSpec((16, tb), lambda i, j: (0, i * (t // tb) + j)),
            pl.BlockSpec((GDN_CONV, 3 * width), lambda i, j: (0, 0)),
            pl.BlockSpec((2, LANES), lambda i, j: (0, 0)),
            pl.BlockSpec((16, 2), lambda i, j: (0, 0)),
            pl.BlockSpec((1, width), lambda i, j: (0, 0)),
        ],
        out_specs=pl.BlockSpec((None, tb, width), lambda i, j: (i, j, 0)),
        out_shape=jax.ShapeDtypeStruct((b, t, width), BF16),
        scratch_shapes=[pltpu.VMEM((tb, width), F32), pltpu.VMEM((tb, width), F32),
                        pltpu.VMEM((tb, width), F32),
                        pltpu.VMEM((tb + 8, width), F32),
                        pltpu.VMEM((nch, chunk, GDN_DV), F32),
                        pltpu.VMEM((nch, chunk, GDN_DK), BF16),
                        pltpu.VMEM((nch, chunk, chunk), BF16),
                        pltpu.VMEM((nch, chunk, GDN_DK), BF16),
                        pltpu.VMEM((nch, GDN_DK, chunk), BF16),
                        pltpu.VMEM((nch, 8, LANES), F32),
                        pltpu.VMEM((GDN_HEADS, GDN_DK, GDN_DV), F32)],
        compiler_params=pltpu.CompilerParams(
            dimension_semantics=("arbitrary", "arbitrary"), vmem_limit_bytes=VMEM_LIMIT),
        name="gdn",
    )(proj3, proj3, proj3, proj3, proj3, proj3, proj3, proj3, gates_r, conv_w, par_c, par_r, og)


def _dsa_prep_kernel(cq_ref, ckv_ref, idx_ref, qn_ref, kn_ref, wuq_ref, wqi_ref, wuk_ref,
                     qlat_ref, qidx_ref, ckvn_ref, ckvt_ref, kidx_ref):
    cq = cq_ref[...]
    ms = jnp.mean(cq * cq, axis=-1, keepdims=True)
    cqn = ((cq * lax.rsqrt(ms + NORM_EPS)) * qn_ref[...]).astype(BF16)
    ckv = ckv_ref[...]
    ms = jnp.mean(ckv * ckv, axis=-1, keepdims=True)
    ckvn = (ckv * lax.rsqrt(ms + NORM_EPS)) * kn_ref[...]
    ckvn_ref[...] = ckvn.astype(BF16)
    ckvt_ref[...] = ckvn.T.astype(BF16)
    q = jnp.dot(cqn, wuq_ref[...], preferred_element_type=F32)
    qlat = jnp.dot(q.astype(BF16), wuk_ref[...], preferred_element_type=F32) * (DSA_HEAD_DIM ** -0.5)
    qlat_ref[...] = qlat.astype(BF16)
    qidx_ref[...] = jnp.dot(cqn, wqi_ref[...], preferred_element_type=F32).astype(BF16)
    kidx_ref[...] = idx_ref[:, 0:IDX_DIM].astype(BF16)


def _dsa_prep(proj2, qn, kn, wuq, wqi, wuk_bd):
    m = proj2.shape[0]
    tm = min(1024, m)
    full = lambda shape: pl.BlockSpec(shape, lambda i: (0, 0))
    return pl.pallas_call(
        _dsa_prep_kernel,
        grid=(m // tm,),
        in_specs=[
            pl.BlockSpec((tm, DSA_Q_RANK), lambda i: (i, C_CQ // DSA_Q_RANK)),
            pl.BlockSpec((tm, DSA_KV_RANK), lambda i: (i, C_CKV // DSA_KV_RANK)),
            pl.BlockSpec((tm, LANES), lambda i: (i, C_IDX // LANES)),
            full((1, DSA_Q_RANK)), full((1, DSA_KV_RANK)),
            full((DSA_Q_RANK, DSA_HEADS * DSA_HEAD_DIM)),
            full((DSA_Q_RANK, IDX_HEADS * IDX_DIM)),
            full((DSA_HEADS * DSA_HEAD_DIM, DSA_HEADS * DSA_KV_RANK)),
        ],
        out_specs=[
            pl.BlockSpec((tm, DSA_HEADS * DSA_KV_RANK), lambda i: (i, 0)),
            pl.BlockSpec((tm, IDX_HEADS * IDX_DIM), lambda i: (i, 0)),
            pl.BlockSpec((tm, DSA_KV_RANK), lambda i: (i, 0)),
            pl.BlockSpec((DSA_KV_RANK, tm), lambda i: (0, i)),
            pl.BlockSpec((tm, IDX_DIM), lambda i: (i, 0)),
        ],
        out_shape=[
            jax.ShapeDtypeStruct((m, DSA_HEADS * DSA_KV_RANK), BF16),
            jax.ShapeDtypeStruct((m, IDX_HEADS * IDX_DIM), BF16),
            jax.ShapeDtypeStruct((m, DSA_KV_RANK), BF16),
            jax.ShapeDtypeStruct((DSA_KV_RANK, m), BF16),
            jax.ShapeDtypeStruct((m, IDX_DIM), BF16),
        ],
        compiler_params=pltpu.CompilerParams(
            dimension_semantics=("arbitrary",), vmem_limit_bytes=VMEM_LIMIT),
        name="dsa_prep",
    )(proj2, proj2, proj2, qn, kn, wuq, wqi, wuk_bd)


def _tree_sum(parts):
    while len(parts) > 1:
        parts = [parts[j] + parts[j + 1] for j in range(0, len(parts) - 1, 2)] + (
            [parts[-1]] if len(parts) % 2 else [])
    return parts[0]


def _dsa_kernel(qidx_ref, wrow_ref, qlat_ref, kidx_ref, ckv_ref, ckvt_ref, wuvt_ref, y_ref,
                key_s, hi_s, lo_s, cut_s, acc_s, *, qb, tk, n_sel):
    i = pl.program_id(1)
    n_tiles = (i * qb + qb + tk - 1) // tk
    key_pos = lax.broadcasted_iota(jnp.int32, (tk, qb), 0)
    q_pos = i * qb + lax.broadcasted_iota(jnp.int32, (tk, qb), 1)
    w_rows = wrow_ref[...] * (IDX_HEADS ** -0.5 * IDX_DIM ** -0.5)
    qidx = qidx_ref[...]
    q_heads = [qidx[:, h * IDX_DIM:(h + 1) * IDX_DIM] for h in range(IDX_HEADS)]
    i16_min = -(1 << 15)
    i16_max = (1 << 15) - 1
    sub16 = 16
    nt_dims = (((1,), (1,)), ((), ()))

    def score_body(kt, carry):
        k0 = pl.multiple_of(kt * tk, tk)
        kk = kidx_ref[pl.ds(k0, tk), :]
        sh = [lax.dot_general(kk, q_heads[h], nt_dims, preferred_element_type=F32)
              for h in range(IDX_HEADS)]
        acc = jnp.zeros((tk, qb), F32)
        for h in range(IDX_HEADS):
            acc = acc + jnp.maximum(sh[h], 0.0) * w_rows[h:h + 1, :]
        acc = acc + 0.0
        bits = pltpu.bitcast(acc, jnp.int32)
        keys = jnp.where(bits < 0, bits ^ jnp.int32(0x7FFFFFFF), bits)
        keys = jnp.where(k0 + key_pos <= q_pos, keys, jnp.int32(INT_MIN))
        key_s[kt] = keys
        hi_s[kt] = (keys >> 16).astype(jnp.int16)
        return carry

    lax.fori_loop(0, n_tiles, score_body, 0)

    def count_ge(src_s, cand):
        cand16 = jnp.broadcast_to(cand, (sub16, qb)).astype(jnp.int16)
        one, zero = jnp.int16(1), jnp.int16(0)

        def body(kt, acc):
            tile = src_s[kt]
            return acc + _tree_sum([jnp.where(tile[j * sub16:(j + 1) * sub16, :] >= cand16, one, zero)
                                    for j in range(tk // sub16)])

        cnt = lax.fori_loop(0, n_tiles, body, jnp.zeros((sub16, qb), jnp.int16))
        return jnp.sum(cnt.astype(jnp.int32), axis=0, keepdims=True)

    def kth_largest16(src_s, rank):
        thr = jnp.full((1, qb), i16_min, jnp.int32)
        n_ge = jnp.zeros((1, qb), jnp.int32) + n_tiles * tk
        n_gt = jnp.zeros((1, qb), jnp.int32)
        for bit in range(15, -1, -1):
            cand = thr + jnp.int32(1 << bit)
            cnt = count_ge(src_s, cand)
            ok = cnt >= rank
            thr = jnp.where(ok, cand, thr)
            n_ge = jnp.where(ok, cnt, n_ge)
            n_gt = jnp.where(ok, n_gt, cnt)
        return thr, n_ge, n_gt

    hi_thr, hi_ge, above = kth_largest16(hi_s, jnp.int32(n_sel))
    rank_lo = jnp.int32(n_sel) - above

    def lo_body(kt, carry):
        keys = key_s[kt]
        lo = (keys & jnp.int32(0xFFFF)) + jnp.int32(i16_min)
        lo_s[kt] = jnp.where((keys >> 16) == hi_thr, lo, jnp.int32(i16_min)).astype(jnp.int16)
        return carry

    lax.fori_loop(0, n_tiles, lo_body, 0)
    lo_thr, lo_ge, _ = kth_largest16(lo_s, rank_lo)
    thr_raw = hi_thr * jnp.int32(1 << 16) + (lo_thr - jnp.int32(i16_min))
    thr = jnp.maximum(thr_raw, jnp.int32(INT_MIN + 1))

    in_bucket = jnp.where(lo_thr > i16_min, lo_ge, hi_ge - above)
    excess = jnp.where(thr_raw > INT_MIN, above + in_bucket - jnp.int32(n_sel), 0)

    max_excess = jnp.max(excess)
    last = tk * key_s.shape[0] - 1
    assert last <= i16_max
    direct_limit = max(last, 1).bit_length()

    @pl.when(max_excess > 0)
    def _():
        def fill(kt, carry):
            rev = jnp.int32(last) - (kt * tk + key_pos)
            lo_s[kt] = jnp.where(key_s[kt] == thr_raw, rev, jnp.int32(-1)).astype(jnp.int16)
            return carry

        lax.fori_loop(0, n_tiles, fill, 0)

        @pl.when(max_excess <= direct_limit)
        def _():
            def next_tied(prev):
                prev16 = jnp.broadcast_to(prev, (sub16, qb)).astype(jnp.int16)
                big = jnp.int16(i16_max)

                def body(kt, acc):
                    tile = lo_s[kt]
                    parts = [jnp.where(tile[j * sub16:(j + 1) * sub16, :] > prev16,
                                       tile[j * sub16:(j + 1) * sub16, :], big)
                             for j in range(tk // sub16)]
                    while len(parts) > 1:
                        parts = [jnp.where(parts[j] < parts[j + 1], parts[j], parts[j + 1])
                                 for j in range(0, len(parts), 2)]
                    return jnp.where(acc < parts[0], acc, parts[0])

                low = lax.fori_loop(0, n_tiles, body, jnp.full((sub16, qb), i16_max, jnp.int16))
                return jnp.min(low.astype(jnp.int32), axis=0, keepdims=True)

            def step(j, carry):
                prev, cut = carry
                cur = next_tied(prev)
                return cur, jnp.where(j < excess, cur + 1, cut)

            _, cut = lax.fori_loop(0, max_excess, step,
                                   (jnp.full((1, qb), -1, jnp.int32), jnp.zeros((1, qb), jnp.int32)))
            cut_s[0:1, :] = cut

        @pl.when(max_excess > direct_limit)
        def _():
            keep = count_ge(lo_s, jnp.zeros((1, qb), jnp.int32)) - excess
            cut = jnp.zeros((1, qb), jnp.int32)
            for bit in range(direct_limit - 1, -1, -1):
                cand = cut + jnp.int32(1 << bit)
                cut = jnp.where(count_ge(lo_s, cand) >= keep, cand, cut)
            cut_s[0:1, :] = jnp.where(excess > 0, cut, 0)

        cut = cut_s[0:1, :]

        def demote(kt, carry):
            keys = key_s[kt]
            rev = jnp.int32(last) - (kt * tk + key_pos)
            demoted = jnp.where(rev < cut, jnp.int32(INT_MIN), keys)
            key_s[kt] = jnp.where(keys == thr_raw, demoted, keys)
            return carry

        lax.fori_loop(0, n_tiles, demote, 0)

    qlat = qlat_ref[...]
    heads = range(DSA_HEADS)
    ql = [qlat[:, h * DSA_KV_RANK:(h + 1) * DSA_KV_RANK] for h in heads]
    acc_s[...] = jnp.zeros_like(acc_s)

    def attn_body(kt, carry):
        m_run, l_run = carry
        k0 = pl.multiple_of(kt * tk, tk)
        ckv = ckv_ref[pl.ds(k0, tk), :]
        ckvt = ckvt_ref[:, pl.ds(k0, tk)]
        bias = jnp.where(key_s[kt] >= thr, 0.0, -jnp.inf)
        lg = [lax.dot_general(ckv, ql[h], nt_dims, preferred_element_type=F32) for h in heads]
        m_out, l_out, p_all, alphas = [], [], [], []
        for h in heads:
            lgm = lg[h] + bias
            m_new = jnp.maximum(m_run[h], jnp.max(lgm, axis=0, keepdims=True))
            p = jnp.exp(lgm - m_new)
            alpha = jnp.exp(m_run[h] - m_new)
            l_out.append(alpha * l_run[h] + jnp.sum(p, axis=0, keepdims=True))
            m_out.append(m_new)
            p_all.append(p.astype(BF16))
            alphas.append(alpha)
        pv = [jnp.dot(ckvt, p_all[h], preferred_element_type=F32) for h in heads]
        for h in heads:
            acc_s[h] = alphas[h] * acc_s[h] + pv[h]
        return tuple(m_out), tuple(l_out)

    init = (tuple(jnp.full((1, qb), -1e30, F32) for _ in heads),
            tuple(jnp.zeros((1, qb), F32) for _ in heads))
    _, l_fin = lax.fori_loop(0, n_tiles, attn_body, init)
    o_lat = [(acc_s[h] / l_fin[h]).astype(BF16) for h in heads]
    outs = [jnp.dot(wuvt_ref[h], o_lat[h], preferred_element_type=F32) for h in heads]
    y_ref[...] = jnp.concatenate(outs, axis=0).T.astype(y_ref.dtype)


def _dsa(qidx, widx_rows, qlat, kidx, ckvn, ckvt, wuvt, *, qb, tk):
    b, t, _ = qidx.shape
    n_sel = min(IDX_TOPK, t // 4)
    kern = functools.partial(_dsa_kernel, qb=qb, tk=tk, n_sel=n_sel)
    return pl.pallas_call(
        kern,
        grid=(b, t // qb),
        in_specs=[
            pl.BlockSpec((None, qb, IDX_HEADS * IDX_DIM), lambda i, j: (i, j, 0)),
            pl.BlockSpec((IDX_HEADS, qb), lambda i, j: (0, i * (t // qb) + j)),
            pl.BlockSpec((None, qb, DSA_HEADS * DSA_KV_RANK), lambda i, j: (i, j, 0)),
            pl.BlockSpec((None, t, IDX_DIM), lambda i, j: (i, 0, 0)),
            pl.BlockSpec((None, t, DSA_KV_RANK), lambda i, j: (i, 0, 0)),
            pl.BlockSpec((DSA_KV_RANK, t), lambda i, j: (0, i)),
            pl.BlockSpec((DSA_HEADS, DSA_HEAD_DIM, DSA_KV_RANK), lambda i, j: (0, 0, 0)),
        ],
        out_specs=pl.BlockSpec((None, qb, DSA_HEADS * DSA_HEAD_DIM), lambda i, j: (i, j, 0)),
        out_shape=jax.ShapeDtypeStruct((b, t, DSA_HEADS * DSA_HEAD_DIM), BF16),
        scratch_shapes=[pltpu.VMEM((t // tk, tk, qb), jnp.int32),
                        pltpu.VMEM((t // tk, tk, qb), jnp.int16),
                        pltpu.VMEM((t // tk, tk, qb), jnp.int16),
                        pltpu.VMEM((8, qb), jnp.int32),
                        pltpu.VMEM((DSA_HEADS, DSA_KV_RANK, qb), F32)],
        compiler_params=pltpu.CompilerParams(
            dimension_semantics=("arbitrary", "arbitrary"), vmem_limit_bytes=VMEM_LIMIT),
        name="dsa",
    )(qidx, widx_rows, qlat, kidx, ckvn, ckvt, wuvt)


def _pack_w_in(w):
    pts = [0]
    for wd in IN_WIDTHS:
        pts.append(pts[-1] + wd)
    seg = [w[..., pts[i]:pts[i + 1]] for i in range(len(IN_WIDTHS))]
    (gq, gk, gv, gz, ga, gb, cq, ckv, ik, iw, mq, mk, mv, mo, mi, mf) = seg
    zeros = lambda n: jnp.zeros(w.shape[:-1] + (n,), w.dtype)
    idx_blk = jnp.concatenate([ik, iw, zeros(LANES - IDX_DIM - IDX_HEADS)], axis=-1)
    gate_blk = jnp.concatenate([ga, gb, mi, mf, zeros(LANES - 16)], axis=-1)
    out = jnp.concatenate([gq, gk, gv, gz, cq, ckv, idx_blk, mq, mk, mv, mo, gate_blk], axis=-1)
    assert out.shape[-1] == NP
    return out.astype(BF16)


def _lane_rows(vals_at, depth):
    rows = jnp.zeros((depth, LANES), F32)
    for start, v in vals_at:
        rows = rows.at[:, start:start + v.shape[1]].set(v.astype(F32))
    return rows


def kernel(x, attn_norm, w_in, gdn_conv, gdn_a_log, gdn_dt_bias, gdn_out_norm, dsa_q_norm,
           dsa_kv_norm, dsa_w_uq, dsa_w_qidx, dsa_w_uk, dsa_w_uv, mlstm_i_bias, mlstm_f_bias,
           mlstm_out_norm, w_out, ffn_norm, w_gate, w_up, w_down, final_norm):
    b, t, d = x.shape
    depth = w_in.shape[0]
    gdn_chunk, mlstm_chunk = min(128, t), min(128, t)
    tb = min(512, t)
    qb = min(512, t)
    tk = min(512, t)

    w_in_p = _pack_w_in(w_in)
    gdn_par_c = jnp.stack([_lane_rows([(0, gdn_a_log)], depth), _lane_rows([(0, gdn_dt_bias)], depth)],
                          axis=1)
    gdn_par_r = jnp.swapaxes(gdn_par_c[:, :, :16], 1, 2)
    gdn_og = jnp.tile(gdn_out_norm, (1, GDN_HEADS))
    wuk_bd = jnp.zeros((depth, DSA_HEADS * DSA_HEAD_DIM, DSA_HEADS * DSA_KV_RANK), F32)
    for h in range(DSA_HEADS):
        wuk_bd = wuk_bd.at[:, h * DSA_HEAD_DIM:(h + 1) * DSA_HEAD_DIM,
                           h * DSA_KV_RANK:(h + 1) * DSA_KV_RANK].set(jnp.swapaxes(dsa_w_uk[:, h], 1, 2))
    wuk_bd = wuk_bd.astype(BF16)
    wuq_b, wqi_b = dsa_w_uq.astype(BF16), dsa_w_qidx.astype(BF16)
    wuvt_b = jnp.swapaxes(dsa_w_uv, 2, 3).astype(BF16)
    ml_bias_c = _lane_rows([(8, mlstm_i_bias), (12, mlstm_f_bias)], depth)
    ml_og = jnp.tile(mlstm_out_norm, (1, MLSTM_HEADS))
    w_out_b, w_gate_b, w_up_b, w_down_b = (a.astype(BF16) for a in (w_out, w_gate, w_up, w_down))

    x2 = x.reshape(b * t, d)
    for l in range(depth):
        proj2, gates_t, widx_rows = _inproj(x2, attn_norm[l].reshape(1, d), w_in_p[l])
        proj3 = proj2.reshape(b, t, NP)

        y_a = _gdn(proj3, gates_t, gdn_conv[l], gdn_par_c[l], gdn_par_r[l],
                   gdn_og[l].reshape(1, -1), chunk=gdn_chunk, tb=tb)

        qlat, qidx, ckvn, ckvt, kidx = _dsa_prep(
            proj2, dsa_q_norm[l].reshape(1, -1), dsa_kv_norm[l].reshape(1, -1),
            wuq_b[l], wqi_b[l], wuk_bd[l])
        r3 = lambda a: a.reshape(b, t, a.shape[-1])
        y_b = _dsa(r3(qidx), widx_rows, r3(qlat), r3(kidx), r3(ckvn), ckvt, wuvt_b[l], qb=qb, tk=tk)

        y_c = _mlstm(proj3, gates_t, ml_bias_c[l].reshape(1, LANES),
                     ml_bias_c[l, :16].reshape(16, 1), ml_og[l].reshape(1, -1),
                     chunk=mlstm_chunk, tb=tb)

        x2 = _out_ffn(x2, y_a.reshape(b * t, -1), y_b.reshape(b * t, -1), y_c.reshape(b * t, -1),
                      w_out_b[l], ffn_norm[l].reshape(1, d), w_gate_b[l], w_up_b[l], w_down_b[l],
                      final_norm.reshape(1, d), final_norm=(l == depth - 1))
    return x2.reshape(b, t, d)
```
